```python
import jax, jax.numpy as jnp
from jax import lax
import numpy as np

D_MODEL = 1024
BATCH = 8
SEQ = 4096
DEPTH = 4
DEC_BATCH = 8
DEC_SEQ = 8192
PAST_LEN = 128

HEAD_DIM = 64
DILATED_PATTERNS = ((128, 1), (512, 4), (2048, 16))
N_DIL = 3
ATT_HEADS = 8
ATT_WIDTH = ATT_HEADS * HEAD_DIM
ATT_PROJ = N_DIL * 3 * ATT_WIDTH
ATT_BLOCK = 64
ROPE_THETA = 10000.0
SSD_HEADS = 8
SSD_WIDTH = SSD_HEADS * HEAD_DIM
SSD_GROUPS = 2
D_STATE = 128
CONV_K = 5
CONV_DIM = SSD_WIDTH + 2 * SSD_GROUPS * D_STATE
SSD_CHUNK = 128
IN_DIM = ATT_PROJ + SSD_WIDTH + CONV_DIM + 2 * SSD_HEADS
MIX_WIDTH = ATT_WIDTH + SSD_WIDTH
N_EXPERT_GROUPS = 4
EXPERTS_PER_GROUP = 8
N_EXPERTS = N_EXPERT_GROUPS * EXPERTS_PER_GROUP
TOP_K = 2
EXPERT_FF = 512
MOE_BLOCK = 256
DN_ALPHA = (2 * DEPTH) ** 0.25
DN_BETA = (8 * DEPTH) ** -0.25
LN_EPS = 1e-5
RMS_EPS = 1e-5
NEG = -1e30

kernel_name = 'hymba_ssd_dilated_hmoe_encoder'

F32 = jnp.float32


def layer_norm(x, g, b):
    xf = x.astype(F32)
    mu = xf.mean(-1, keepdims=True)
    var = jnp.square(xf - mu).mean(-1, keepdims=True)
    return ((xf - mu) * lax.rsqrt(var + LN_EPS) * g.astype(F32) + b.astype(F32)).astype(x.dtype)


def rope(t, pos):
    half = HEAD_DIM // 2
    inv = ROPE_THETA ** (-jnp.arange(half, dtype=F32) / half)
    ang = pos.astype(F32)[:, None] * inv[None, :]
    cos = jnp.cos(ang)[None, :, None, :]
    sin = jnp.sin(ang)[None, :, None, :]
    tf = t.astype(F32)
    t1, t2 = tf[..., :half], tf[..., half:]
    return jnp.concatenate([t1 * cos - t2 * sin, t2 * cos + t1 * sin], -1).astype(t.dtype)


def to_sub(t, dil):
    b, s = t.shape[:2]
    L = s // dil
    t = t.reshape((b, L, dil) + t.shape[2:])
    t = jnp.swapaxes(t, 1, 2)
    return t.reshape((b * dil, L) + t.shape[3:])


def from_sub(t, b, dil, L):
    t = t[:, :L]
    t = t.reshape((b, dil, L) + t.shape[2:])
    t = jnp.swapaxes(t, 1, 2)
    return t.reshape((b, dil * L) + t.shape[3:])


def dilated_window_attn(q, k, v, dil, half):
    b, s, h, e = q.shape
    L = s // dil
    nb = -(-L // ATT_BLOCK)
    Lp = nb * ATT_BLOCK
    qs, ks, vs = to_sub(q, dil), to_sub(k, dil), to_sub(v, dil)
    bd = qs.shape[0]
    qb = jnp.pad(qs, ((0, 0), (0, Lp - L), (0, 0), (0, 0))).reshape(bd, nb, ATT_BLOCK, h, e)
    kpad = ((0, 0), (ATT_BLOCK, Lp - L + ATT_BLOCK), (0, 0), (0, 0))
    kp = jnp.pad(ks, kpad).reshape(bd, nb + 2, ATT_BLOCK, h, e)
    vp = jnp.pad(vs, kpad).reshape(bd, nb + 2, ATT_BLOCK, h, e)
    kw = jnp.concatenate([kp[:, :-2], kp[:, 1:-1], kp[:, 2:]], axis=2)
    vw = jnp.concatenate([vp[:, :-2], vp[:, 1:-1], vp[:, 2:]], axis=2)
    scores = jnp.einsum('bnqhe,bnkhe->bnhqk', qb, kw).astype(F32) * (HEAD_DIM ** -0.5)
    qpos = jnp.arange(nb)[:, None] * ATT_BLOCK + jnp.arange(ATT_BLOCK)[None, :]
    kpos = jnp.arange(nb)[:, None] * ATT_BLOCK - ATT_BLOCK + jnp.arange(3 * ATT_BLOCK)[None, :]
    valid = (jnp.abs(qpos[:, :, None] - kpos[:, None, :]) <= half) & (kpos[:, None, :] >= 0) & (kpos[:, None, :] < L)
    scores = jnp.where(valid[None, :, None], scores, NEG)
    m = scores.max(-1)
    p = jnp.exp(scores - m[..., None])
    l = p.sum(-1)
    o = jnp.einsum('bnhqk,bnkhe->bnqhe', p, vw.astype(F32))
    o = o / jnp.swapaxes(l, 2, 3)[..., None]
    o = from_sub(o.reshape(bd, Lp, h, e), b, dil, L)
    m = from_sub(jnp.swapaxes(m, 2, 3).reshape(bd, Lp, h), b, dil, L)
    l = from_sub(jnp.swapaxes(l, 2, 3).reshape(bd, Lp, h), b, dil, L)
    return o, m, l


def centred_depthwise_conv(x, w, bias):
    pad = (CONV_K - 1) // 2
    y = lax.conv_general_dilated(x, w[:, None, :], (1,), [(pad, pad)],
                                 dimension_numbers=('NWC', 'WIO', 'NWC'),
                                 feature_group_count=x.shape[-1])
    return y + bias


def ssd_chunked(x, dt, A, Bm, Cm):
    b, l, h, p = x.shape
    g, n = Bm.shape[2], Bm.shape[3]
    r = h // g
    c = l // SSD_CHUNK
    q = SSD_CHUNK
    a = (dt * A).reshape(b, c, q, h).transpose(0, 3, 1, 2)
    a_cum = jnp.cumsum(a, axis=-1)
    xdt = (x.astype(F32) * dt[..., None]).reshape(b, c, q, g, r, p)
    Bc = Bm.astype(F32).reshape(b, c, q, g, n)
    Cc = Cm.astype(F32).reshape(b, c, q, g, n)
    seg = a_cum[..., :, None] - a_cum[..., None, :]
    lower = jnp.tril(jnp.ones((q, q), dtype=bool))
    Lmat = jnp.where(lower, jnp.exp(jnp.where(lower, seg, 0.0)), 0.0).reshape(b, g, r, c, q, q)
    cb = jnp.einsum('bcqgn,bckgn->bcgqk', Cc, Bc)
    y_diag = jnp.einsum('bcgqk,bgrcqk,bckgrp->bcqgrp', cb, Lmat, xdt)
    decay_states = jnp.exp(a_cum[..., -1:] - a_cum).reshape(b, g, r, c, q)
    states = jnp.einsum('bckgn,bgrck,bckgrp->bcgrpn', Bc, decay_states, xdt)
    chunk_decay = jnp.moveaxis(jnp.exp(a_cum[..., -1]).reshape(b, g, r, c), 3, 0)

    def step(carry, inp):
        st, dec = inp
        return carry * dec[..., None, None] + st, carry

    init = jnp.zeros((b, g, r, p, n), F32)
    _, prev = lax.scan(step, init, (jnp.moveaxis(states, 1, 0), chunk_decay))
    prev = jnp.moveaxis(prev, 0, 1)
    y_off = jnp.einsum('bcqgn,bcgrpn,bgrcq->bcqgrp', Cc, prev, jnp.exp(a_cum).reshape(b, g, r, c, q))
    return (y_diag + y_off).reshape(b, l, h, p)


def gated_rmsnorm(y, z, w):
    gy = y.astype(F32) * jax.nn.silu(z.astype(F32))
    shp = gy.shape
    gy = gy.reshape(shp[:-1] + (SSD_GROUPS, shp[-1] // SSD_GROUPS))
    gy = gy * lax.rsqrt(jnp.mean(jnp.square(gy), -1, keepdims=True) + RMS_EPS)
    return (gy.reshape(shp) * w.astype(F32)).astype(y.dtype)


def hybrid_mixer(x, w_in, conv_w, conv_b, a_log, dt_bias, d_skip, ssd_norm_w, w_out):
    b, s, _ = x.shape
    proj = x @ w_in
    att = proj[..., :ATT_PROJ].reshape(b, s, N_DIL, 3, ATT_HEADS, HEAD_DIM)
    z = proj[..., ATT_PROJ:ATT_PROJ + SSD_WIDTH]
    xbc = proj[..., ATT_PROJ + SSD_WIDTH:ATT_PROJ + SSD_WIDTH + CONV_DIM]
    dt_raw = proj[..., ATT_PROJ + SSD_WIDTH + CONV_DIM:]
    pos = jnp.arange(s)
    outs, ms, ls = [], [], []
    for gi, (win, dil) in enumerate(DILATED_PATTERNS):
        qg = rope(att[:, :, gi, 0], pos)
        kg = rope(att[:, :, gi, 1], pos)
        o, m, l = dilated_window_attn(qg, kg, att[:, :, gi, 2], dil, win // (2 * dil))
        outs.append(o); ms.append(m); ls.append(l)
    ms = jnp.stack(ms); ls = jnp.stack(ls); outs = jnp.stack(outs)
    wts = ls * jnp.exp(ms - ms.max(0))
    att_out = ((wts[..., None] * outs).sum(0) / wts.sum(0)[..., None]).reshape(b, s, ATT_WIDTH).astype(x.dtype)
    xbc = jax.nn.silu(centred_depthwise_conv(xbc, conv_w, conv_b))
    xs = xbc[..., :SSD_WIDTH].reshape(b, s, SSD_HEADS, HEAD_DIM)
    Bm = xbc[..., SSD_WIDTH:SSD_WIDTH + SSD_GROUPS * D_STATE].reshape(b, s, SSD_GROUPS, D_STATE)
    Cm = xbc[..., SSD_WIDTH + SSD_GROUPS * D_STATE:].reshape(b, s, SSD_GROUPS, D_STATE)
    dt = jax.nn.softplus(dt_raw.astype(F32).reshape(b, s, 2, SSD_HEADS) + dt_bias.astype(F32))
    A = -jnp.exp(a_log.astype(F32))
    y_f = ssd_chunked(xs, dt[:, :, 0], A[0], Bm, Cm)
    fl = lambda t: jnp.flip(t, axis=1)
    y_b = fl(ssd_chunked(fl(xs), fl(dt[:, :, 1]), A[1], fl(Bm), fl(Cm)))
    y = y_f + y_b + xs.astype(F32) * d_skip.astype(F32)[:, None]
    ssd_out = gated_rmsnorm(y.reshape(b, s, SSD_WIDTH).astype(x.dtype), z, ssd_norm_w)
    return jnp.concatenate([ssd_out, att_out], axis=-1) @ w_out


def hier_moe(x, router_group, router_expert, w_gate, w_up, w_down):
    b, s, d = x.shape
    T = b * s
    A = T * TOP_K
    xt = x.reshape(T, d)
    gp = jax.nn.softmax((xt @ router_group).astype(F32), axis=-1)
    g_prob, g_idx = lax.top_k(gp, 1)
    el = (xt @ router_expert).astype(F32).reshape(T, N_EXPERT_GROUPS, EXPERTS_PER_GROUP)
    el = jnp.take_along_axis(el, g_idx[:, :, None], axis=1)[:, 0]
    e_prob, e_idx = lax.top_k(jax.nn.softmax(el, axis=-1), TOP_K)
    gate = g_prob * e_prob / e_prob.sum(-1, keepdims=True)
    expert = (g_idx * EXPERTS_PER_GROUP + e_idx).reshape(A).astype(jnp.int32)
    order = jnp.argsort(expert, stable=True)
    sorted_e = expert[order]
    counts = jnp.bincount(expert, length=N_EXPERTS)
    padded = (counts + MOE_BLOCK - 1) // MOE_BLOCK * MOE_BLOCK
    pad_end = jnp.cumsum(padded)
    pad_start = pad_end - padded
    start = jnp.cumsum(counts) - counts
    dest_sorted = (pad_start[sorted_e] + jnp.arange(A, dtype=jnp.int32) - start[sorted_e]).astype(jnp.int32)
    n_blocks = -(-A // MOE_BLOCK) + N_EXPERTS
    P = n_blocks * MOE_BLOCK
    slot_token = jnp.full((P,), T, jnp.int32).at[dest_sorted].set((order // TOP_K).astype(jnp.int32))
    block_expert = jnp.minimum(jnp.searchsorted(pad_end, jnp.arange(n_blocks) * MOE_BLOCK, side='right'),
                               N_EXPERTS - 1).astype(jnp.int32)
    xb = jnp.concatenate([xt, jnp.zeros((1, d), xt.dtype)], 0)[slot_token].reshape(n_blocks, MOE_BLOCK, d)

    def expert_block(args):
        xblk, e = args
        hdn = jax.nn.silu(xblk @ w_gate[e]) * (xblk @ w_up[e])
        return hdn @ w_down[e]

    yb = lax.map(expert_block, (xb, block_expert)).reshape(P, d)
    dest = jnp.zeros((A,), jnp.int32).at[order].set(dest_sorted)
    y = yb[dest].reshape(T, TOP_K, d)
    out = jnp.einsum('tkd,tk->td', y, gate.astype(y.dtype))
    return out.reshape(b, s, d)


def trunk(x, ln_in_g, ln_in_b, w_in, conv_w, conv_b, a_log, dt_bias, d_skip, ssd_norm_w, w_out,
          ln1_g, ln1_b, router_group, router_expert, w_gate, w_up, w_down, ln2_g, ln2_b):
    h = layer_norm(x, ln_in_g, ln_in_b)
    for i in range(DEPTH):
        mix = hybrid_mixer(h, w_in[i], conv_w[i], conv_b[i], a_log[i], dt_bias[i], d_skip[i],
                           ssd_norm_w[i], w_out[i])
        h = layer_norm(DN_ALPHA * h + mix, ln1_g[i], ln1_b[i])
        ffn = hier_moe(h, router_group[i], router_expert[i], w_gate[i], w_up[i], w_down[i])
        h = layer_norm(DN_ALPHA * h + ffn, ln2_g[i], ln2_b[i])
    return h


def setup_inputs(seed: int = 0) -> dict:
    key = jax.random.key(seed)
    ks = jax.random.split(key, 24)
    nrm = lambda k, shp, sc: jax.random.normal(k, shp, F32) * sc
    dt0 = jnp.exp(jax.random.uniform(ks[8], (DEPTH, 2, SSD_HEADS), F32) * (np.log(0.1) - np.log(0.001)) + np.log(0.001))
    return {
        'x_prompt': nrm(ks[0], (BATCH, SEQ, D_MODEL), 1.0),
        'x_sample': nrm(ks[1], (DEC_BATCH, DEC_SEQ, D_MODEL), 1.0),
        'ln_in_g': 1.0 + nrm(ks[2], (D_MODEL,), 0.02),
        'ln_in_b': nrm(ks[3], (D_MODEL,), 0.02),
        'w_in': nrm(ks[4], (DEPTH, D_MODEL, IN_DIM), D_MODEL ** -0.5),
        'conv_w': nrm(ks[5], (DEPTH, CONV_K, CONV_DIM), CONV_K ** -0.5),
        'conv_b': nrm(ks[6], (DEPTH, CONV_DIM), 0.02),
        'a_log': jnp.log(jax.random.uniform(ks[7], (DEPTH, 2, SSD_HEADS), F32, 1.0, 16.0)),
        'dt_bias': dt0 + jnp.log(-jnp.expm1(-dt0)),
        'd_skip': 1.0 + nrm(ks[9], (DEPTH, SSD_HEADS), 0.02),
        'ssd_norm_w': 1.0 + nrm(ks[10], (DEPTH, SSD_WIDTH), 0.02),
        'w_out': nrm(ks[11], (DEPTH, MIX_WIDTH, D_MODEL), DN_BETA * MIX_WIDTH ** -0.5),
        'ln1_g': 1.0 + nrm(ks[12], (DEPTH, D_MODEL), 0.02),
        'ln1_b': nrm(ks[13], (DEPTH, D_MODEL), 0.02),
        'router_group': nrm(ks[14], (DEPTH, D_MODEL, N_EXPERT_GROUPS), D_MODEL ** -0.5),
        'router_expert': nrm(ks[15], (DEPTH, D_MODEL, N_EXPERTS), D_MODEL ** -0.5),
        'w_gate': nrm(ks[16], (DEPTH, N_EXPERTS, D_MODEL, EXPERT_FF), D_MODEL ** -0.5),
        'w_up': nrm(ks[17], (DEPTH, N_EXPERTS, D_MODEL, EXPERT_FF), D_MODEL ** -0.5),
        'w_down': nrm(ks[18], (DEPTH, N_EXPERTS, EXPERT_FF, D_MODEL), DN_BETA * EXPERT_FF ** -0.5),
        'ln2_g': 1.0 + nrm(ks[19], (DEPTH, D_MODEL), 0.02),
        'ln2_b': nrm(ks[20], (DEPTH, D_MODEL), 0.02),
    }


def reference(x_prompt, x_sample, ln_in_g, ln_in_b, w_in, conv_w, conv_b, a_log, dt_bias, d_skip,
              ssd_norm_w, w_out, ln1_g, ln1_b, router_group, router_expert, w_gate, w_up, w_down,
              ln2_g, ln2_b):
    y_prompt = trunk(x_prompt, ln_in_g, ln_in_b, w_in, conv_w, conv_b, a_log, dt_bias, d_skip,
                     ssd_norm_w, w_out, ln1_g, ln1_b, router_group, router_expert, w_gate, w_up,
                     w_down, ln2_g, ln2_b)
    y_sample = trunk(x_sample, ln_in_g, ln_in_b, w_in, conv_w, conv_b, a_log, dt_bias, d_skip,
                     ssd_norm_w, w_out, ln1_g, ln1_b, router_group, router_expert, w_gate, w_up,
                     w_down, ln2_g, ln2_b)
    return (y_prompt, y_sample)
```

```python
import functools

import numpy as np
import jax
import jax.numpy as jnp
from jax import lax
from jax.experimental import pallas as pl
from jax.experimental.pallas import tpu as pltpu

F32 = jnp.float32
BF16 = jnp.bfloat16
I32 = jnp.int32

D_MODEL = 1024
DEPTH = 4
HEAD_DIM = 64
DILATED_PATTERNS = ((128, 1), (512, 4), (2048, 16))
N_DIL = 3
ATT_HEADS = 8
ATT_WIDTH = ATT_HEADS * HEAD_DIM
ATT_PROJ = N_DIL * 3 * ATT_WIDTH
ROPE_THETA = 10000.0
SSD_HEADS = 8
SSD_WIDTH = SSD_HEADS * HEAD_DIM
SSD_GROUPS = 2
HEADS_PER_GROUP = SSD_HEADS // SSD_GROUPS
D_STATE = 128
CONV_K = 5
CONV_DIM = SSD_WIDTH + 2 * SSD_GROUPS * D_STATE
N_EXPERT_GROUPS = 4
EXPERTS_PER_GROUP = 8
N_EXPERTS = N_EXPERT_GROUPS * EXPERTS_PER_GROUP
TOP_K = 2
EXPERT_FF = 512
MOE_BLOCK = 256
DN_ALPHA = (2 * DEPTH) ** 0.25
LN_EPS = 1e-5
RMS_EPS = 1e-5
NEG = -1e30

LANES = 128
SSD_CHUNK = 128
ATT_SUB = 128
ATT_HALO = 64
CONV_HALO = 16
VMEM_LIMIT = 48 * 1024 * 1024


def _cparams(*sem):
    return pltpu.CompilerParams(dimension_semantics=sem, vmem_limit_bytes=VMEM_LIMIT)


def _const_spec(shape):
    nd = len(shape)
    return pl.BlockSpec(shape, lambda *_: (0,) * nd)


def _ln_rows(x, g, b):
    mu = jnp.mean(x, -1, keepdims=True)
    xc = x - mu
    var = jnp.mean(xc * xc, -1, keepdims=True)
    return xc * lax.rsqrt(var + LN_EPS) * g + b


def _silu(x):
    return x / (1.0 + jnp.exp(-x))


def _ln_kernel(x_ref, g_ref, b_ref, o_ref):
    o_ref[...] = _ln_rows(x_ref[...], g_ref[...], b_ref[...])


def _layer_norm(x, g, b, tm=512):
    T, D = x.shape
    return pl.pallas_call(
        _ln_kernel,
        grid=(T // tm,),
        in_specs=[pl.BlockSpec((tm, D), lambda i: (i, 0)), _const_spec((1, D)), _const_spec((1, D))],
        out_specs=pl.BlockSpec((tm, D), lambda i: (i, 0)),
        out_shape=jax.ShapeDtypeStruct((T, D), F32),
        compiler_params=_cparams("parallel"),
        name="ln_in",
    )(x, g.reshape(1, D), b.reshape(1, D))


def _proj_kernel(x_ref, wa_ref, wz_ref, wd_ref, oa_ref, oz_ref, od_ref, *, tn):
    xb = x_ref[...].astype(BF16)
    for w_ref, o_ref in ((wa_ref, oa_ref), (wz_ref, oz_ref), (wd_ref, od_ref)):
        n = o_ref.shape[1]
        step = min(tn, n)
        for j in range(n // step):
            sl = slice(j * step, (j + 1) * step)
            o_ref[:, sl] = jnp.dot(xb, w_ref[:, sl], preferred_element_type=F32).astype(o_ref.dtype)


def _in_proj(h, w_att, w_zx, w_dt, tm=512, tn=512):
    T, D = h.shape
    outs = ((w_att.shape[1], BF16), (w_zx.shape[1], BF16), (w_dt.shape[1], F32))
    return pl.pallas_call(
        functools.partial(_proj_kernel, tn=tn),
        grid=(T // tm,),
        in_specs=[pl.BlockSpec((tm, D), lambda i: (i, 0))]
        + [_const_spec(w.shape) for w in (w_att, w_zx, w_dt)],
        out_specs=[pl.BlockSpec((tm, n), lambda i: (i, 0)) for n, _ in outs],
        out_shape=[jax.ShapeDtypeStruct((T, n), dt) for n, dt in outs],
        compiler_params=_cparams("parallel"),
        name="in_proj",
    )(h, w_att, w_zx, w_dt)


def _rope_store(dst_ref, row0, t, tab):
    cos = tab[:, :LANES]
    sin = tab[:, LANES:]
    lane = lax.broadcasted_iota(I32, (1, LANES), 1)
    first = (lane % HEAD_DIM) < (HEAD_DIM // 2)
    rows = t.shape[0]
    for c in range(ATT_WIDTH // LANES):
        tc = t[:, c * LANES:(c + 1) * LANES]
        rot = jnp.where(first, pltpu.roll(tc, LANES - HEAD_DIM // 2, 1), pltpu.roll(tc, HEAD_DIM // 2, 1))
        dst_ref[row0:row0 + rows, c * LANES:(c + 1) * LANES] = (tc * cos + rot * sin).astype(BF16)


def _attn_kernel(q_ref, kc_ref, kp_ref, kn_ref, vc_ref, vp_ref, vn_ref, tc_ref, tp_ref, tn_ref,
                 o_ref, lse_ref, qbuf, kbuf, vbuf, *, bq, seq_len, half):
    qi = pl.program_id(2)
    scale = HEAD_DIM ** -0.5
    _rope_store(qbuf, 0, q_ref[0].astype(F32) * scale, tc_ref[...])
    _rope_store(kbuf, 0, kp_ref[0].astype(F32), tp_ref[...])
    _rope_store(kbuf, ATT_HALO, kc_ref[0].astype(F32), tc_ref[...])
    _rope_store(kbuf, ATT_HALO + bq, kn_ref[0].astype(F32), tn_ref[...])
    vbuf[0:ATT_HALO] = vp_ref[0]
    vbuf[ATT_HALO:ATT_HALO + bq] = vc_ref[0]
    vbuf[ATT_HALO + bq:] = vn_ref[0]

    nk = ATT_SUB + 2 * ATT_HALO
    ri = lax.broadcasted_iota(I32, (ATT_SUB, nk), 0)
    ci = lax.broadcasted_iota(I32, (ATT_SUB, nk), 1)
    band = jnp.abs(ci - ATT_HALO - ri) <= half
    lane = lax.broadcasted_iota(I32, (1, LANES), 1)
    for sb in range(bq // ATT_SUB):
        r0 = sb * ATT_SUB
        kpos = qi * bq + r0 - ATT_HALO + ci
        mask = band & (kpos >= 0) & (kpos < seq_len)
        lse_tile = jnp.zeros((ATT_SUB, LANES), F32)
        for h in range(ATT_HEADS):
            hs = slice(h * HEAD_DIM, (h + 1) * HEAD_DIM)
            s = lax.dot_general(qbuf[r0:r0 + ATT_SUB, hs], kbuf[r0:r0 + nk, hs],
                                (((1,), (1,)), ((), ())), preferred_element_type=F32)
            s = jnp.where(mask, s, NEG)
            m = jnp.max(s, -1, keepdims=True)
            p = jnp.exp(s - m)
            l = jnp.sum(p, -1, keepdims=True)
            o = jnp.dot(p.astype(BF16), vbuf[r0:r0 + nk, hs], preferred_element_type=F32) / l
            o_ref[0, r0:r0 + ATT_SUB, hs] = o.astype(o_ref.dtype)
            lse_tile = jnp.where(lane == h, m + jnp.log(l), lse_tile)
        lse_ref[0, r0:r0 + ATT_SUB, :] = lse_tile


def _dilated_attention(att, rope_tab, b, s, gi):
    win, dil = DILATED_PATTERNS[gi]
    half = win // (2 * dil)
    assert half <= ATT_HALO
    L = s // dil
    bq = min(256, L)
    nq = L // bq
    hb = bq // ATT_HALO
    nhb = L // ATT_HALO
    ncol = ATT_PROJ // ATT_WIDTH
    att_v = att.reshape(b, L, dil * ATT_PROJ)
    tab_v = rope_tab.reshape(L, dil * 2 * LANES)

    def cur(j):
        return lambda bi, r, qi: (bi, qi, r * ncol + gi * 3 + j)

    def prev(j):
        return lambda bi, r, qi: (bi, jnp.maximum(qi * hb - 1, 0), r * ncol + gi * 3 + j)

    def nxt(j):
        return lambda bi, r, qi: (bi, jnp.minimum((qi + 1) * hb, nhb - 1), r * ncol + gi * 3 + j)

    blk = lambda n, f: pl.BlockSpec((1, n, ATT_WIDTH), f)
    in_specs = [
        blk(bq, cur(0)),
        blk(bq, cur(1)), blk(ATT_HALO, prev(1)), blk(ATT_HALO, nxt(1)),
        blk(bq, cur(2)), blk(ATT_HALO, prev(2)), blk(ATT_HALO, nxt(2)),
        pl.BlockSpec((bq, 2 * LANES), lambda bi, r, qi: (qi, r)),
        pl.BlockSpec((ATT_HALO, 2 * LANES), lambda bi, r, qi: (jnp.maximum(qi * hb - 1, 0), r)),
        pl.BlockSpec((ATT_HALO, 2 * LANES), lambda bi, r, qi: (jnp.minimum((qi + 1) * hb, nhb - 1), r)),
    ]
    o, lse = pl.pallas_call(
        functools.partial(_attn_kernel, bq=bq, seq_len=L, half=half),
        grid=(b, dil, nq),
        in_specs=in_specs,
        out_specs=[pl.BlockSpec((1, bq, ATT_WIDTH), lambda bi, r, qi: (bi, qi, r)),
                   pl.BlockSpec((1, bq, LANES), lambda bi, r, qi: (bi, qi, r))],
        out_shape=[jax.ShapeDtypeStruct((b, L, dil * ATT_WIDTH), BF16),
                   jax.ShapeDtypeStruct((b, L, dil * LANES), F32)],
        scratch_shapes=[pltpu.VMEM((bq, ATT_WIDTH), BF16),
                        pltpu.VMEM((bq + 2 * ATT_HALO, ATT_WIDTH), BF16),
                        pltpu.VMEM((bq + 2 * ATT_HALO, ATT_WIDTH), BF16)],
        compiler_params=_cparams("parallel", "parallel", "parallel"),
        name=f"dil_attn_{gi}",
    )(att_v, att_v, att_v, att_v, att_v, att_v, att_v, tab_v, tab_v, tab_v)
    return o.reshape(b * s, ATT_WIDTH), lse.reshape(b * s, LANES)


def _rope_table(s):
    half = HEAD_DIM // 2
    inv = ROPE_THETA ** (-jnp.arange(half, dtype=F32) / half)
    ang = jnp.arange(s).astype(F32)[:, None] * inv[None, :]
    cos, sin = jnp.cos(ang), jnp.sin(ang)
    cos_h = jnp.concatenate([cos, cos], -1)
    sin_h = jnp.concatenate([-sin, sin], -1)
    rep = LANES // HEAD_DIM
    return jnp.concatenate([jnp.tile(cos_h, (1, rep)), jnp.tile(sin_h, (1, rep))], -1)


def _conv_kernel(c_ref, p_ref, n_ref, w_ref, b_ref, o_ref, buf, *, ts):
    i = pl.program_id(1)
    last = pl.num_programs(1) - 1
    pad = (CONV_K - 1) // 2
    buf[0:CONV_HALO] = jnp.where(i > 0, p_ref[0].astype(F32), 0.0)
    buf[CONV_HALO:CONV_HALO + ts] = c_ref[0].astype(F32)
    buf[CONV_HALO + ts:] = jnp.where(i < last, n_ref[0].astype(F32), 0.0)
    acc = jnp.zeros((ts, CONV_DIM), F32) + b_ref[...]
    for j in range(CONV_K):
        acc = acc + w_ref[j:j + 1, :] * buf[CONV_HALO - pad + j:CONV_HALO - pad + j + ts, :]
    o_ref[0] = _silu(acc).astype(o_ref.dtype)


def _conv_silu(zx, conv_w, conv_b, b, s, ts=512):
    ts = min(ts, s)
    zx_v = zx.reshape(b, s, CONV_DIM + SSD_WIDTH)
    r = ts // CONV_HALO
    nh = s // CONV_HALO
    w8 = jnp.zeros((8, CONV_DIM), F32).at[:CONV_K].set(conv_w)
    return pl.pallas_call(
        functools.partial(_conv_kernel, ts=ts),
        grid=(b, s // ts),
        in_specs=[pl.BlockSpec((1, ts, CONV_DIM), lambda bi, i: (bi, i, 0)),
                  pl.BlockSpec((1, CONV_HALO, CONV_DIM), lambda bi, i: (bi, jnp.maximum(i * r - 1, 0), 0)),
                  pl.BlockSpec((1, CONV_HALO, CONV_DIM), lambda bi, i: (bi, jnp.minimum((i + 1) * r, nh - 1), 0)),
                  _const_spec((8, CONV_DIM)), _const_spec((1, CONV_DIM))],
        out_specs=pl.BlockSpec((1, ts, CONV_DIM), lambda bi, i: (bi, i, 0)),
        out_shape=jax.ShapeDtypeStruct((b, s, CONV_DIM), BF16),
        scratch_shapes=[pltpu.VMEM((ts + 2 * CONV_HALO, CONV_DIM), F32)],
        compiler_params=_cparams("parallel", "parallel"),
        name="conv_silu",
    )(zx_v, zx_v, zx_v, w8, conv_b.reshape(1, CONV_DIM))


def _expand_heads(v, off):
    head = lax.broadcasted_iota(I32, (1, SSD_WIDTH), 1) // HEAD_DIM
    out = jnp.zeros((v.shape[0], SSD_WIDTH), F32)
    for h in range(SSD_HEADS):
        out = jnp.where(head == h, v[:, off + h:off + h + 1], out)
    return out


def _softplus(x):
    return jnp.maximum(x, 0.0) + jnp.log(1.0 + jnp.exp(-jnp.abs(x)))


def _ssd_chunk(xc, dtr, bias, a_row, state_ref, *, reverse, off):
    Q = SSD_CHUNK
    hi = lax.Precision.HIGHEST
    dt = _softplus(dtr + bias)
    a = dt * a_row
    ri = lax.broadcasted_iota(I32, (Q, Q), 0)
    ci = lax.broadcasted_iota(I32, (Q, Q), 1)
    keep = (ci >= ri) if reverse else (ci <= ri)
    tri = keep.astype(F32)
    cum = jnp.dot(tri, a, precision=hi, preferred_element_type=F32)
    cum_t = lax.dot_general(a.T, tri, (((1,), (1,)), ((), ())), precision=hi,
                            preferred_element_type=F32)
    edge = 0 if reverse else Q - 1
    tot = cum[edge:edge + 1, :]
    dt512 = _expand_heads(dt, off)
    dec512 = _expand_heads(jnp.exp(tot - cum), off)
    ecum = jnp.exp(cum)
    etot512 = _expand_heads(jnp.exp(tot), off)
    xs = xc[:, :SSD_WIDTH].astype(F32)
    xdt = xs * dt512
    xdt_b = xdt.astype(BF16)
    xdd_b = (xdt * dec512).astype(BF16)
    gw = HEADS_PER_GROUP * HEAD_DIM
    ys = []
    for g in range(SSD_GROUPS):
        bg = xc[:, SSD_WIDTH + g * D_STATE:SSD_WIDTH + (g + 1) * D_STATE]
        cg = xc[:, SSD_WIDTH + (SSD_GROUPS + g) * D_STATE:SSD_WIDTH + (SSD_GROUPS + g + 1) * D_STATE]
        cb = lax.dot_general(cg, bg, (((1,), (1,)), ((), ())), preferred_element_type=F32)
        sg = state_ref[g]
        yoff = jnp.dot(cg, sg.astype(BF16), preferred_element_type=F32)
        for hh in range(HEADS_PER_GROUP):
            h = g * HEADS_PER_GROUP + hh
            ln = off + h
            seg = cum[:, ln:ln + 1] - cum_t[ln:ln + 1, :]
            lmat = jnp.where(keep, jnp.exp(jnp.where(keep, seg, 0.0)), 0.0)
            yd = jnp.dot((cb * lmat).astype(BF16), xdt_b[:, h * HEAD_DIM:(h + 1) * HEAD_DIM],
                         preferred_element_type=F32)
            ys.append(yd + yoff[:, hh * HEAD_DIM:(hh + 1) * HEAD_DIM] * ecum[:, ln:ln + 1])
        bg_t = bg.astype(F32).T.astype(BF16)
        state_ref[g] = sg * etot512[:, g * gw:(g + 1) * gw] + jnp.dot(
            bg_t, xdd_b[:, g * gw:(g + 1) * gw], preferred_element_type=F32)
    return ys


def _ssd_fwd_kernel(x_ref, dt_ref, bias_ref, a_ref, y_ref, state_ref, *, nch):
    @pl.when(pl.program_id(1) == 0)
    def _():
        state_ref[...] = jnp.zeros_like(state_ref)

    for c in range(nch):
        rows = slice(c * SSD_CHUNK, (c + 1) * SSD_CHUNK)
        ys = _ssd_chunk(x_ref[0, rows, :], dt_ref[0, rows, :], bias_ref[...], a_ref[...], state_ref,
                        reverse=False, off=0)
        for h, y in enumerate(ys):
            y_ref[0, rows, h * HEAD_DIM:(h + 1) * HEAD_DIM] = y


def _ssd_bwd_kernel(x_ref, dt_ref, bias_ref, a_ref, yf_ref, z_ref, dskip_ref, nw_ref, o_ref, state_ref,
                    ybuf, *, nch):
    @pl.when(pl.program_id(1) == 0)
    def _():
        state_ref[...] = jnp.zeros_like(state_ref)

    for c in reversed(range(nch)):
        rows = slice(c * SSD_CHUNK, (c + 1) * SSD_CHUNK)
        ys = _ssd_chunk(x_ref[0, rows, :], dt_ref[0, rows, :], bias_ref[...], a_ref[...], state_ref,
                        reverse=True, off=SSD_HEADS)
        for h, y in enumerate(ys):
            ybuf[rows, h * HEAD_DIM:(h + 1) * HEAD_DIM] = y
    xs = x_ref[0, :, :SSD_WIDTH].astype(F32)
    y = yf_ref[0] + ybuf[...] + xs * dskip_ref[...]
    gy = y * _silu(z_ref[0].astype(F32))
    gw = SSD_WIDTH // SSD_GROUPS
    for g in range(SSD_GROUPS):
        part = gy[:, g * gw:(g + 1) * gw]
        ms = jnp.mean(part * part, -1, keepdims=True)
        o_ref[0, :, g * gw:(g + 1) * gw] = (part * lax.rsqrt(ms + RMS_EPS) * nw_ref[:, g * gw:(g + 1) * gw]
                                             ).astype(o_ref.dtype)


def _ssd(xc, dt_raw, zx, dt_bias, a_log, d_skip, norm_w, b, s, nch=4):
    nch = min(nch, s // SSD_CHUNK)
    R = nch * SSD_CHUNK
    n = s // R
    dt_v = dt_raw.reshape(b, s, LANES)
    zx_v = zx.reshape(b, s, CONV_DIM + SSD_WIDTH)
    a_neg = -jnp.exp(a_log.astype(F32))
    pad = LANES - 2 * SSD_HEADS
    bias = jnp.pad(dt_bias.astype(F32).reshape(1, 2 * SSD_HEADS), ((0, 0), (0, pad)))
    a_f = jnp.pad(a_neg[0].reshape(1, SSD_HEADS), ((0, 0), (0, LANES - SSD_HEADS)))
    a_b = jnp.pad(a_neg[1].reshape(1, SSD_HEADS), ((0, 0), (SSD_HEADS, pad)))
    dskip = jnp.repeat(d_skip.astype(F32), HEAD_DIM).reshape(1, SSD_WIDTH)
    state = pltpu.VMEM((SSD_GROUPS, D_STATE, HEADS_PER_GROUP * HEAD_DIM), F32)
    fwd = lambda bi, i: (bi, i, 0)
    rev = lambda bi, i: (bi, n - 1 - i, 0)
    y_f = pl.pallas_call(
        functools.partial(_ssd_fwd_kernel, nch=nch),
        grid=(b, n),
        in_specs=[pl.BlockSpec((1, R, CONV_DIM), fwd), pl.BlockSpec((1, R, LANES), fwd),
                  _const_spec((1, LANES)), _const_spec((1, LANES))],
        out_specs=pl.BlockSpec((1, R, SSD_WIDTH), fwd),
        out_shape=jax.ShapeDtypeStruct((b, s, SSD_WIDTH), F32),
        scratch_shapes=[state],
        compiler_params=_cparams("parallel", "arbitrary"),
        name="ssd_fwd",
    )(xc, dt_v, bias, a_f)
    out = pl.pallas_call(
        functools.partial(_ssd_bwd_kernel, nch=nch),
        grid=(b, n),
        in_specs=[pl.BlockSpec((1, R, CONV_DIM), rev), pl.BlockSpec((1, R, LANES), rev),
                  _const_spec((1, LANES)), _const_spec((1, LANES)),
                  pl.BlockSpec((1, R, SSD_WIDTH), rev),
                  pl.BlockSpec((1, R, SSD_WIDTH), lambda bi, i: (bi, n - 1 - i, CONV_DIM // SSD_WIDTH)),
                  _const_spec((1, SSD_WIDTH)), _const_spec((1, SSD_WIDTH))],
        out_specs=pl.BlockSpec((1, R, SSD_WIDTH), rev),
        out_shape=jax.ShapeDtypeStruct((b, s, SSD_WIDTH), BF16),
        scratch_shapes=[state, pltpu.VMEM((R, SSD_WIDTH), F32)],
        compiler_params=_cparams("parallel", "arbitrary"),
        name="ssd_bwd",
    )(xc, dt_v, bias, a_b, y_f, zx_v, dskip, norm_w.astype(F32).reshape(1, SSD_WIDTH))
    return out.reshape(b * s, SSD_WIDTH)


def _expand_att_heads(v):
    head = lax.broadcasted_iota(I32, (1, ATT_WIDTH), 1) // HEAD_DIM
    out = jnp.zeros((v.shape[0], ATT_WIDTH), F32)
    for h in range(ATT_HEADS):
        out = jnp.where(head == h, v[:, h:h + 1], out)
    return out


def _out_proj_kernel(ssd_ref, o0_ref, o1_ref, o2_ref, l0_ref, l1_ref, l2_ref, h_ref, w1_ref, w2_ref,
                     g_ref, b_ref, out_ref):
    lses = [l0_ref[...], l1_ref[...], l2_ref[...]]
    mx = jnp.maximum(jnp.maximum(lses[0], lses[1]), lses[2])
    es = [jnp.exp(l - mx) for l in lses]
    den = es[0] + es[1] + es[2]
    att = jnp.zeros(o0_ref.shape, F32)
    for e, o_ref in zip(es, (o0_ref, o1_ref, o2_ref)):
        att = att + _expand_att_heads(e / den) * o_ref[...].astype(F32)
    y = jnp.dot(ssd_ref[...], w1_ref[...], preferred_element_type=F32)
    y = y + jnp.dot(att.astype(BF16), w2_ref[...], preferred_element_type=F32)
    out_ref[...] = _ln_rows(DN_ALPHA * h_ref[...] + y, g_ref[...], b_ref[...])


def _out_proj_ln(ssd, os_, lses, h, w_out, g, b, tm=512):
    T, D = h.shape
    row = lambda n: pl.BlockSpec((tm, n), lambda i: (i, 0))
    w1 = w_out[:SSD_WIDTH]
    w2 = w_out[SSD_WIDTH:]
    return pl.pallas_call(
        _out_proj_kernel,
        grid=(T // tm,),
        in_specs=[row(SSD_WIDTH)] + [row(ATT_WIDTH)] * 3 + [row(LANES)] * 3 + [row(D),
                  _const_spec(w1.shape), _const_spec(w2.shape), _const_spec((1, D)), _const_spec((1, D))],
        out_specs=row(D),
        out_shape=jax.ShapeDtypeStruct((T, D), F32),
        compiler_params=_cparams("parallel"),
        name="out_proj_ln",
    )(ssd, *os_, *lses, h, w1, w2, g.reshape(1, D), b.reshape(1, D))


def _router_kernel(h_ref, w_ref, id_ref, gate_ref):
    logits = lax.dot_general(w_ref[...], h_ref[...], (((1,), (1,)), ((), ())),
                             precision=lax.Precision.HIGHEST, preferred_element_type=F32)
    tm = logits.shape[1]
    row = lax.broadcasted_iota(I32, (8, tm), 0)
    lg = jnp.where(row < N_EXPERT_GROUPS, logits[0:8], NEG)
    gm = jnp.max(lg, 0, keepdims=True)
    gs = jnp.sum(jnp.exp(lg - gm), 0, keepdims=True)
    g_idx = jnp.min(jnp.where(lg == gm, row, 8), 0, keepdims=True)
    g_prob = 1.0 / gs
    el = jnp.zeros((8, tm), F32)
    for g in range(N_EXPERT_GROUPS):
        el = jnp.where(g_idx == g, logits[8 + 8 * g:16 + 8 * g], el)
    em = jnp.max(el, 0, keepdims=True)
    ee = jnp.exp(el - em)
    p = ee / jnp.sum(ee, 0, keepdims=True)
    p1 = jnp.max(p, 0, keepdims=True)
    i1 = jnp.min(jnp.where(p == p1, row, 8), 0, keepdims=True)
    pr = jnp.where(row == i1, -1.0, p)
    p2 = jnp.max(pr, 0, keepdims=True)
    i2 = jnp.min(jnp.where(pr == p2, row, 8), 0, keepdims=True)
    den = p1 + p2
    base = g_idx * EXPERTS_PER_GROUP
    id_ref[...] = jnp.where(row == 0, base + i1, jnp.where(row == 1, base + i2, 0))
    gate_ref[...] = jnp.where(row == 0, g_prob * p1 / den, jnp.where(row == 1, g_prob * p2 / den, 0.0))


def _router(h, w_router, tm=512):
    T, D = h.shape
    nr = w_router.shape[0]
    return pl.pallas_call(
        _router_kernel,
        grid=(T // tm,),
        in_specs=[pl.BlockSpec((tm, D), lambda i: (i, 0)), _const_spec((nr, D))],
        out_specs=[pl.BlockSpec((8, tm), lambda i: (0, i)), pl.BlockSpec((8, tm), lambda i: (0, i))],
        out_shape=[jax.ShapeDtypeStruct((8, T), I32), jax.ShapeDtypeStruct((8, T), F32)],
        compiler_params=_cparams("parallel"),
        name="router",
    )(h, w_router)


def _expert_kernel(be_ref, nu_ref, x_ref, wg_ref, wu_ref, wd_ref, o_ref):
    @pl.when(pl.program_id(0) < nu_ref[0])
    def _():
        x = x_ref[...]
        hg = jnp.dot(x, wg_ref[0], preferred_element_type=F32)
        hu = jnp.dot(x, wu_ref[0], preferred_element_type=F32)
        hdn = (_silu(hg) * hu).astype(BF16)
        o_ref[...] = jnp.dot(hdn, wd_ref[0], preferred_element_type=F32)


def _expert_ffn(xb, block_expert, n_used, w_gate, w_up, w_down):
    P, D = xb.shape
    nb = P // MOE_BLOCK
    grid_spec = pltpu.PrefetchScalarGridSpec(
        num_scalar_prefetch=2,
        grid=(nb,),
        in_specs=[pl.BlockSpec((MOE_BLOCK, D), lambda i, be, nu: (i, 0)),
                  pl.BlockSpec((1, D, EXPERT_FF), lambda i, be, nu: (be[i], 0, 0)),
                  pl.BlockSpec((1, D, EXPERT_FF), lambda i, be, nu: (be[i], 0, 0)),
                  pl.BlockSpec((1, EXPERT_FF, D), lambda i, be, nu: (be[i], 0, 0))],
        out_specs=pl.BlockSpec((MOE_BLOCK, D), lambda i, be, nu: (i, 0)),
    )
    return pl.pallas_call(
        _expert_kernel,
        grid_spec=grid_spec,
        out_shape=jax.ShapeDtypeStruct((P, D), F32),
        compiler_params=_cparams("arbitrary"),
        name="expert_ffn",
    )(block_expert, n_used, xb, w_gate, w_up, w_down)


def _combine_kernel(h_ref, y_ref, gate_ref, g_ref, b_ref, o_ref):
    d = h_ref.shape[1]
    gt = gate_ref[...]
    ffn = y_ref[:, :d] * gt[:, 0:1] + y_ref[:, d:] * gt[:, 1:2]
    o_ref[...] = _ln_rows(DN_ALPHA * h_ref[...] + ffn, g_ref[...], b_ref[...])


def _combine_ln(h, y2, gates, g, b, tm=512):
    T, D = h.shape
    return pl.pallas_call(
        _combine_kernel,
        grid=(T // tm,),
        in_specs=[pl.BlockSpec((tm, D), lambda i: (i, 0)), pl.BlockSpec((tm, TOP_K * D), lambda i: (i, 0)),
                  pl.BlockSpec((tm, TOP_K), lambda i: (i, 0)), _const_spec((1, D)), _const_spec((1, D))],
        out_specs=pl.BlockSpec((tm, D), lambda i: (i, 0)),
        out_shape=jax.ShapeDtypeStruct((T, D), F32),
        compiler_params=_cparams("parallel"),
        name="combine_ln",
    )(h, y2, gates, g.reshape(1, D), b.reshape(1, D))


def _mixer(h, b, s, lw, rope_tab):
    att, zx, dt_raw = _in_proj(h, lw["w_att"], lw["w_zx"], lw["w_dt"])
    os_, lses = [], []
    for gi in range(N_DIL):
        o, lse = _dilated_attention(att, rope_tab, b, s, gi)
        os_.append(o)
        lses.append(lse)
    xc = _conv_silu(zx, lw["conv_w"], lw["conv_b"], b, s)
    ssd = _ssd(xc, dt_raw, zx, lw["dt_bias"], lw["a_log"], lw["d_skip"], lw["ssd_norm_w"], b, s)
    return _out_proj_ln(ssd, os_, lses, h, lw["w_out"], lw["ln1_g"], lw["ln1_b"])


def _moe(h, lw):
    T, D = h.shape
    A = T * TOP_K
    ids, gates = _router(h, lw["w_router"])
    expert = ids[:TOP_K].T.reshape(A)
    order = jnp.argsort(expert, stable=True)
    sorted_e = expert[order]
    counts = jnp.bincount(expert, length=N_EXPERTS)
    padded = (counts + MOE_BLOCK - 1) // MOE_BLOCK * MOE_BLOCK
    pad_end = jnp.cumsum(padded)
    pad_start = pad_end - padded
    start = jnp.cumsum(counts) - counts
    dest_sorted = (pad_start[sorted_e] + jnp.arange(A, dtype=I32) - start[sorted_e]).astype(I32)
    n_blocks = -(-A // MOE_BLOCK) + N_EXPERTS
    P = n_blocks * MOE_BLOCK
    slot_token = jnp.full((P,), T, I32).at[dest_sorted].set((order // TOP_K).astype(I32))
    block_expert = jnp.minimum(jnp.searchsorted(pad_end, jnp.arange(n_blocks) * MOE_BLOCK, side="right"),
                               N_EXPERTS - 1).astype(I32)
    n_used = (pad_end[-1] // MOE_BLOCK).astype(I32).reshape(1)
    dest = jnp.zeros((A,), I32).at[order].set(dest_sorted)
    xb = jnp.concatenate([h.astype(BF16), jnp.zeros((1, D), BF16)], 0)[slot_token]
    yb = _expert_ffn(xb, block_expert, n_used, lw["w_gate"], lw["w_up"], lw["w_down"])
    y2 = yb[dest].reshape(T, TOP_K * D)
    return _combine_ln(h, y2, gates[:TOP_K].T, lw["ln2_g"], lw["ln2_b"])


def _trunk(x, ln_in_g, ln_in_b, layers):
    b, s, D = x.shape
    rope_tab = _rope_table(s)
    h = _layer_norm(x.reshape(b * s, D), ln_in_g, ln_in_b)
    for lw in layers:
        h = _mixer(h, b, s, lw, rope_tab)
        h = _moe(h, lw)
    return h.reshape(b, s, D)


def _prep_layers(w_in, conv_w, conv_b, a_log, dt_bias, d_skip, ssd_norm_w, w_out, ln1_g, ln1_b,
                 router_group, router_expert, w_gate, w_up, w_down, ln2_g, ln2_b):
    layers = []
    z0 = ATT_PROJ
    x0 = ATT_PROJ + SSD_WIDTH
    d0 = x0 + CONV_DIM
    for i in range(w_in.shape[0]):
        w = w_in[i]
        w_dt = jnp.pad(w[:, d0:], ((0, 0), (0, LANES - 2 * SSD_HEADS))).astype(BF16)
        w_router = jnp.concatenate([
            router_group[i].T, jnp.zeros((8 - N_EXPERT_GROUPS, D_MODEL), F32), router_expert[i].T], 0)
        layers.append(dict(
            w_att=w[:, :z0].astype(BF16),
            w_zx=jnp.concatenate([w[:, x0:d0], w[:, z0:x0]], 1).astype(BF16),
            w_dt=w_dt,
            conv_w=conv_w[i], conv_b=conv_b[i], a_log=a_log[i], dt_bias=dt_bias[i], d_skip=d_skip[i],
            ssd_norm_w=ssd_norm_w[i], w_out=w_out[i].astype(BF16), ln1_g=ln1_g[i], ln1_b=ln1_b[i],
            w_router=w_router, w_gate=w_gate[i].astype(BF16), w_up=w_up[i].astype(BF16),
            w_down=w_down[i].astype(BF16), ln2_g=ln2_g[i], ln2_b=ln2_b[i]))
    return layers


def kernel(x_prompt, x_sample, ln_in_g, ln_in_b, w_in, conv_w, conv_b, a_log, dt_bias, d_skip, ssd_norm_w,
           w_out, ln1_g, ln1_b, router_group, router_expert, w_gate, w_up, w_down, ln2_g, ln2_b):
    layers = _prep_layers(w_in, conv_w, conv_b, a_log, dt_bias, d_skip, ssd_norm_w, w_out, ln1_g, ln1_b,
                          router_group, router_expert, w_gate, w_up, w_down, ln2_g, ln2_b)
    y_prompt = _trunk(x_prompt, ln_in_g, ln_in_b, layers)
    y_sample = _trunk(x_sample, ln_in_g, ln_in_b, layers)
    return (y_prompt, y_sample)
```

```python
import functools

import numpy as np
import jax
import jax.numpy as jnp
from jax import lax
from jax.experimental import pallas as pl
from jax.experimental.pallas import tpu as pltpu

F32 = jnp.float32
BF16 = jnp.bfloat16
I32 = jnp.int32

D_MODEL = 1024
DEPTH = 4
HEAD_DIM = 64
DILATED_PATTERNS = ((128, 1), (512, 4), (2048, 16))
N_DIL = 3
ATT_HEADS = 8
ATT_WIDTH = ATT_HEADS * HEAD_DIM
ATT_PROJ = N_DIL * 3 * ATT_WIDTH
ROPE_THETA = 10000.0
SSD_HEADS = 8
SSD_WIDTH = SSD_HEADS * HEAD_DIM
SSD_GROUPS = 2
HEADS_PER_GROUP = SSD_HEADS // SSD_GROUPS
D_STATE = 128
CONV_K = 5
CONV_DIM = SSD_WIDTH + 2 * SSD_GROUPS * D_STATE
N_EXPERT_GROUPS = 4
EXPERTS_PER_GROUP = 8
N_EXPERTS = N_EXPERT_GROUPS * EXPERTS_PER_GROUP
TOP_K = 2
EXPERT_FF = 512
MOE_BLOCK = 256
DN_ALPHA = (2 * DEPTH) ** 0.25
LN_EPS = 1e-5
RMS_EPS = 1e-5
NEG = -1e30

LANES = 128
SSD_CHUNK = 128
ATT_SUB = 128
ATT_HALO = 64
CONV_HALO = 16
VMEM_LIMIT = 48 * 1024 * 1024


def _cparams(*sem):
    return pltpu.CompilerParams(dimension_semantics=sem, vmem_limit_bytes=VMEM_LIMIT)


def _const_spec(shape):
    nd = len(shape)
    return pl.BlockSpec(shape, lambda *_: (0,) * nd)


def _ln_rows(x, g, b):
    mu = jnp.mean(x, -1, keepdims=True)
    xc = x - mu
    var = jnp.mean(xc * xc, -1, keepdims=True)
    return xc * lax.rsqrt(var + LN_EPS) * g + b


def _silu(x):
    return x / (1.0 + jnp.exp(-x))


def _ln_kernel(x_ref, g_ref, b_ref, o_ref):
    o_ref[...] = _ln_rows(x_ref[...], g_ref[...], b_ref[...])


def _layer_norm(x, g, b, tm=512):
    T, D = x.shape
    return pl.pallas_call(
        _ln_kernel,
        grid=(T // tm,),
        in_specs=[pl.BlockSpec((tm, D), lambda i: (i, 0)), _const_spec((1, D)), _const_spec((1, D))],
        out_specs=pl.BlockSpec((tm, D), lambda i: (i, 0)),
        out_shape=jax.ShapeDtypeStruct((T, D), F32),
        compiler_params=_cparams("parallel"),
        name="ln_in",
    )(x, g.reshape(1, D), b.reshape(1, D))


def _rope_chunks(y, tab):
    cos = tab[:, :LANES]
    sin = tab[:, LANES:]
    lane = lax.broadcasted_iota(I32, (1, LANES), 1)
    first = (lane % HEAD_DIM) < (HEAD_DIM // 2)
    out = []
    for c in range(ATT_WIDTH // LANES):
        tc = y[:, c * LANES:(c + 1) * LANES]
        rot = jnp.where(first, pltpu.roll(tc, LANES - HEAD_DIM // 2, 1), pltpu.roll(tc, HEAD_DIM // 2, 1))
        out.append(tc * cos + rot * sin)
    return out


def _proj_kernel(x_ref, wa_ref, wz_ref, wd_ref, t0_ref, t1_ref, t2_ref,
                 a0_ref, a1_ref, a2_ref, oz_ref, od_ref, xc_ref, xs_ref, *, tm, tn):
    xb = x_ref[0].astype(BF16)
    nlc = x_ref.shape[2] // LANES
    for c in range(nlc):
        xc_ref[c] = x_ref[0, :, c * LANES:(c + 1) * LANES]
    for j in range(oz_ref.shape[2] // tn):
        sl = slice(j * tn, (j + 1) * tn)
        oz_ref[0, :, sl] = jnp.dot(xb, wz_ref[:, sl], preferred_element_type=F32).astype(oz_ref.dtype)
    od_ref[0] = jnp.dot(xb, wd_ref[...], preferred_element_type=F32)
    scale = HEAD_DIM ** -0.5
    nchunk = ATT_WIDTH // LANES
    for g, (t_ref, a_ref) in enumerate(((t0_ref, a0_ref), (t1_ref, a1_ref), (t2_ref, a2_ref))):
        dil = DILATED_PATTERNS[g][1]
        n = tm // dil
        if dil == 1:
            xp = xb
        else:
            for r in range(dil):
                for c in range(nlc):
                    xs_ref[r * n:(r + 1) * n, c * LANES:(c + 1) * LANES] = (
                        xc_ref[c, pl.ds(r, n, stride=dil), :].astype(BF16))
            xp = xs_ref[...]
        tab = t_ref[...].reshape(tm, 2 * LANES)
        for j in range(3):
            c0 = (g * 3 + j) * ATT_WIDTH
            y = jnp.dot(xp, wa_ref[:, c0:c0 + ATT_WIDTH], preferred_element_type=F32)
            if j < 2:
                chunks = _rope_chunks(y, tab)
                if j == 0:
                    chunks = [ch * scale for ch in chunks]
            else:
                chunks = [y[:, c * LANES:(c + 1) * LANES] for c in range(nchunk)]
            for c, ch in enumerate(chunks):
                chb = ch.astype(BF16)
                col = j * ATT_WIDTH + c * LANES
                for r in range(dil):
                    a_ref[0, r, :, col:col + LANES] = chb[r * n:(r + 1) * n]


def _in_proj(h, w_att, w_zx, w_dt, tabs, tm=512, tn=512):
    b, s, D = h.shape
    dils = [d for _, d in DILATED_PATTERNS]
    qkv = 3 * ATT_WIDTH
    nz, nd = w_zx.shape[1], w_dt.shape[1]
    row = lambda n: pl.BlockSpec((1, tm, n), lambda bi, i: (bi, i, 0))
    sub = lambda d, n: pl.BlockSpec((1, d, tm // d, n), lambda bi, i: (bi, 0, i, 0))
    return pl.pallas_call(
        functools.partial(_proj_kernel, tm=tm, tn=tn),
        grid=(b, s // tm),
        in_specs=[row(D)] + [_const_spec(w.shape) for w in (w_att, w_zx, w_dt)]
        + [pl.BlockSpec((d, tm // d, 2 * LANES), lambda bi, i: (0, i, 0)) for d in dils],
        out_specs=[sub(d, qkv) for d in dils] + [row(nz), row(nd)],
        out_shape=[jax.ShapeDtypeStruct((b, d, s // d, qkv), BF16) for d in dils]
        + [jax.ShapeDtypeStruct((b, s, nz), BF16), jax.ShapeDtypeStruct((b, s, nd), F32)],
        scratch_shapes=[pltpu.VMEM((D // LANES, tm, LANES), F32), pltpu.VMEM((tm, D), BF16)],
        compiler_params=_cparams("parallel", "parallel"),
        name="in_proj",
    )(h, w_att, w_zx, w_dt, *tabs)


def _attn_kernel(q_ref, kc_ref, kp_ref, kn_ref, vc_ref, vp_ref, vn_ref, o_ref, lse_ref, kbuf, vbuf,
                 *, bq, seq_len, half):
    qi = pl.program_id(2)
    kbuf[0:ATT_HALO] = kp_ref[0, 0]
    kbuf[ATT_HALO:ATT_HALO + bq] = kc_ref[0, 0]
    kbuf[ATT_HALO + bq:] = kn_ref[0, 0]
    vbuf[0:ATT_HALO] = vp_ref[0, 0]
    vbuf[ATT_HALO:ATT_HALO + bq] = vc_ref[0, 0]
    vbuf[ATT_HALO + bq:] = vn_ref[0, 0]

    nk = ATT_SUB + 2 * ATT_HALO
    npair = ATT_HEADS // 2
    ri = lax.broadcasted_iota(I32, (ATT_SUB, nk), 0)
    ci = lax.broadcasted_iota(I32, (ATT_SUB, nk), 1)
    band = jnp.abs(ci - ATT_HALO - ri) <= half
    lane = lax.broadcasted_iota(I32, (1, LANES), 1)
    even = lane < HEAD_DIM
    for sb in range(bq // ATT_SUB):
        r0 = sb * ATT_SUB
        kpos = qi * bq + r0 - ATT_HALO + ci
        mask = band & (kpos >= 0) & (kpos < seq_len)
        ss = []
        for j in range(npair):
            cs = slice(j * LANES, (j + 1) * LANES)
            qp = q_ref[0, 0, r0:r0 + ATT_SUB, cs]
            zero = jnp.zeros_like(qp)
            lhs = jnp.concatenate([jnp.where(even, qp, zero), jnp.where(even, zero, qp)], axis=0)
            ss.append(lax.dot_general(lhs, kbuf[r0:r0 + nk, cs], (((1,), (1,)), ((), ())),
                                      preferred_element_type=F32))
        s = jnp.stack(ss).reshape(ATT_HEADS, ATT_SUB, nk)
        s = jnp.where(mask[None], s, NEG)
        m = jnp.max(s, -1, keepdims=True)
        p = jnp.exp(s - m)
        l = jnp.sum(p, -1, keepdims=True)
        pb = p.astype(BF16).reshape(npair, 2 * ATT_SUB, nk)
        l2 = l.reshape(npair, 2 * ATT_SUB, 1)
        lse = m + jnp.log(l)
        for j in range(npair):
            cs = slice(j * LANES, (j + 1) * LANES)
            o2 = jnp.dot(pb[j], vbuf[r0:r0 + nk, cs], preferred_element_type=F32) / l2[j]
            o_ref[0, 0, r0:r0 + ATT_SUB, cs] = jnp.where(even, o2[:ATT_SUB], o2[ATT_SUB:]).astype(o_ref.dtype)
        lse_tile = jnp.zeros((ATT_SUB, LANES), F32)
        for h in range(ATT_HEADS):
            lse_tile = jnp.where(lane == h, lse[h], lse_tile)
        lse_ref[0, 0, r0:r0 + ATT_SUB, :] = lse_tile


def _dilated_attention(att, gi, bq=256):
    win, dil = DILATED_PATTERNS[gi]
    half = win // (2 * dil)
    assert half <= ATT_HALO
    b, _, L, _ = att.shape
    bq = min(bq, L)
    nq = L // bq
    hb = bq // ATT_HALO
    nhb = L // ATT_HALO
    cur = lambda j: (lambda bi, r, qi: (bi, r, qi, j))
    prev = lambda j: (lambda bi, r, qi: (bi, r, jnp.maximum(qi * hb - 1, 0), j))
    nxt = lambda j: (lambda bi, r, qi: (bi, r, jnp.minimum((qi + 1) * hb, nhb - 1), j))
    blk = lambda n, f: pl.BlockSpec((1, 1, n, ATT_WIDTH), f)
    return pl.pallas_call(
        functools.partial(_attn_kernel, bq=bq, seq_len=L, half=half),
        grid=(b, dil, nq),
        in_specs=[blk(bq, cur(0)),
                  blk(bq, cur(1)), blk(ATT_HALO, prev(1)), blk(ATT_HALO, nxt(1)),
                  blk(bq, cur(2)), blk(ATT_HALO, prev(2)), blk(ATT_HALO, nxt(2))],
        out_specs=[pl.BlockSpec((1, 1, bq, ATT_WIDTH), lambda bi, r, qi: (bi, r, qi, 0)),
                   pl.BlockSpec((1, 1, bq, LANES), lambda bi, r, qi: (bi, r, qi, 0))],
        out_shape=[jax.ShapeDtypeStruct((b, dil, L, ATT_WIDTH), BF16),
                   jax.ShapeDtypeStruct((b, dil, L, LANES), F32)],
        scratch_shapes=[pltpu.VMEM((bq + 2 * ATT_HALO, ATT_WIDTH), BF16),
                        pltpu.VMEM((bq + 2 * ATT_HALO, ATT_WIDTH), BF16)],
        compiler_params=_cparams("parallel", "parallel", "parallel"),
        name=f"dil_attn_{gi}",
    )(att, att, att, att, att, att, att)


def _rope_tables(s):
    half = HEAD_DIM // 2
    inv = ROPE_THETA ** (-jnp.arange(half, dtype=F32) / half)
    ang = jnp.arange(s).astype(F32)[:, None] * inv[None, :]
    cos, sin = jnp.cos(ang), jnp.sin(ang)
    cos_h = jnp.concatenate([cos, cos], -1)
    sin_h = jnp.concatenate([-sin, sin], -1)
    rep = LANES // HEAD_DIM
    tab = jnp.concatenate([jnp.tile(cos_h, (1, rep)), jnp.tile(sin_h, (1, rep))], -1)
    return [tab.reshape(s // d, d, 2 * LANES).transpose(1, 0, 2) for _, d in DILATED_PATTERNS]


def _conv_kernel(c_ref, p_ref, n_ref, w_ref, b_ref, o_ref, buf, *, ts):
    i = pl.program_id(1)
    last = pl.num_programs(1) - 1
    pad = (CONV_K - 1) // 2
    buf[0:CONV_HALO] = jnp.where(i > 0, p_ref[0].astype(F32), 0.0)
    buf[CONV_HALO:CONV_HALO + ts] = c_ref[0].astype(F32)
    buf[CONV_HALO + ts:] = jnp.where(i < last, n_ref[0].astype(F32), 0.0)
    acc = jnp.zeros((ts, CONV_DIM), F32) + b_ref[...]
    for j in range(CONV_K):
        acc = acc + w_ref[j:j + 1, :] * buf[CONV_HALO - pad + j:CONV_HALO - pad + j + ts, :]
    o_ref[0] = _silu(acc).astype(o_ref.dtype)


def _conv_silu(zx_v, conv_w, conv_b, ts=512):
    b, s, _ = zx_v.shape
    ts = min(ts, s)
    r = ts // CONV_HALO
    nh = s // CONV_HALO
    w8 = jnp.zeros((8, CONV_DIM), F32).at[:CONV_K].set(conv_w)
    return pl.pallas_call(
        functools.partial(_conv_kernel, ts=ts),
        grid=(b, s // ts),
        in_specs=[pl.BlockSpec((1, ts, CONV_DIM), lambda bi, i: (bi, i, 0)),
                  pl.BlockSpec((1, CONV_HALO, CONV_DIM), lambda bi, i: (bi, jnp.maximum(i * r - 1, 0), 0)),
                  pl.BlockSpec((1, CONV_HALO, CONV_DIM), lambda bi, i: (bi, jnp.minimum((i + 1) * r, nh - 1), 0)),
                  _const_spec((8, CONV_DIM)), _const_spec((1, CONV_DIM))],
        out_specs=pl.BlockSpec((1, ts, CONV_DIM), lambda bi, i: (bi, i, 0)),
        out_shape=jax.ShapeDtypeStruct((b, s, CONV_DIM), BF16),
        scratch_shapes=[pltpu.VMEM((ts + 2 * CONV_HALO, CONV_DIM), F32)],
        compiler_params=_cparams("parallel", "parallel"),
        name="conv_silu",
    )(zx_v, zx_v, zx_v, w8, conv_b.reshape(1, CONV_DIM))


def _expand_heads(v, off):
    head = lax.broadcasted_iota(I32, (1, SSD_WIDTH), 1) // HEAD_DIM
    out = jnp.zeros((v.shape[0], SSD_WIDTH), F32)
    for h in range(SSD_HEADS):
        out = jnp.where(head == h, v[:, off + h:off + h + 1], out)
    return out


def _softplus(x):
    return jnp.maximum(x, 0.0) + jnp.log(1.0 + jnp.exp(-jnp.abs(x)))


def _ssd_chunk(xc, dtr, bias, a_row, state_ref, *, reverse, off):
    Q = SSD_CHUNK
    hi = lax.Precision.HIGHEST
    dt = _softplus(dtr + bias)
    a = dt * a_row
    ri = lax.broadcasted_iota(I32, (Q, Q), 0)
    ci = lax.broadcasted_iota(I32, (Q, Q), 1)
    keep = (ci >= ri) if reverse else (ci <= ri)
    tri = keep.astype(F32)
    cum = jnp.dot(tri, a, precision=hi, preferred_element_type=F32)
    cum_t = lax.dot_general(a.T, tri, (((1,), (1,)), ((), ())), precision=hi,
                            preferred_element_type=F32)
    edge = 0 if reverse else Q - 1
    tot = cum[edge:edge + 1, :]
    dt512 = _expand_heads(dt, off)
    dec512 = _expand_heads(jnp.exp(tot - cum), off)
    ecum = jnp.exp(cum)
    etot512 = _expand_heads(jnp.exp(tot), off)
    xs = xc[:, :SSD_WIDTH].astype(F32)
    xdt = xs * dt512
    xdt_b = xdt.astype(BF16)
    xdd_b = (xdt * dec512).astype(BF16)
    gw = HEADS_PER_GROUP * HEAD_DIM
    ys = []
    for g in range(SSD_GROUPS):
        bg = xc[:, SSD_WIDTH + g * D_STATE:SSD_WIDTH + (g + 1) * D_STATE]
        cg = xc[:, SSD_WIDTH + (SSD_GROUPS + g) * D_STATE:SSD_WIDTH + (SSD_GROUPS + g + 1) * D_STATE]
        cb = lax.dot_general(cg, bg, (((1,), (1,)), ((), ())), preferred_element_type=F32)
        sg = state_ref[g]
        yoff = jnp.dot(cg, sg.astype(BF16), preferred_element_type=F32)
        for hh in range(HEADS_PER_GROUP):
            h = g * HEADS_PER_GROUP + hh
            ln = off + h
            seg = cum[:, ln:ln + 1] - cum_t[ln:ln + 1, :]
            lmat = jnp.where(keep, jnp.exp(jnp.where(keep, seg, 0.0)), 0.0)
            yd = jnp.dot((cb * lmat).astype(BF16), xdt_b[:, h * HEAD_DIM:(h + 1) * HEAD_DIM],
                         preferred_element_type=F32)
            ys.append(yd + yoff[:, hh * HEAD_DIM:(hh + 1) * HEAD_DIM] * ecum[:, ln:ln + 1])
        bg_t = bg.astype(F32).T.astype(BF16)
        state_ref[g] = sg * etot512[:, g * gw:(g + 1) * gw] + jnp.dot(
            bg_t, xdd_b[:, g * gw:(g + 1) * gw], preferred_element_type=F32)
    return ys


def _ssd_fwd_kernel(x_ref, dt_ref, bias_ref, a_ref, y_ref, state_ref, *, nch):
    @pl.when(pl.program_id(1) == 0)
    def _():
        state_ref[...] = jnp.zeros_like(state_ref)

    for c in range(nch):
        rows = slice(c * SSD_CHUNK, (c + 1) * SSD_CHUNK)
        ys = _ssd_chunk(x_ref[0, rows, :], dt_ref[0, rows, :], bias_ref[...], a_ref[...], state_ref,
                        reverse=False, off=0)
        for h, y in enumerate(ys):
            y_ref[0, rows, h * HEAD_DIM:(h + 1) * HEAD_DIM] = y


def _ssd_bwd_kernel(x_ref, dt_ref, bias_ref, a_ref, yf_ref, z_ref, dskip_ref, nw_ref, o_ref, state_ref,
                    ybuf, *, nch):
    @pl.when(pl.program_id(1) == 0)
    def _():
        state_ref[...] = jnp.zeros_like(state_ref)

    for c in reversed(range(nch)):
        rows = slice(c * SSD_CHUNK, (c + 1) * SSD_CHUNK)
        ys = _ssd_chunk(x_ref[0, rows, :], dt_ref[0, rows, :], bias_ref[...], a_ref[...], state_ref,
                        reverse=True, off=SSD_HEADS)
        for h, y in enumerate(ys):
            ybuf[rows, h * HEAD_DIM:(h + 1) * HEAD_DIM] = y
    xs = x_ref[0, :, :SSD_WIDTH].astype(F32)
    y = yf_ref[0] + ybuf[...] + xs * dskip_ref[...]
    gy = y * _silu(z_ref[0].astype(F32))
    gw = SSD_WIDTH // SSD_GROUPS
    for g in range(SSD_GROUPS):
        part = gy[:, g * gw:(g + 1) * gw]
        ms = jnp.mean(part * part, -1, keepdims=True)
        o_ref[0, :, g * gw:(g + 1) * gw] = (part * lax.rsqrt(ms + RMS_EPS) * nw_ref[:, g * gw:(g + 1) * gw]
                                             ).astype(o_ref.dtype)


def _ssd(xc, dt_v, zx_v, dt_bias, a_log, d_skip, norm_w, nch=4):
    b, s, _ = xc.shape
    nch = min(nch, s // SSD_CHUNK)
    R = nch * SSD_CHUNK
    n = s // R
    a_neg = -jnp.exp(a_log.astype(F32))
    pad = LANES - 2 * SSD_HEADS
    bias = jnp.pad(dt_bias.astype(F32).reshape(1, 2 * SSD_HEADS), ((0, 0), (0, pad)))
    a_f = jnp.pad(a_neg[0].reshape(1, SSD_HEADS), ((0, 0), (0, LANES - SSD_HEADS)))
    a_b = jnp.pad(a_neg[1].reshape(1, SSD_HEADS), ((0, 0), (SSD_HEADS, pad)))
    dskip = jnp.repeat(d_skip.astype(F32), HEAD_DIM).reshape(1, SSD_WIDTH)
    state = pltpu.VMEM((SSD_GROUPS, D_STATE, HEADS_PER_GROUP * HEAD_DIM), F32)
    fwd = lambda bi, i: (bi, i, 0)
    rev = lambda bi, i: (bi, n - 1 - i, 0)
    y_f = pl.pallas_call(
        functools.partial(_ssd_fwd_kernel, nch=nch),
        grid=(b, n),
        in_specs=[pl.BlockSpec((1, R, CONV_DIM), fwd), pl.BlockSpec((1, R, LANES), fwd),
                  _const_spec((1, LANES)), _const_spec((1, LANES))],
        out_specs=pl.BlockSpec((1, R, SSD_WIDTH), fwd),
        out_shape=jax.ShapeDtypeStruct((b, s, SSD_WIDTH), F32),
        scratch_shapes=[state],
        compiler_params=_cparams("parallel", "arbitrary"),
        name="ssd_fwd",
    )(xc, dt_v, bias, a_f)
    out = pl.pallas_call(
        functools.partial(_ssd_bwd_kernel, nch=nch),
        grid=(b, n),
        in_specs=[pl.BlockSpec((1, R, CONV_DIM), rev), pl.BlockSpec((1, R, LANES), rev),
                  _const_spec((1, LANES)), _const_spec((1, LANES)),
                  pl.BlockSpec((1, R, SSD_WIDTH), rev),
                  pl.BlockSpec((1, R, SSD_WIDTH), lambda bi, i: (bi, n - 1 - i, CONV_DIM // SSD_WIDTH)),
                  _const_spec((1, SSD_WIDTH)), _const_spec((1, SSD_WIDTH))],
        out_specs=pl.BlockSpec((1, R, SSD_WIDTH), rev),
        out_shape=jax.ShapeDtypeStruct((b, s, SSD_WIDTH), BF16),
        scratch_shapes=[state, pltpu.VMEM((R, SSD_WIDTH), F32)],
        compiler_params=_cparams("parallel", "arbitrary"),
        name="ssd_bwd",
    )(xc, dt_v, bias, a_b, y_f, zx_v, dskip, norm_w.astype(F32).reshape(1, SSD_WIDTH))
    return out


def _expand_att_heads(v):
    head = lax.broadcasted_iota(I32, (1, ATT_WIDTH), 1) // HEAD_DIM
    out = jnp.zeros((v.shape[0], ATT_WIDTH), F32)
    for h in range(ATT_HEADS):
        out = jnp.where(head == h, v[:, h:h + 1], out)
    return out


def _natural_order(src_ref, scr_ref):
    dil, n, w = src_ref.shape[1:]
    if dil == 1:
        return src_ref[0, 0].astype(F32)
    for r in range(dil):
        for c in range(w // LANES):
            scr_ref[c, pl.ds(r, n, stride=dil), :] = src_ref[0, r, :, c * LANES:(c + 1) * LANES].astype(F32)
    return jnp.concatenate([scr_ref[c] for c in range(w // LANES)], axis=1)


def _out_proj_kernel(ssd_ref, o0_ref, o1_ref, o2_ref, l0_ref, l1_ref, l2_ref, h_ref, w1_ref, w2_ref,
                     g_ref, b_ref, out_ref, so1, so2, sl1, sl2):
    lses = [_natural_order(r, s) for r, s in ((l0_ref, None), (l1_ref, sl1), (l2_ref, sl2))]
    mx = jnp.maximum(jnp.maximum(lses[0], lses[1]), lses[2])
    es = [jnp.exp(l - mx) for l in lses]
    den = es[0] + es[1] + es[2]
    att = jnp.zeros((h_ref.shape[1], ATT_WIDTH), F32)
    for e, o_ref, scr in zip(es, (o0_ref, o1_ref, o2_ref), (None, so1, so2)):
        att = att + _expand_att_heads(e / den) * _natural_order(o_ref, scr)
    y = jnp.dot(ssd_ref[0], w1_ref[...], preferred_element_type=F32)
    y = y + jnp.dot(att.astype(BF16), w2_ref[...], preferred_element_type=F32)
    out_ref[0] = _ln_rows(DN_ALPHA * h_ref[0] + y, g_ref[...], b_ref[...])


def _out_proj_ln(ssd, os_, lses, h, w_out, g, b, tm=512):
    bsz, s, D = h.shape
    dils = [d for _, d in DILATED_PATTERNS]
    row = lambda n: pl.BlockSpec((1, tm, n), lambda bi, i: (bi, i, 0))
    sub = lambda d, n: pl.BlockSpec((1, d, tm // d, n), lambda bi, i: (bi, 0, i, 0))
    w1 = w_out[:SSD_WIDTH]
    w2 = w_out[SSD_WIDTH:]
    return pl.pallas_call(
        _out_proj_kernel,
        grid=(bsz, s // tm),
        in_specs=[row(SSD_WIDTH)] + [sub(d, ATT_WIDTH) for d in dils] + [sub(d, LANES) for d in dils]
        + [row(D), _const_spec(w1.shape), _const_spec(w2.shape), _const_spec((1, D)), _const_spec((1, D))],
        out_specs=row(D),
        out_shape=jax.ShapeDtypeStruct((bsz, s, D), F32),
        scratch_shapes=[pltpu.VMEM((ATT_WIDTH // LANES, tm, LANES), F32)] * 2
        + [pltpu.VMEM((1, tm, LANES), F32)] * 2,
        compiler_params=_cparams("parallel", "parallel"),
        name="out_proj_ln",
    )(ssd, *os_, *lses, h, w1, w2, g.reshape(1, D), b.reshape(1, D))


def _router_kernel(h_ref, w_ref, id_ref, gate_ref):
    logits = lax.dot_general(w_ref[...], h_ref[...], (((1,), (1,)), ((), ())),
                             precision=lax.Precision.HIGHEST, preferred_element_type=F32)
    tm = logits.shape[1]
    row = lax.broadcasted_iota(I32, (8, tm), 0)
    lg = jnp.where(row < N_EXPERT_GROUPS, logits[0:8], NEG)
    gm = jnp.max(lg, 0, keepdims=True)
    gs = jnp.sum(jnp.exp(lg - gm), 0, keepdims=True)
    g_idx = jnp.min(jnp.where(lg == gm, row, 8), 0, keepdims=True)
    g_prob = 1.0 / gs
    el = jnp.zeros((8, tm), F32)
    for g in range(N_EXPERT_GROUPS):
        el = jnp.where(g_idx == g, logits[8 + 8 * g:16 + 8 * g], el)
    em = jnp.max(el, 0, keepdims=True)
    ee = jnp.exp(el - em)
    p = ee / jnp.sum(ee, 0, keepdims=True)
    p1 = jnp.max(p, 0, keepdims=True)
    i1 = jnp.min(jnp.where(p == p1, row, 8), 0, keepdims=True)
    pr = jnp.where(row == i1, -1.0, p)
    p2 = jnp.max(pr, 0, keepdims=True)
    i2 = jnp.min(jnp.where(pr == p2, row, 8), 0, keepdims=True)
    den = p1 + p2
    base = g_idx * EXPERTS_PER_GROUP
    id_ref[...] = jnp.where(row == 0, base + i1, jnp.where(row == 1, base + i2, 0))
    gate_ref[...] = jnp.where(row == 0, g_prob * p1 / den, jnp.where(row == 1, g_prob * p2 / den, 0.0))


def _router(h, w_router, tm=512):
    T, D = h.shape
    nr = w_router.shape[0]
    return pl.pallas_call(
        _router_kernel,
        grid=(T // tm,),
        in_specs=[pl.BlockSpec((tm, D), lambda i: (i, 0)), _const_spec((nr, D))],
        out_specs=[pl.BlockSpec((8, tm), lambda i: (0, i)), pl.BlockSpec((8, tm), lambda i: (0, i))],
        out_shape=[jax.ShapeDtypeStruct((8, T), I32), jax.ShapeDtypeStruct((8, T), F32)],
        compiler_params=_cparams("parallel"),
        name="router",
    )(h, w_router)


def _expert_kernel(be_ref, nu_ref, x_ref, wg_ref, wu_ref, wd_ref, o_ref):
    @pl.when(pl.program_id(0) < nu_ref[0])
    def _():
        x = x_ref[...]
        hg = jnp.dot(x, wg_ref[0], preferred_element_type=F32)
        hu = jnp.dot(x, wu_ref[0], preferred_element_type=F32)
        hdn = (_silu(hg) * hu).astype(BF16)
        o_ref[...] = jnp.dot(hdn, wd_ref[0], preferred_element_type=F32)


def _expert_ffn(xb, block_expert, n_used, w_gate, w_up, w_down):
    P, D = xb.shape
    nb = P // MOE_BLOCK
    grid_spec = pltpu.PrefetchScalarGridSpec(
        num_scalar_prefetch=2,
        grid=(nb,),
        in_specs=[pl.BlockSpec((MOE_BLOCK, D), lambda i, be, nu: (i, 0)),
                  pl.BlockSpec((1, D, EXPERT_FF), lambda i, be, nu: (be[i], 0, 0)),
                  pl.BlockSpec((1, D, EXPERT_FF), lambda i, be, nu: (be[i], 0, 0)),
                  pl.BlockSpec((1, EXPERT_FF, D), lambda i, be, nu: (be[i], 0, 0))],
        out_specs=pl.BlockSpec((MOE_BLOCK, D), lambda i, be, nu: (i, 0)),
    )
    return pl.pallas_call(
        _expert_kernel,
        grid_spec=grid_spec,
        out_shape=jax.ShapeDtypeStruct((P, D), F32),
        compiler_params=_cparams("arbitrary"),
        name="expert_ffn",
    )(block_expert, n_used, xb, w_gate, w_up, w_down)


def _combine_kernel(h_ref, y_ref, gate_ref, g_ref, b_ref, o_ref):
    d = h_ref.shape[1]
    gt = gate_ref[...]
    ffn = y_ref[:, :d] * gt[:, 0:1] + y_ref[:, d:] * gt[:, 1:2]
    o_ref[...] = _ln_rows(DN_ALPHA * h_ref[...] + ffn, g_ref[...], b_ref[...])


def _combine_ln(h, y2, gates, g, b, tm=512):
    T, D = h.shape
    return pl.pallas_call(
        _combine_kernel,
        grid=(T // tm,),
        in_specs=[pl.BlockSpec((tm, D), lambda i: (i, 0)), pl.BlockSpec((tm, TOP_K * D), lambda i: (i, 0)),
                  pl.BlockSpec((tm, TOP_K), lambda i: (i, 0)), _const_spec((1, D)), _const_spec((1, D))],
        out_specs=pl.BlockSpec((tm, D), lambda i: (i, 0)),
        out_shape=jax.ShapeDtypeStruct((T, D), F32),
        compiler_params=_cparams("parallel"),
        name="combine_ln",
    )(h, y2, gates, g.reshape(1, D), b.reshape(1, D))


def _mixer(h, lw, tabs):
    *atts, zx, dt_raw = _in_proj(h, lw["w_att"], lw["w_zx"], lw["w_dt"], tabs)
    os_, lses = [], []
    for gi in range(N_DIL):
        o, lse = _dilated_attention(atts[gi], gi)
        os_.append(o)
        lses.append(lse)
    xc = _conv_silu(zx, lw["conv_w"], lw["conv_b"])
    ssd = _ssd(xc, dt_raw, zx, lw["dt_bias"], lw["a_log"], lw["d_skip"], lw["ssd_norm_w"])
    return _out_proj_ln(ssd, os_, lses, h, lw["w_out"], lw["ln1_g"], lw["ln1_b"])


def _moe(h, lw):
    T, D = h.shape
    A = T * TOP_K
    ids, gates = _router(h, lw["w_router"])
    expert = ids[:TOP_K].T.reshape(A)
    order = jnp.argsort(expert, stable=True).astype(I32)
    inv = jnp.argsort(order).astype(I32)
    onehot = expert[:, None] == jnp.arange(N_EXPERTS, dtype=I32)[None, :]
    counts = jnp.sum(onehot, 0, dtype=I32)
    padded = (counts + MOE_BLOCK - 1) // MOE_BLOCK * MOE_BLOCK
    pad_end = jnp.cumsum(padded)
    pad_start = pad_end - padded
    start = jnp.cumsum(counts) - counts
    dest = inv + jnp.sum(jnp.where(onehot, (pad_start - start)[None, :], 0), -1, dtype=I32)
    n_blocks = -(-A // MOE_BLOCK) + N_EXPERTS
    blk0 = jnp.arange(n_blocks, dtype=I32) * MOE_BLOCK
    block_expert = jnp.minimum(jnp.sum(pad_end[None, :] <= blk0[:, None], -1, dtype=I32), N_EXPERTS - 1)
    n_used = (pad_end[-1] // MOE_BLOCK).astype(I32).reshape(1)
    sel = block_expert[:, None] == jnp.arange(N_EXPERTS, dtype=I32)[None, :]
    pick = lambda v: jnp.sum(jnp.where(sel, v[None, :], 0), -1, dtype=I32)[:, None]
    off = blk0[:, None] - pick(pad_start) + jnp.arange(MOE_BLOCK, dtype=I32)[None, :]
    src = jnp.clip(pick(start) + off, 0, A - 1)
    slot_token = jnp.where(off < pick(counts), order[src] // TOP_K, T).reshape(-1)
    xb = jnp.concatenate([h.astype(BF16), jnp.zeros((1, D), BF16)], 0)[slot_token]
    yb = _expert_ffn(xb, block_expert, n_used, lw["w_gate"], lw["w_up"], lw["w_down"])
    y2 = yb[dest].reshape(T, TOP_K * D)
    return _combine_ln(h, y2, gates[:TOP_K].T, lw["ln2_g"], lw["ln2_b"])


def _trunk(x, ln_in_g, ln_in_b, layers):
    b, s, D = x.shape
    tabs = _rope_tables(s)
    h = _layer_norm(x.reshape(b * s, D), ln_in_g, ln_in_b)
    for lw in layers:
        h = _mixer(h.reshape(b, s, D), lw, tabs)
        h = _moe(h.reshape(b * s, D), lw)
    return h.reshape(b, s, D)


def _prep_layers(w_in, conv_w, conv_b, a_log, dt_bias, d_skip, ssd_norm_w, w_out, ln1_g, ln1_b,
                 router_group, router_expert, w_gate, w_up, w_down, ln2_g, ln2_b):
    layers = []
    z0 = ATT_PROJ
    x0 = ATT_PROJ + SSD_WIDTH
    d0 = x0 + CONV_DIM
    for i in range(w_in.shape[0]):
        w = w_in[i]
        w_dt = jnp.pad(w[:, d0:], ((0, 0), (0, LANES - 2 * SSD_HEADS))).astype(BF16)
        w_router = jnp.concatenate([
            router_group[i].T, jnp.zeros((8 - N_EXPERT_GROUPS, D_MODEL), F32), router_expert[i].T], 0)
        layers.append(dict(
            w_att=w[:, :z0].astype(BF16),
            w_zx=jnp.concatenate([w[:, x0:d0], w[:, z0:x0]], 1).astype(BF16),
            w_dt=w_dt,
            conv_w=conv_w[i], conv_b=conv_b[i], a_log=a_log[i], dt_bias=dt_bias[i], d_skip=d_skip[i],
            ssd_norm_w=ssd_norm_w[i], w_out=w_out[i].astype(BF16), ln1_g=ln1_g[i], ln1_b=ln1_b[i],
            w_router=w_router, w_gate=w_gate[i].astype(BF16), w_up=w_up[i].astype(BF16),
            w_down=w_down[i].astype(BF16), ln2_g=ln2_g[i], ln2_b=ln2_b[i]))
    return layers


def kernel(x_prompt, x_sample, ln_in_g, ln_in_b, w_in, conv_w, conv_b, a_log, dt_bias, d_skip, ssd_norm_w,
           w_out, ln1_g, ln1_b, router_group, router_expert, w_gate, w_up, w_down, ln2_g, ln2_b):
    layers = _prep_layers(w_in, conv_w, conv_b, a_log, dt_bias, d_skip, ssd_norm_w, w_out, ln1_g, ln1_b,
                          router_group, router_expert, w_gate, w_up, w_down, ln2_g, ln2_b)
    y_prompt = _trunk(x_prompt, ln_in_g, ln_in_b, layers)
    y_sample = _trunk(x_sample, ln_in_g, ln_in_b, layers)
    return (y_prompt, y_sample)
```

```python
import functools

import numpy as np
import jax
import jax.numpy as jnp
from jax import lax
from jax.experimental import pallas as pl
from jax.experimental.pallas import tpu as pltpu

F32 = jnp.float32
BF16 = jnp.bfloat16
I32 = jnp.int32

D_MODEL = 1024
DEPTH = 4
HEAD_DIM = 64
DILATED_PATTERNS = ((128, 1), (512, 4), (2048, 16))
N_DIL = 3
ATT_HEADS = 8
ATT_WIDTH = ATT_HEADS * HEAD_DIM
ATT_PROJ = N_DIL * 3 * ATT_WIDTH
ROPE_THETA = 10000.0
SSD_HEADS = 8
SSD_WIDTH = SSD_HEADS * HEAD_DIM
SSD_GROUPS = 2
HEADS_PER_GROUP = SSD_HEADS // SSD_GROUPS
D_STATE = 128
CONV_K = 5
CONV_DIM = SSD_WIDTH + 2 * SSD_GROUPS * D_STATE
N_EXPERT_GROUPS = 4
EXPERTS_PER_GROUP = 8
N_EXPERTS = N_EXPERT_GROUPS * EXPERTS_PER_GROUP
TOP_K = 2
EXPERT_FF = 512
MOE_BLOCK = 256
DN_ALPHA = (2 * DEPTH) ** 0.25
LN_EPS = 1e-5
RMS_EPS = 1e-5
NEG = -1e30

LANES = 128
SSD_CHUNK = 128
ATT_SUB = 128
ATT_HALO = 64
CONV_HALO = 16
VMEM_LIMIT = 48 * 1024 * 1024


def _cparams(*sem):
    return pltpu.CompilerParams(dimension_semantics=sem, vmem_limit_bytes=VMEM_LIMIT)


def _const_spec(shape):
    nd = len(shape)
    return pl.BlockSpec(shape, lambda *_: (0,) * nd)


def _ln_rows(x, g, b):
    mu = jnp.mean(x, -1, keepdims=True)
    xc = x - mu
    var = jnp.mean(xc * xc, -1, keepdims=True)
    return xc * lax.rsqrt(var + LN_EPS) * g + b


def _silu(x):
    return x / (1.0 + jnp.exp(-x))


def _ln_kernel(x_ref, g_ref, b_ref, o_ref):
    o_ref[...] = _ln_rows(x_ref[...], g_ref[...], b_ref[...])


def _layer_norm(x, g, b, tm=512):
    T, D = x.shape
    return pl.pallas_call(
        _ln_kernel,
        grid=(T // tm,),
        in_specs=[pl.BlockSpec((tm, D), lambda i: (i, 0)), _const_spec((1, D)), _const_spec((1, D))],
        out_specs=pl.BlockSpec((tm, D), lambda i: (i, 0)),
        out_shape=jax.ShapeDtypeStruct((T, D), F32),
        compiler_params=_cparams("parallel"),
        name="ln_in",
    )(x, g.reshape(1, D), b.reshape(1, D))


def _rope_chunks(y, tab):
    cos = tab[:, :LANES]
    sin = tab[:, LANES:]
    lane = lax.broadcasted_iota(I32, (1, LANES), 1)
    first = (lane % HEAD_DIM) < (HEAD_DIM // 2)
    out = []
    for c in range(ATT_WIDTH // LANES):
        tc = y[:, c * LANES:(c + 1) * LANES]
        rot = jnp.where(first, pltpu.roll(tc, LANES - HEAD_DIM // 2, 1), pltpu.roll(tc, HEAD_DIM // 2, 1))
        out.append(tc * cos + rot * sin)
    return out


def _proj_kernel(x_ref, wa_ref, wz_ref, wd_ref, t0_ref, t1_ref, t2_ref,
                 a0_ref, a1_ref, a2_ref, oz_ref, od_ref, xc_ref, xs_ref, *, tm, tn):
    xb = x_ref[0].astype(BF16)
    nlc = x_ref.shape[2] // LANES
    for c in range(nlc):
        xc_ref[c] = x_ref[0, :, c * LANES:(c + 1) * LANES]
    for j in range(oz_ref.shape[2] // tn):
        sl = slice(j * tn, (j + 1) * tn)
        oz_ref[0, :, sl] = jnp.dot(xb, wz_ref[:, sl], preferred_element_type=F32).astype(oz_ref.dtype)
    od_ref[0] = jnp.dot(xb, wd_ref[...], preferred_element_type=F32)
    scale = HEAD_DIM ** -0.5
    nchunk = ATT_WIDTH // LANES
    for g, (t_ref, a_ref) in enumerate(((t0_ref, a0_ref), (t1_ref, a1_ref), (t2_ref, a2_ref))):
        dil = DILATED_PATTERNS[g][1]
        n = tm // dil
        if dil == 1:
            xp = xb
        else:
            for r in range(dil):
                for c in range(nlc):
                    xs_ref[r * n:(r + 1) * n, c * LANES:(c + 1) * LANES] = (
                        xc_ref[c, pl.ds(r, n, stride=dil), :].astype(BF16))
            xp = xs_ref[...]
        tab = t_ref[...].reshape(tm, 2 * LANES)
        for j in range(3):
            c0 = (g * 3 + j) * ATT_WIDTH
            y = jnp.dot(xp, wa_ref[:, c0:c0 + ATT_WIDTH], preferred_element_type=F32)
            if j < 2:
                chunks = _rope_chunks(y, tab)
                if j == 0:
                    chunks = [ch * scale for ch in chunks]
            else:
                chunks = [y[:, c * LANES:(c + 1) * LANES] for c in range(nchunk)]
            for c, ch in enumerate(chunks):
                chb = ch.astype(BF16)
                col = j * ATT_WIDTH + c * LANES
                for r in range(dil):
                    a_ref[0, r, :, col:col + LANES] = chb[r * n:(r + 1) * n]


def _in_proj(h, w_att, w_zx, w_dt, tabs, tm=512, tn=512):
    b, s, D = h.shape
    dils = [d for _, d in DILATED_PATTERNS]
    qkv = 3 * ATT_WIDTH
    nz, nd = w_zx.shape[1], w_dt.shape[1]
    row = lambda n: pl.BlockSpec((1, tm, n), lambda bi, i: (bi, i, 0))
    sub = lambda d, n: pl.BlockSpec((1, d, tm // d, n), lambda bi, i: (bi, 0, i, 0))
    return pl.pallas_call(
        functools.partial(_proj_kernel, tm=tm, tn=tn),
        grid=(b, s // tm),
        in_specs=[row(D)] + [_const_spec(w.shape) for w in (w_att, w_zx, w_dt)]
        + [pl.BlockSpec((d, tm // d, 2 * LANES), lambda bi, i: (0, i, 0)) for d in dils],
        out_specs=[sub(d, qkv) for d in dils] + [row(nz), row(nd)],
        out_shape=[jax.ShapeDtypeStruct((b, d, s // d, qkv), BF16) for d in dils]
        + [jax.ShapeDtypeStruct((b, s, nz), BF16), jax.ShapeDtypeStruct((b, s, nd), F32)],
        scratch_shapes=[pltpu.VMEM((D // LANES, tm, LANES), F32), pltpu.VMEM((tm, D), BF16)],
        compiler_params=_cparams("parallel", "parallel"),
        name="in_proj",
    )(h, w_att, w_zx, w_dt, *tabs)


def _attn_kernel(q_ref, kc_ref, kp_ref, kn_ref, vc_ref, vp_ref, vn_ref, o_ref, lse_ref, kbuf, vbuf,
                 *, bq, seq_len, half):
    qi = pl.program_id(2)
    kbuf[0:ATT_HALO] = kp_ref[0, 0]
    kbuf[ATT_HALO:ATT_HALO + bq] = kc_ref[0, 0]
    kbuf[ATT_HALO + bq:] = kn_ref[0, 0]
    vbuf[0:ATT_HALO] = vp_ref[0, 0]
    vbuf[ATT_HALO:ATT_HALO + bq] = vc_ref[0, 0]
    vbuf[ATT_HALO + bq:] = vn_ref[0, 0]

    nk = ATT_SUB + 2 * ATT_HALO
    npair = ATT_HEADS // 2
    ri = lax.broadcasted_iota(I32, (ATT_SUB, nk), 0)
    ci = lax.broadcasted_iota(I32, (ATT_SUB, nk), 1)
    band = jnp.abs(ci - ATT_HALO - ri) <= half
    lane = lax.broadcasted_iota(I32, (1, LANES), 1)
    even = lane < HEAD_DIM
    for sb in range(bq // ATT_SUB):
        r0 = sb * ATT_SUB
        kpos = qi * bq + r0 - ATT_HALO + ci
        mask = band & (kpos >= 0) & (kpos < seq_len)
        ss = []
        for j in range(npair):
            cs = slice(j * LANES, (j + 1) * LANES)
            qp = q_ref[0, 0, r0:r0 + ATT_SUB, cs]
            zero = jnp.zeros_like(qp)
            lhs = jnp.concatenate([jnp.where(even, qp, zero), jnp.where(even, zero, qp)], axis=0)
            ss.append(lax.dot_general(lhs, kbuf[r0:r0 + nk, cs], (((1,), (1,)), ((), ())),
                                      preferred_element_type=F32))
        s = jnp.stack(ss).reshape(ATT_HEADS, ATT_SUB, nk)
        s = jnp.where(mask[None], s, NEG)
        m = jnp.max(s, -1, keepdims=True)
        p = jnp.exp(s - m)
        l = jnp.sum(p, -1, keepdims=True)
        pb = p.astype(BF16).reshape(npair, 2 * ATT_SUB, nk)
        l2 = l.reshape(npair, 2 * ATT_SUB, 1)
        lse = m + jnp.log(l)
        for j in range(npair):
            cs = slice(j * LANES, (j + 1) * LANES)
            o2 = jnp.dot(pb[j], vbuf[r0:r0 + nk, cs], preferred_element_type=F32) / l2[j]
            o_ref[0, 0, r0:r0 + ATT_SUB, cs] = jnp.where(even, o2[:ATT_SUB], o2[ATT_SUB:]).astype(o_ref.dtype)
        lse_tile = jnp.zeros((ATT_SUB, LANES), F32)
        for h in range(ATT_HEADS):
            lse_tile = jnp.where(lane == h, lse[h], lse_tile)
        lse_ref[0, 0, r0:r0 + ATT_SUB, :] = lse_tile


def _dilated_attention(att, gi, bq=256):
    win, dil = DILATED_PATTERNS[gi]
    half = win // (2 * dil)
    assert half <= ATT_HALO
    b, _, L, _ = att.shape
    bq = min(bq, L)
    nq = L // bq
    hb = bq // ATT_HALO
    nhb = L // ATT_HALO
    cur = lambda j: (lambda bi, r, qi: (bi, r, qi, j))
    prev = lambda j: (lambda bi, r, qi: (bi, r, jnp.maximum(qi * hb - 1, 0), j))
    nxt = lambda j: (lambda bi, r, qi: (bi, r, jnp.minimum((qi + 1) * hb, nhb - 1), j))
    blk = lambda n, f: pl.BlockSpec((1, 1, n, ATT_WIDTH), f)
    return pl.pallas_call(
        functools.partial(_attn_kernel, bq=bq, seq_len=L, half=half),
        grid=(b, dil, nq),
        in_specs=[blk(bq, cur(0)),
                  blk(bq, cur(1)), blk(ATT_HALO, prev(1)), blk(ATT_HALO, nxt(1)),
                  blk(bq, cur(2)), blk(ATT_HALO, prev(2)), blk(ATT_HALO, nxt(2))],
        out_specs=[pl.BlockSpec((1, 1, bq, ATT_WIDTH), lambda bi, r, qi: (bi, r, qi, 0)),
                   pl.BlockSpec((1, 1, bq, LANES), lambda bi, r, qi: (bi, r, qi, 0))],
        out_shape=[jax.ShapeDtypeStruct((b, dil, L, ATT_WIDTH), BF16),
                   jax.ShapeDtypeStruct((b, dil, L, LANES), F32)],
        scratch_shapes=[pltpu.VMEM((bq + 2 * ATT_HALO, ATT_WIDTH), BF16),
                        pltpu.VMEM((bq + 2 * ATT_HALO, ATT_WIDTH), BF16)],
        compiler_params=_cparams("parallel", "parallel", "parallel"),
        name=f"dil_attn_{gi}",
    )(att, att, att, att, att, att, att)


def _rope_tables(s):
    half = HEAD_DIM // 2
    inv = ROPE_THETA ** (-jnp.arange(half, dtype=F32) / half)
    ang = jnp.arange(s).astype(F32)[:, None] * inv[None, :]
    cos, sin = jnp.cos(ang), jnp.sin(ang)
    cos_h = jnp.concatenate([cos, cos], -1)
    sin_h = jnp.concatenate([-sin, sin], -1)
    rep = LANES // HEAD_DIM
    tab = jnp.concatenate([jnp.tile(cos_h, (1, rep)), jnp.tile(sin_h, (1, rep))], -1)
    return [tab.reshape(s // d, d, 2 * LANES).transpose(1, 0, 2) for _, d in DILATED_PATTERNS]


def _conv_kernel(c_ref, p_ref, n_ref, w_ref, b_ref, o_ref, buf, *, ts):
    i = pl.program_id(1)
    last = pl.num_programs(1) - 1
    pad = (CONV_K - 1) // 2
    buf[0:CONV_HALO] = jnp.where(i > 0, p_ref[0].astype(F32), 0.0)
    buf[CONV_HALO:CONV_HALO + ts] = c_ref[0].astype(F32)
    buf[CONV_HALO + ts:] = jnp.where(i < last, n_ref[0].astype(F32), 0.0)
    acc = jnp.zeros((ts, CONV_DIM), F32) + b_ref[...]
    for j in range(CONV_K):
        acc = acc + w_ref[j:j + 1, :] * buf[CONV_HALO - pad + j:CONV_HALO - pad + j + ts, :]
    o_ref[0] = _silu(acc).astype(o_ref.dtype)


def _conv_silu(zx_v, conv_w, conv_b, ts=512):
    b, s, _ = zx_v.shape
    ts = min(ts, s)
    r = ts // CONV_HALO
    nh = s // CONV_HALO
    w8 = jnp.zeros((8, CONV_DIM), F32).at[:CONV_K].set(conv_w)
    return pl.pallas_call(
        functools.partial(_conv_kernel, ts=ts),
        grid=(b, s // ts),
        in_specs=[pl.BlockSpec((1, ts, CONV_DIM), lambda bi, i: (bi, i, 0)),
                  pl.BlockSpec((1, CONV_HALO, CONV_DIM), lambda bi, i: (bi, jnp.maximum(i * r - 1, 0), 0)),
                  pl.BlockSpec((1, CONV_HALO, CONV_DIM), lambda bi, i: (bi, jnp.minimum((i + 1) * r, nh - 1), 0)),
                  _const_spec((8, CONV_DIM)), _const_spec((1, CONV_DIM))],
        out_specs=pl.BlockSpec((1, ts, CONV_DIM), lambda bi, i: (bi, i, 0)),
        out_shape=jax.ShapeDtypeStruct((b, s, CONV_DIM), BF16),
        scratch_shapes=[pltpu.VMEM((ts + 2 * CONV_HALO, CONV_DIM), F32)],
        compiler_params=_cparams("parallel", "parallel"),
        name="conv_silu",
    )(zx_v, zx_v, zx_v, w8, conv_b.reshape(1, CONV_DIM))


def _expand_heads(v, off):
    head = lax.broadcasted_iota(I32, (1, SSD_WIDTH), 1) // HEAD_DIM
    out = jnp.zeros((v.shape[0], SSD_WIDTH), F32)
    for h in range(SSD_HEADS):
        out = jnp.where(head == h, v[:, off + h:off + h + 1], out)
    return out


def _head_selector(off):
    r = lax.broadcasted_iota(I32, (LANES, SSD_WIDTH), 0)
    c = lax.broadcasted_iota(I32, (LANES, SSD_WIDTH), 1)
    return (r == c // HEAD_DIM + off).astype(BF16)


def _expand_heads_mxu(v, sel):
    hi = v.astype(BF16)
    lo = (v - hi.astype(F32)).astype(BF16)
    return (jnp.dot(hi, sel, preferred_element_type=F32) + jnp.dot(lo, sel, preferred_element_type=F32))


def _softplus(x):
    return jnp.maximum(x, 0.0) + jnp.log(1.0 + jnp.exp(-jnp.abs(x)))


def _ssd_chunk(xc, dtr, bias, a_row, state_ref, *, reverse, off):
    Q = SSD_CHUNK
    hi = lax.Precision.HIGHEST
    dt = _softplus(dtr + bias)
    a = dt * a_row
    ri = lax.broadcasted_iota(I32, (Q, Q), 0)
    ci = lax.broadcasted_iota(I32, (Q, Q), 1)
    keep = (ci >= ri) if reverse else (ci <= ri)
    tri = keep.astype(F32)
    cum = jnp.dot(tri, a, precision=hi, preferred_element_type=F32)
    cum_t = lax.dot_general(a.T, tri, (((1,), (1,)), ((), ())), precision=hi,
                            preferred_element_type=F32)
    edge = 0 if reverse else Q - 1
    tot = cum[edge:edge + 1, :]
    sel = _head_selector(off)
    dt512 = _expand_heads_mxu(dt, sel)
    dec512 = _expand_heads_mxu(jnp.exp(tot - cum), sel)
    ecum512 = _expand_heads_mxu(jnp.exp(cum), sel)
    etot512 = _expand_heads(jnp.exp(tot), off)
    xs = xc[:, :SSD_WIDTH].astype(F32)
    xdt = xs * dt512
    xdt_b = xdt.astype(BF16)
    xdd_b = (xdt * dec512).astype(BF16)
    gw = HEADS_PER_GROUP * HEAD_DIM
    lane = lax.broadcasted_iota(I32, (1, LANES), 1)
    even = lane < HEAD_DIM
    ys = []
    for g in range(SSD_GROUPS):
        bg = xc[:, SSD_WIDTH + g * D_STATE:SSD_WIDTH + (g + 1) * D_STATE]
        cg = xc[:, SSD_WIDTH + (SSD_GROUPS + g) * D_STATE:SSD_WIDTH + (SSD_GROUPS + g + 1) * D_STATE]
        cb = lax.dot_general(cg, bg, (((1,), (1,)), ((), ())), preferred_element_type=F32)
        sg = state_ref[g]
        yoff = jnp.dot(cg, sg.astype(BF16), preferred_element_type=F32)
        for pr in range(HEADS_PER_GROUP // 2):
            ms = []
            for hh in (2 * pr, 2 * pr + 1):
                ln = off + g * HEADS_PER_GROUP + hh
                seg = cum[:, ln:ln + 1] - cum_t[ln:ln + 1, :]
                lmat = jnp.where(keep, jnp.exp(jnp.where(keep, seg, 0.0)), 0.0)
                ms.append((cb * lmat).astype(BF16))
            c0 = g * gw + pr * LANES
            yd2 = jnp.dot(jnp.concatenate(ms, axis=0), xdt_b[:, c0:c0 + LANES], preferred_element_type=F32)
            yd = jnp.where(even, yd2[:Q], yd2[Q:])
            ys.append(yd + yoff[:, pr * LANES:(pr + 1) * LANES] * ecum512[:, c0:c0 + LANES])
        bg_t = bg.astype(F32).T.astype(BF16)
        state_ref[g] = sg * etot512[:, g * gw:(g + 1) * gw] + jnp.dot(
            bg_t, xdd_b[:, g * gw:(g + 1) * gw], preferred_element_type=F32)
    return ys


def _ssd_fwd_kernel(x_ref, dt_ref, bias_ref, a_ref, y_ref, state_ref, *, nch):
    @pl.when(pl.program_id(1) == 0)
    def _():
        state_ref[...] = jnp.zeros_like(state_ref)

    for c in range(nch):
        rows = slice(c * SSD_CHUNK, (c + 1) * SSD_CHUNK)
        ys = _ssd_chunk(x_ref[0, rows, :], dt_ref[0, rows, :], bias_ref[...], a_ref[...], state_ref,
                        reverse=False, off=0)
        for j, y in enumerate(ys):
            y_ref[0, rows, j * LANES:(j + 1) * LANES] = y


def _ssd_bwd_kernel(x_ref, dt_ref, bias_ref, a_ref, yf_ref, z_ref, dskip_ref, nw_ref, o_ref, state_ref,
                    ybuf, *, nch):
    @pl.when(pl.program_id(1) == 0)
    def _():
        state_ref[...] = jnp.zeros_like(state_ref)

    for c in reversed(range(nch)):
        rows = slice(c * SSD_CHUNK, (c + 1) * SSD_CHUNK)
        ys = _ssd_chunk(x_ref[0, rows, :], dt_ref[0, rows, :], bias_ref[...], a_ref[...], state_ref,
                        reverse=True, off=SSD_HEADS)
        for j, y in enumerate(ys):
            ybuf[rows, j * LANES:(j + 1) * LANES] = y
    xs = x_ref[0, :, :SSD_WIDTH].astype(F32)
    y = yf_ref[0] + ybuf[...] + xs * dskip_ref[...]
    gy = y * _silu(z_ref[0].astype(F32))
    gw = SSD_WIDTH // SSD_GROUPS
    for g in range(SSD_GROUPS):
        part = gy[:, g * gw:(g + 1) * gw]
        ms = jnp.mean(part * part, -1, keepdims=True)
        o_ref[0, :, g * gw:(g + 1) * gw] = (part * lax.rsqrt(ms + RMS_EPS) * nw_ref[:, g * gw:(g + 1) * gw]
                                             ).astype(o_ref.dtype)


def _ssd(xc, dt_v, zx_v, dt_bias, a_log, d_skip, norm_w, nch=4):
    b, s, _ = xc.shape
    nch = min(nch, s // SSD_CHUNK)
    R = nch * SSD_CHUNK
    n = s // R
    a_neg = -jnp.exp(a_log.astype(F32))
    pad = LANES - 2 * SSD_HEADS
    bias = jnp.pad(dt_bias.astype(F32).reshape(1, 2 * SSD_HEADS), ((0, 0), (0, pad)))
    a_f = jnp.pad(a_neg[0].reshape(1, SSD_HEADS), ((0, 0), (0, LANES - SSD_HEADS)))
    a_b = jnp.pad(a_neg[1].reshape(1, SSD_HEADS), ((0, 0), (SSD_HEADS, pad)))
    dskip = jnp.repeat(d_skip.astype(F32), HEAD_DIM).reshape(1, SSD_WIDTH)
    state = pltpu.VMEM((SSD_GROUPS, D_STATE, HEADS_PER_GROUP * HEAD_DIM), F32)
    fwd = lambda bi, i: (bi, i, 0)
    rev = lambda bi, i: (bi, n - 1 - i, 0)
    y_f = pl.pallas_call(
        functools.partial(_ssd_fwd_kernel, nch=nch),
        grid=(b, n),
        in_specs=[pl.BlockSpec((1, R, CONV_DIM), fwd), pl.BlockSpec((1, R, LANES), fwd),
                  _const_spec((1, LANES)), _const_spec((1, LANES))],
        out_specs=pl.BlockSpec((1, R, SSD_WIDTH), fwd),
        out_shape=jax.ShapeDtypeStruct((b, s, SSD_WIDTH), F32),
        scratch_shapes=[state],
        compiler_params=_cparams("parallel", "arbitrary"),
        name="ssd_fwd",
    )(xc, dt_v, bias, a_f)
    out = pl.pallas_call(
        functools.partial(_ssd_bwd_kernel, nch=nch),
        grid=(b, n),
        in_specs=[pl.BlockSpec((1, R, CONV_DIM), rev), pl.BlockSpec((1, R, LANES), rev),
                  _const_spec((1, LANES)), _const_spec((1, LANES)),
                  pl.BlockSpec((1, R, SSD_WIDTH), rev),
                  pl.BlockSpec((1, R, SSD_WIDTH), lambda bi, i: (bi, n - 1 - i, CONV_DIM // SSD_WIDTH)),
                  _const_spec((1, SSD_WIDTH)), _const_spec((1, SSD_WIDTH))],
        out_specs=pl.BlockSpec((1, R, SSD_WIDTH), rev),
        out_shape=jax.ShapeDtypeStruct((b, s, SSD_WIDTH), BF16),
        scratch_shapes=[state, pltpu.VMEM((R, SSD_WIDTH), F32)],
        compiler_params=_cparams("parallel", "arbitrary"),
        name="ssd_bwd",
    )(xc, dt_v, bias, a_b, y_f, zx_v, dskip, norm_w.astype(F32).reshape(1, SSD_WIDTH))
    return out


def _expand_att_heads(v):
    head = lax.broadcasted_iota(I32, (1, ATT_WIDTH), 1) // HEAD_DIM
    out = jnp.zeros((v.shape[0], ATT_WIDTH), F32)
    for h in range(ATT_HEADS):
        out = jnp.where(head == h, v[:, h:h + 1], out)
    return out


def _natural_order(src_ref, scr_ref):
    dil, n, w = src_ref.shape[1:]
    if dil == 1:
        return src_ref[0, 0].astype(F32)
    for r in range(dil):
        for c in range(w // LANES):
            scr_ref[c, pl.ds(r, n, stride=dil), :] = src_ref[0, r, :, c * LANES:(c + 1) * LANES].astype(F32)
    return jnp.concatenate([scr_ref[c] for c in range(w // LANES)], axis=1)


def _out_proj_kernel(ssd_ref, o0_ref, o1_ref, o2_ref, l0_ref, l1_ref, l2_ref, h_ref, w1_ref, w2_ref,
                     g_ref, b_ref, out_ref, so1, so2, sl1, sl2):
    lses = [_natural_order(r, s) for r, s in ((l0_ref, None), (l1_ref, sl1), (l2_ref, sl2))]
    mx = jnp.maximum(jnp.maximum(lses[0], lses[1]), lses[2])
    es = [jnp.exp(l - mx) for l in lses]
    den = es[0] + es[1] + es[2]
    att = jnp.zeros((h_ref.shape[1], ATT_WIDTH), F32)
    for e, o_ref, scr in zip(es, (o0_ref, o1_ref, o2_ref), (None, so1, so2)):
        att = att + _expand_att_heads(e / den) * _natural_order(o_ref, scr)
    y = jnp.dot(ssd_ref[0], w1_ref[...], preferred_element_type=F32)
    y = y + jnp.dot(att.astype(BF16), w2_ref[...], preferred_element_type=F32)
    out_ref[0] = _ln_rows(DN_ALPHA * h_ref[0] + y, g_ref[...], b_ref[...])


def _out_proj_ln(ssd, os_, lses, h, w_out, g, b, tm=512):
    bsz, s, D = h.shape
    dils = [d for _, d in DILATED_PATTERNS]
    row = lambda n: pl.BlockSpec((1, tm, n), lambda bi, i: (bi, i, 0))
    sub = lambda d, n: pl.BlockSpec((1, d, tm // d, n), lambda bi, i: (bi, 0, i, 0))
    w1 = w_out[:SSD_WIDTH]
    w2 = w_out[SSD_WIDTH:]
    return pl.pallas_call(
        _out_proj_kernel,
        grid=(bsz, s // tm),
        in_specs=[row(SSD_WIDTH)] + [sub(d, ATT_WIDTH) for d in dils] + [sub(d, LANES) for d in dils]
        + [row(D), _const_spec(w1.shape), _const_spec(w2.shape), _const_spec((1, D)), _const_spec((1, D))],
        out_specs=row(D),
        out_shape=jax.ShapeDtypeStruct((bsz, s, D), F32),
        scratch_shapes=[pltpu.VMEM((ATT_WIDTH // LANES, tm, LANES), F32)] * 2
        + [pltpu.VMEM((1, tm, LANES), F32)] * 2,
        compiler_params=_cparams("parallel", "parallel"),
        name="out_proj_ln",
    )(ssd, *os_, *lses, h, w1, w2, g.reshape(1, D), b.reshape(1, D))


def _router_kernel(h_ref, w_ref, id_ref, gate_ref):
    logits = lax.dot_general(w_ref[...], h_ref[...], (((1,), (1,)), ((), ())),
                             precision=lax.Precision.HIGHEST, preferred_element_type=F32)
    tm = logits.shape[1]
    row = lax.broadcasted_iota(I32, (8, tm), 0)
    lg = jnp.where(row < N_EXPERT_GROUPS, logits[0:8], NEG)
    gm = jnp.max(lg, 0, keepdims=True)
    gs = jnp.sum(jnp.exp(lg - gm), 0, keepdims=True)
    g_idx = jnp.min(jnp.where(lg == gm, row, 8), 0, keepdims=True)
    g_prob = 1.0 / gs
    el = jnp.zeros((8, tm), F32)
    for g in range(N_EXPERT_GROUPS):
        el = jnp.where(g_idx == g, logits[8 + 8 * g:16 + 8 * g], el)
    em = jnp.max(el, 0, keepdims=True)
    ee = jnp.exp(el - em)
    p = ee / jnp.sum(ee, 0, keepdims=True)
    p1 = jnp.max(p, 0, keepdims=True)
    i1 = jnp.min(jnp.where(p == p1, row, 8), 0, keepdims=True)
    pr = jnp.where(row == i1, -1.0, p)
    p2 = jnp.max(pr, 0, keepdims=True)
    i2 = jnp.min(jnp.where(pr == p2, row, 8), 0, keepdims=True)
    den = p1 + p2
    base = g_idx * EXPERTS_PER_GROUP
    id_ref[...] = jnp.where(row == 0, base + i1, jnp.where(row == 1, base + i2, 0))
    gate_ref[...] = jnp.where(row == 0, g_prob * p1 / den, jnp.where(row == 1, g_prob * p2 / den, 0.0))


def _router(h, w_router, tm=512):
    T, D = h.shape
    nr = w_router.shape[0]
    return pl.pallas_call(
        _router_kernel,
        grid=(T // tm,),
        in_specs=[pl.BlockSpec((tm, D), lambda i: (i, 0)), _const_spec((nr, D))],
        out_specs=[pl.BlockSpec((8, tm), lambda i: (0, i)), pl.BlockSpec((8, tm), lambda i: (0, i))],
        out_shape=[jax.ShapeDtypeStruct((8, T), I32), jax.ShapeDtypeStruct((8, T), F32)],
        compiler_params=_cparams("parallel"),
        name="router",
    )(h, w_router)


def _expert_kernel(be_ref, nu_ref, tokc_ref, tokn_ref, dst_ref, h_hbm, wg_ref, wu_ref, wd_ref, y_hbm,
                   xbuf, ybuf, gsem, ssem, *, spare_rows):
    i = pl.program_id(0)
    nu = nu_ref[0]
    slot = i % 2

    def gather_start(tok_ref, s):
        def body(j, c):
            pltpu.make_async_copy(h_hbm.at[pl.ds(tok_ref[0, 0, j], 1)], xbuf.at[s, pl.ds(j, 1)],
                                  gsem.at[s]).start()
            return c
        lax.fori_loop(0, MOE_BLOCK, body, 0, unroll=16)

    def gather_wait(s):
        pltpu.make_async_copy(h_hbm.at[pl.ds(0, MOE_BLOCK)], xbuf.at[s], gsem.at[s]).wait()

    def scatter_wait(s):
        pltpu.make_async_copy(ybuf.at[s], y_hbm.at[pl.ds(0, MOE_BLOCK)], ssem.at[s]).wait()

    @pl.when(i == 0)
    def _():
        gather_start(tokc_ref, 0)
        ybuf[1] = jnp.zeros(ybuf.shape[1:], F32)
        fills = [pltpu.make_async_copy(ybuf.at[1], y_hbm.at[pl.ds(r, MOE_BLOCK)], ssem.at[1])
                 for r in spare_rows]
        for f in fills:
            f.start()
        for f in fills:
            f.wait()

    @pl.when(i + 1 < nu)
    def _():
        gather_start(tokn_ref, 1 - slot)

    @pl.when(i < nu)
    def _():
        gather_wait(slot)

        @pl.when(i >= 2)
        def _():
            scatter_wait(slot)

        x = xbuf[slot].astype(BF16)
        hg = jnp.dot(x, wg_ref[0], preferred_element_type=F32)
        hu = jnp.dot(x, wu_ref[0], preferred_element_type=F32)
        hdn = (_silu(hg) * hu).astype(BF16)
        ybuf[slot] = jnp.dot(hdn, wd_ref[0], preferred_element_type=F32)

        def body(j, c):
            pltpu.make_async_copy(ybuf.at[slot, pl.ds(j, 1)], y_hbm.at[pl.ds(dst_ref[0, 0, j], 1)],
                                  ssem.at[slot]).start()
            return c
        lax.fori_loop(0, MOE_BLOCK, body, 0, unroll=16)

        @pl.when(i == nu - 1)
        def _():
            @pl.when(i >= 1)
            def _():
                scatter_wait(1 - slot)
            scatter_wait(slot)


def _expert_ffn(h, slot_token, slot_dst, block_expert, n_used, w_gate, w_up, w_down, n_rows, spare_rows):
    T, D = h.shape
    nb = slot_token.shape[0]
    smem = lambda f: pl.BlockSpec((1, 1, MOE_BLOCK), f, memory_space=pltpu.SMEM)
    grid_spec = pltpu.PrefetchScalarGridSpec(
        num_scalar_prefetch=2,
        grid=(nb,),
        in_specs=[smem(lambda i, be, nu: (i, 0, 0)),
                  smem(lambda i, be, nu: (jnp.minimum(i + 1, nb - 1), 0, 0)),
                  smem(lambda i, be, nu: (i, 0, 0)),
                  pl.BlockSpec(memory_space=pl.ANY),
                  pl.BlockSpec((1, D, EXPERT_FF), lambda i, be, nu: (be[i], 0, 0)),
                  pl.BlockSpec((1, D, EXPERT_FF), lambda i, be, nu: (be[i], 0, 0)),
                  pl.BlockSpec((1, EXPERT_FF, D), lambda i, be, nu: (be[i], 0, 0))],
        out_specs=pl.BlockSpec(memory_space=pl.ANY),
        scratch_shapes=[pltpu.VMEM((2, MOE_BLOCK, D), F32), pltpu.VMEM((2, MOE_BLOCK, D), F32),
                        pltpu.SemaphoreType.DMA((2,)), pltpu.SemaphoreType.DMA((2,))],
    )
    return pl.pallas_call(
        functools.partial(_expert_kernel, spare_rows=spare_rows),
        grid_spec=grid_spec,
        out_shape=jax.ShapeDtypeStruct((n_rows, D), F32),
        compiler_params=_cparams("arbitrary"),
        name="expert_ffn",
    )(block_expert, n_used, slot_token, slot_token, slot_dst, h, w_gate, w_up, w_down)


def _combine_kernel(h_ref, y0_ref, y1_ref, gate_ref, g_ref, b_ref, o_ref):
    gt = gate_ref[...]
    ffn = y0_ref[0] * gt[:, 0:1] + y1_ref[0] * gt[:, 1:2]
    o_ref[...] = _ln_rows(DN_ALPHA * h_ref[...] + ffn, g_ref[...], b_ref[...])


def _combine_ln(h, y2, gates, g, b, tm=512):
    T, D = h.shape
    return pl.pallas_call(
        _combine_kernel,
        grid=(T // tm,),
        in_specs=[pl.BlockSpec((tm, D), lambda i: (i, 0)),
                  pl.BlockSpec((1, tm, D), lambda i: (0, i, 0)), pl.BlockSpec((1, tm, D), lambda i: (1, i, 0)),
                  pl.BlockSpec((tm, TOP_K), lambda i: (i, 0)), _const_spec((1, D)), _const_spec((1, D))],
        out_specs=pl.BlockSpec((tm, D), lambda i: (i, 0)),
        out_shape=jax.ShapeDtypeStruct((T, D), F32),
        compiler_params=_cparams("parallel"),
        name="combine_ln",
    )(h, y2, y2, gates, g.reshape(1, D), b.reshape(1, D))


def _mixer(h, lw, tabs):
    *atts, zx, dt_raw = _in_proj(h, lw["w_att"], lw["w_zx"], lw["w_dt"], tabs)
    os_, lses = [], []
    for gi in range(N_DIL):
        o, lse = _dilated_attention(atts[gi], gi)
        os_.append(o)
        lses.append(lse)
    xc = _conv_silu(zx, lw["conv_w"], lw["conv_b"])
    ssd = _ssd(xc, dt_raw, zx, lw["dt_bias"], lw["a_log"], lw["d_skip"], lw["ssd_norm_w"])
    return _out_proj_ln(ssd, os_, lses, h, lw["w_out"], lw["ln1_g"], lw["ln1_b"])


def _moe(h, lw):
    T, D = h.shape
    A = T * TOP_K
    ids, gates = _router(h, lw["w_router"])
    expert = ids[:TOP_K].T.reshape(A)
    order = jnp.argsort(expert, stable=True).astype(I32)
    onehot = expert[:, None] == jnp.arange(N_EXPERTS, dtype=I32)[None, :]
    counts = jnp.sum(onehot, 0, dtype=I32)
    padded = (counts + MOE_BLOCK - 1) // MOE_BLOCK * MOE_BLOCK
    pad_end = jnp.cumsum(padded)
    pad_start = pad_end - padded
    start = jnp.cumsum(counts) - counts
    n_blocks = -(-A // MOE_BLOCK) + N_EXPERTS
    blk0 = jnp.arange(n_blocks, dtype=I32) * MOE_BLOCK
    block_expert = jnp.minimum(jnp.sum(pad_end[None, :] <= blk0[:, None], -1, dtype=I32), N_EXPERTS - 1)
    n_used = (pad_end[-1] // MOE_BLOCK).astype(I32).reshape(1)
    sel = block_expert[:, None] == jnp.arange(N_EXPERTS, dtype=I32)[None, :]
    pick = lambda v: jnp.sum(jnp.where(sel, v[None, :], 0), -1, dtype=I32)[:, None]
    lane = jnp.arange(MOE_BLOCK, dtype=I32)[None, :]
    off = blk0[:, None] - pick(pad_start) + lane
    valid = off < pick(counts)
    asg = order[jnp.clip(pick(start) + off, 0, A - 1)]
    plane = T + MOE_BLOCK
    spare = (jnp.arange(n_blocks, dtype=I32)[:, None] % 2) * plane + T + lane
    slot_token = jnp.where(valid, asg // TOP_K, 0).reshape(n_blocks, 1, MOE_BLOCK)
    slot_dst = jnp.where(valid, (asg % TOP_K) * plane + asg // TOP_K, spare).reshape(n_blocks, 1, MOE_BLOCK)
    y2 = _expert_ffn(h, slot_token, slot_dst, block_expert, n_used, lw["w_gate"], lw["w_up"], lw["w_down"],
                     TOP_K * plane, tuple(k * plane + T for k in range(TOP_K)))
    return _combine_ln(h, y2.reshape(TOP_K, plane, D), gates[:TOP_K].T, lw["ln2_g"], lw["ln2_b"])


def _trunk(x, ln_in_g, ln_in_b, layers):
    b, s, D = x.shape
    tabs = _rope_tables(s)
    h = _layer_norm(x.reshape(b * s, D), ln_in_g, ln_in_b)
    for lw in layers:
        h = _mixer(h.reshape(b, s, D), lw, tabs)
        h = _moe(h.reshape(b * s, D), lw)
    return h.reshape(b, s, D)


def _prep_layers(w_in, conv_w, conv_b, a_log, dt_bias, d_skip, ssd_norm_w, w_out, ln1_g, ln1_b,
                 router_group, router_expert, w_gate, w_up, w_down, ln2_g, ln2_b):
    layers = []
    z0 = ATT_PROJ
    x0 = ATT_PROJ + SSD_WIDTH
    d0 = x0 + CONV_DIM
    for i in range(w_in.shape[0]):
        w = w_in[i]
        w_dt = jnp.pad(w[:, d0:], ((0, 0), (0, LANES - 2 * SSD_HEADS))).astype(BF16)
        w_router = jnp.concatenate([
            router_group[i].T, jnp.zeros((8 - N_EXPERT_GROUPS, D_MODEL), F32), router_expert[i].T], 0)
        layers.append(dict(
            w_att=w[:, :z0].astype(BF16),
            w_zx=jnp.concatenate([w[:, x0:d0], w[:, z0:x0]], 1).astype(BF16),
            w_dt=w_dt,
            conv_w=conv_w[i], conv_b=conv_b[i], a_log=a_log[i], dt_bias=dt_bias[i], d_skip=d_skip[i],
            ssd_norm_w=ssd_norm_w[i], w_out=w_out[i].astype(BF16), ln1_g=ln1_g[i], ln1_b=ln1_b[i],
            w_router=w_router, w_gate=w_gate[i].astype(BF16), w_up=w_up[i].astype(BF16),
            w_down=w_down[i].astype(BF16), ln2_g=ln2_g[i], ln2_b=ln2_b[i]))
    return layers


def kernel(x_prompt, x_sample, ln_in_g, ln_in_b, w_in, conv_w, conv_b, a_log, dt_bias, d_skip, ssd_norm_w,
           w_out, ln1_g, ln1_b, router_group, router_expert, w_gate, w_up, w_down, ln2_g, ln2_b):
    layers = _prep_layers(w_in, conv_w, conv_b, a_log, dt_bias, d_skip, ssd_norm_w, w_out, ln1_g, ln1_b,
                          router_group, router_expert, w_gate, w_up, w_down, ln2_g, ln2_b)
    y_prompt = _trunk(x_prompt, ln_in_g, ln_in_b, layers)
    y_sample = _trunk(x_sample, ln_in_g, ln_in_b, layers)
    return (y_prompt, y_sample)
```

```python
import functools

import numpy as np
import jax
import jax.numpy as jnp
from jax import lax
from jax.experimental import pallas as pl
from jax.experimental.pallas import tpu as pltpu

F32 = jnp.float32
BF16 = jnp.bfloat16
I32 = jnp.int32

D_MODEL = 1024
DEPTH = 4
HEAD_DIM = 64
DILATED_PATTERNS = ((128, 1), (512, 4), (2048, 16))
N_DIL = 3
ATT_HEADS = 8
ATT_WIDTH = ATT_HEADS * HEAD_DIM
ATT_PROJ = N_DIL * 3 * ATT_WIDTH
ROPE_THETA = 10000.0
SSD_HEADS = 8
SSD_WIDTH = SSD_HEADS * HEAD_DIM
SSD_GROUPS = 2
HEADS_PER_GROUP = SSD_HEADS // SSD_GROUPS
D_STATE = 128
CONV_K = 5
CONV_DIM = SSD_WIDTH + 2 * SSD_GROUPS * D_STATE
N_EXPERT_GROUPS = 4
EXPERTS_PER_GROUP = 8
N_EXPERTS = N_EXPERT_GROUPS * EXPERTS_PER_GROUP
TOP_K = 2
EXPERT_FF = 512
MOE_BLOCK = 256
DN_ALPHA = (2 * DEPTH) ** 0.25
LN_EPS = 1e-5
RMS_EPS = 1e-5
NEG = -1e30

LANES = 128
SSD_CHUNK = 128
ATT_SUB = 128
ATT_HALO = 64
CONV_HALO = 16
NBUF = 3
VMEM_LIMIT = 48 * 1024 * 1024


def _cparams(*sem):
    return pltpu.CompilerParams(dimension_semantics=sem, vmem_limit_bytes=VMEM_LIMIT)


def _const_spec(shape):
    nd = len(shape)
    return pl.BlockSpec(shape, lambda *_: (0,) * nd)


def _ln_rows(x, g, b):
    mu = jnp.mean(x, -1, keepdims=True)
    xc = x - mu
    var = jnp.mean(xc * xc, -1, keepdims=True)
    return xc * lax.rsqrt(var + LN_EPS) * g + b


def _silu(x):
    return x / (1.0 + jnp.exp(-x))


def _ln_kernel(x_ref, g_ref, b_ref, o_ref):
    o_ref[...] = _ln_rows(x_ref[...], g_ref[...], b_ref[...])


def _layer_norm(x, g, b, tm=512):
    T, D = x.shape
    return pl.pallas_call(
        _ln_kernel,
        grid=(T // tm,),
        in_specs=[pl.BlockSpec((tm, D), lambda i: (i, 0)), _const_spec((1, D)), _const_spec((1, D))],
        out_specs=pl.BlockSpec((tm, D), lambda i: (i, 0)),
        out_shape=jax.ShapeDtypeStruct((T, D), F32),
        compiler_params=_cparams("parallel"),
        name="ln_in",
    )(x, g.reshape(1, D), b.reshape(1, D))


def _rope_chunks(y, tab):
    cos = tab[:, :LANES]
    sin = tab[:, LANES:]
    lane = lax.broadcasted_iota(I32, (1, LANES), 1)
    first = (lane % HEAD_DIM) < (HEAD_DIM // 2)
    out = []
    for c in range(ATT_WIDTH // LANES):
        tc = y[:, c * LANES:(c + 1) * LANES]
        rot = jnp.where(first, pltpu.roll(tc, LANES - HEAD_DIM // 2, 1), pltpu.roll(tc, HEAD_DIM // 2, 1))
        out.append(tc * cos + rot * sin)
    return out


def _proj_kernel(x_ref, wa_ref, wz_ref, wd_ref, t0_ref, t1_ref, t2_ref,
                 a0_ref, a1_ref, a2_ref, oz_ref, od_ref, xc_ref, xs_ref, *, tm, tn):
    xb = x_ref[0].astype(BF16)
    nlc = x_ref.shape[2] // LANES
    for c in range(nlc):
        xc_ref[c] = x_ref[0, :, c * LANES:(c + 1) * LANES]
    for j in range(oz_ref.shape[2] // tn):
        sl = slice(j * tn, (j + 1) * tn)
        oz_ref[0, :, sl] = jnp.dot(xb, wz_ref[:, sl], preferred_element_type=F32).astype(oz_ref.dtype)
    od_ref[0] = jnp.dot(xb, wd_ref[...], preferred_element_type=F32)
    scale = HEAD_DIM ** -0.5
    nchunk = ATT_WIDTH // LANES
    for g, (t_ref, a_ref) in enumerate(((t0_ref, a0_ref), (t1_ref, a1_ref), (t2_ref, a2_ref))):
        dil = DILATED_PATTERNS[g][1]
        n = tm // dil
        if dil == 1:
            xp = xb
        else:
            for r in range(dil):
                for c in range(nlc):
                    xs_ref[r * n:(r + 1) * n, c * LANES:(c + 1) * LANES] = (
                        xc_ref[c, pl.ds(r, n, stride=dil), :].astype(BF16))
            xp = xs_ref[...]
        tab = t_ref[...].reshape(tm, 2 * LANES)
        for j in range(3):
            c0 = (g * 3 + j) * ATT_WIDTH
            y = jnp.dot(xp, wa_ref[:, c0:c0 + ATT_WIDTH], preferred_element_type=F32)
            if j < 2:
                chunks = _rope_chunks(y, tab)
                if j == 0:
                    chunks = [ch * scale for ch in chunks]
            else:
                chunks = [y[:, c * LANES:(c + 1) * LANES] for c in range(nchunk)]
            for c, ch in enumerate(chunks):
                chb = ch.astype(BF16)
                col = j * ATT_WIDTH + c * LANES
                for r in range(dil):
                    a_ref[0, r, :, col:col + LANES] = chb[r * n:(r + 1) * n]


def _in_proj(h, w_att, w_zx, w_dt, tabs, tm=512, tn=512):
    b, s, D = h.shape
    dils = [d for _, d in DILATED_PATTERNS]
    qkv = 3 * ATT_WIDTH
    nz, nd = w_zx.shape[1], w_dt.shape[1]
    row = lambda n: pl.BlockSpec((1, tm, n), lambda bi, i: (bi, i, 0))
    sub = lambda d, n: pl.BlockSpec((1, d, tm // d, n), lambda bi, i: (bi, 0, i, 0))
    return pl.pallas_call(
        functools.partial(_proj_kernel, tm=tm, tn=tn),
        grid=(b, s // tm),
        in_specs=[row(D)] + [_const_spec(w.shape) for w in (w_att, w_zx, w_dt)]
        + [pl.BlockSpec((d, tm // d, 2 * LANES), lambda bi, i: (0, i, 0)) for d in dils],
        out_specs=[sub(d, qkv) for d in dils] + [row(nz), row(nd)],
        out_shape=[jax.ShapeDtypeStruct((b, d, s // d, qkv), BF16) for d in dils]
        + [jax.ShapeDtypeStruct((b, s, nz), BF16), jax.ShapeDtypeStruct((b, s, nd), F32)],
        scratch_shapes=[pltpu.VMEM((D // LANES, tm, LANES), F32), pltpu.VMEM((tm, D), BF16)],
        compiler_params=_cparams("parallel", "parallel"),
        name="in_proj",
    )(h, w_att, w_zx, w_dt, *tabs)


def _attn_kernel(q_ref, kc_ref, kp_ref, kn_ref, vc_ref, vp_ref, vn_ref, o_ref, lse_ref, kbuf, vbuf,
                 *, bq, seq_len, half):
    qi = pl.program_id(2)
    kbuf[0:ATT_HALO] = kp_ref[0, 0]
    kbuf[ATT_HALO:ATT_HALO + bq] = kc_ref[0, 0]
    kbuf[ATT_HALO + bq:] = kn_ref[0, 0]
    vbuf[0:ATT_HALO] = vp_ref[0, 0]
    vbuf[ATT_HALO:ATT_HALO + bq] = vc_ref[0, 0]
    vbuf[ATT_HALO + bq:] = vn_ref[0, 0]

    nk = ATT_SUB + 2 * ATT_HALO
    npair = ATT_HEADS // 2
    ri = lax.broadcasted_iota(I32, (ATT_SUB, nk), 0)
    ci = lax.broadcasted_iota(I32, (ATT_SUB, nk), 1)
    band = jnp.abs(ci - ATT_HALO - ri) <= half
    lane = lax.broadcasted_iota(I32, (1, LANES), 1)
    even = lane < HEAD_DIM
    for sb in range(bq // ATT_SUB):
        r0 = sb * ATT_SUB
        kpos = qi * bq + r0 - ATT_HALO + ci
        mask = band & (kpos >= 0) & (kpos < seq_len)
        ss = []
        for j in range(npair):
            cs = slice(j * LANES, (j + 1) * LANES)
            qp = q_ref[0, 0, r0:r0 + ATT_SUB, cs]
            zero = jnp.zeros_like(qp)
            lhs = jnp.concatenate([jnp.where(even, qp, zero), jnp.where(even, zero, qp)], axis=0)
            ss.append(lax.dot_general(lhs, kbuf[r0:r0 + nk, cs], (((1,), (1,)), ((), ())),
                                      preferred_element_type=F32))
        s = jnp.stack(ss).reshape(ATT_HEADS, ATT_SUB, nk)
        s = jnp.where(mask[None], s, NEG)
        m = jnp.max(s, -1, keepdims=True)
        p = jnp.exp(s - m)
        l = jnp.sum(p, -1, keepdims=True)
        pb = p.astype(BF16).reshape(npair, 2 * ATT_SUB, nk)
        l2 = l.reshape(npair, 2 * ATT_SUB, 1)
        lse = m + jnp.log(l)
        for j in range(npair):
            cs = slice(j * LANES, (j + 1) * LANES)
            o2 = jnp.dot(pb[j], vbuf[r0:r0 + nk, cs], preferred_element_type=F32) / l2[j]
            o_ref[0, 0, r0:r0 + ATT_SUB, cs] = jnp.where(even, o2[:ATT_SUB], o2[ATT_SUB:]).astype(o_ref.dtype)
        lse_tile = jnp.zeros((ATT_SUB, LANES), F32)
        for h in range(ATT_HEADS):
            lse_tile = jnp.where(lane == h, lse[h], lse_tile)
        lse_ref[0, 0, r0:r0 + ATT_SUB, :] = lse_tile


def _dilated_attention(att, gi, bq=512):
    win, dil = DILATED_PATTERNS[gi]
    half = win // (2 * dil)
    assert half <= ATT_HALO
    b, _, L, _ = att.shape
    bq = min(bq, L)
    nq = L // bq
    hb = bq // ATT_HALO
    nhb = L // ATT_HALO
    cur = lambda j: (lambda bi, r, qi: (bi, r, qi, j))
    prev = lambda j: (lambda bi, r, qi: (bi, r, jnp.maximum(qi * hb - 1, 0), j))
    nxt = lambda j: (lambda bi, r, qi: (bi, r, jnp.minimum((qi + 1) * hb, nhb - 1), j))
    blk = lambda n, f: pl.BlockSpec((1, 1, n, ATT_WIDTH), f)
    return pl.pallas_call(
        functools.partial(_attn_kernel, bq=bq, seq_len=L, half=half),
        grid=(b, dil, nq),
        in_specs=[blk(bq, cur(0)),
                  blk(bq, cur(1)), blk(ATT_HALO, prev(1)), blk(ATT_HALO, nxt(1)),
                  blk(bq, cur(2)), blk(ATT_HALO, prev(2)), blk(ATT_HALO, nxt(2))],
        out_specs=[pl.BlockSpec((1, 1, bq, ATT_WIDTH), lambda bi, r, qi: (bi, r, qi, 0)),
                   pl.BlockSpec((1, 1, bq, LANES), lambda bi, r, qi: (bi, r, qi, 0))],
        out_shape=[jax.ShapeDtypeStruct((b, dil, L, ATT_WIDTH), BF16),
                   jax.ShapeDtypeStruct((b, dil, L, LANES), F32)],
        scratch_shapes=[pltpu.VMEM((bq + 2 * ATT_HALO, ATT_WIDTH), BF16),
                        pltpu.VMEM((bq + 2 * ATT_HALO, ATT_WIDTH), BF16)],
        compiler_params=_cparams("parallel", "parallel", "parallel"),
        name=f"dil_attn_{gi}",
    )(att, att, att, att, att, att, att)


def _rope_tables(s):
    half = HEAD_DIM // 2
    inv = ROPE_THETA ** (-jnp.arange(half, dtype=F32) / half)
    ang = jnp.arange(s).astype(F32)[:, None] * inv[None, :]
    cos, sin = jnp.cos(ang), jnp.sin(ang)
    cos_h = jnp.concatenate([cos, cos], -1)
    sin_h = jnp.concatenate([-sin, sin], -1)
    rep = LANES // HEAD_DIM
    tab = jnp.concatenate([jnp.tile(cos_h, (1, rep)), jnp.tile(sin_h, (1, rep))], -1)
    return [tab.reshape(s // d, d, 2 * LANES).transpose(1, 0, 2) for _, d in DILATED_PATTERNS]


def _conv_kernel(c_ref, p_ref, n_ref, w_ref, b_ref, o_ref, buf, *, ts):
    i = pl.program_id(1)
    last = pl.num_programs(1) - 1
    pad = (CONV_K - 1) // 2
    buf[0:CONV_HALO] = jnp.where(i > 0, p_ref[0].astype(F32), 0.0)
    buf[CONV_HALO:CONV_HALO + ts] = c_ref[0].astype(F32)
    buf[CONV_HALO + ts:] = jnp.where(i < last, n_ref[0].astype(F32), 0.0)
    acc = jnp.zeros((ts, CONV_DIM), F32) + b_ref[...]
    for j in range(CONV_K):
        acc = acc + w_ref[j:j + 1, :] * buf[CONV_HALO - pad + j:CONV_HALO - pad + j + ts, :]
    o_ref[0] = _silu(acc).astype(o_ref.dtype)


def _conv_silu(zx_v, conv_w, conv_b, ts=512):
    b, s, _ = zx_v.shape
    ts = min(ts, s)
    r = ts // CONV_HALO
    nh = s // CONV_HALO
    w8 = jnp.zeros((8, CONV_DIM), F32).at[:CONV_K].set(conv_w)
    return pl.pallas_call(
        functools.partial(_conv_kernel, ts=ts),
        grid=(b, s // ts),
        in_specs=[pl.BlockSpec((1, ts, CONV_DIM), lambda bi, i: (bi, i, 0)),
                  pl.BlockSpec((1, CONV_HALO, CONV_DIM), lambda bi, i: (bi, jnp.maximum(i * r - 1, 0), 0)),
                  pl.BlockSpec((1, CONV_HALO, CONV_DIM), lambda bi, i: (bi, jnp.minimum((i + 1) * r, nh - 1), 0)),
                  _const_spec((8, CONV_DIM)), _const_spec((1, CONV_DIM))],
        out_specs=pl.BlockSpec((1, ts, CONV_DIM), lambda bi, i: (bi, i, 0)),
        out_shape=jax.ShapeDtypeStruct((b, s, CONV_DIM), BF16),
        scratch_shapes=[pltpu.VMEM((ts + 2 * CONV_HALO, CONV_DIM), F32)],
        compiler_params=_cparams("parallel", "parallel"),
        name="conv_silu",
    )(zx_v, zx_v, zx_v, w8, conv_b.reshape(1, CONV_DIM))


def _expand_heads(v, off):
    head = lax.broadcasted_iota(I32, (1, SSD_WIDTH), 1) // HEAD_DIM
    out = jnp.zeros((v.shape[0], SSD_WIDTH), F32)
    for h in range(SSD_HEADS):
        out = jnp.where(head == h, v[:, off + h:off + h + 1], out)
    return out


def _head_selector(off):
    r = lax.broadcasted_iota(I32, (LANES, SSD_WIDTH), 0)
    c = lax.broadcasted_iota(I32, (LANES, SSD_WIDTH), 1)
    return (r == c // HEAD_DIM + off).astype(BF16)


def _expand_heads_mxu(v, sel):
    hi = v.astype(BF16)
    lo = (v - hi.astype(F32)).astype(BF16)
    return (jnp.dot(hi, sel, preferred_element_type=F32) + jnp.dot(lo, sel, preferred_element_type=F32))


def _softplus(x):
    return jnp.maximum(x, 0.0) + jnp.log(1.0 + jnp.exp(-jnp.abs(x)))


def _ssd_chunk(xc, dtr, bias, a_row, state_ref, *, reverse, off):
    Q = SSD_CHUNK
    hi = lax.Precision.HIGHEST
    dt = _softplus(dtr + bias)
    a = dt * a_row
    ri = lax.broadcasted_iota(I32, (Q, Q), 0)
    ci = lax.broadcasted_iota(I32, (Q, Q), 1)
    keep = (ci >= ri) if reverse else (ci <= ri)
    tri = keep.astype(F32)
    cum = jnp.dot(tri, a, precision=hi, preferred_element_type=F32)
    cum_t = lax.dot_general(a.T, tri, (((1,), (1,)), ((), ())), precision=hi,
                            preferred_element_type=F32)
    edge = 0 if reverse else Q - 1
    tot = cum[edge:edge + 1, :]
    sel = _head_selector(off)
    dt512 = _expand_heads_mxu(dt, sel)
    dec512 = _expand_heads_mxu(jnp.exp(tot - cum), sel)
    ecum512 = _expand_heads_mxu(jnp.exp(cum), sel)
    etot512 = _expand_heads(jnp.exp(tot), off)
    xs = xc[:, :SSD_WIDTH].astype(F32)
    xdt = xs * dt512
    xdt_b = xdt.astype(BF16)
    xdd_b = (xdt * dec512).astype(BF16)
    gw = HEADS_PER_GROUP * HEAD_DIM
    lane = lax.broadcasted_iota(I32, (1, LANES), 1)
    even = lane < HEAD_DIM
    ys = []
    for g in range(SSD_GROUPS):
        bg = xc[:, SSD_WIDTH + g * D_STATE:SSD_WIDTH + (g + 1) * D_STATE]
        cg = xc[:, SSD_WIDTH + (SSD_GROUPS + g) * D_STATE:SSD_WIDTH + (SSD_GROUPS + g + 1) * D_STATE]
        cb = lax.dot_general(cg, bg, (((1,), (1,)), ((), ())), preferred_element_type=F32)
        sg = state_ref[g]
        yoff = jnp.dot(cg, sg.astype(BF16), preferred_element_type=F32)
        for pr in range(HEADS_PER_GROUP // 2):
            ms = []
            for hh in (2 * pr, 2 * pr + 1):
                ln = off + g * HEADS_PER_GROUP + hh
                seg = cum[:, ln:ln + 1] - cum_t[ln:ln + 1, :]
                lmat = jnp.where(keep, jnp.exp(jnp.where(keep, seg, 0.0)), 0.0)
                ms.append((cb * lmat).astype(BF16))
            c0 = g * gw + pr * LANES
            yd2 = jnp.dot(jnp.concatenate(ms, axis=0), xdt_b[:, c0:c0 + LANES], preferred_element_type=F32)
            yd = jnp.where(even, yd2[:Q], yd2[Q:])
            ys.append(yd + yoff[:, pr * LANES:(pr + 1) * LANES] * ecum512[:, c0:c0 + LANES])
        bg_t = bg.astype(F32).T.astype(BF16)
        state_ref[g] = sg * etot512[:, g * gw:(g + 1) * gw] + jnp.dot(
            bg_t, xdd_b[:, g * gw:(g + 1) * gw], preferred_element_type=F32)
    return ys


def _ssd_fwd_kernel(x_ref, dt_ref, bias_ref, a_ref, y_ref, state_ref, *, nch):
    @pl.when(pl.program_id(1) == 0)
    def _():
        state_ref[...] = jnp.zeros_like(state_ref)

    for c in range(nch):
        rows = slice(c * SSD_CHUNK, (c + 1) * SSD_CHUNK)
        ys = _ssd_chunk(x_ref[0, rows, :], dt_ref[0, rows, :], bias_ref[...], a_ref[...], state_ref,
                        reverse=False, off=0)
        for j, y in enumerate(ys):
            y_ref[0, rows, j * LANES:(j + 1) * LANES] = y


def _ssd_bwd_kernel(x_ref, dt_ref, bias_ref, a_ref, yf_ref, z_ref, dskip_ref, nw_ref, o_ref, state_ref,
                    ybuf, *, nch):
    @pl.when(pl.program_id(1) == 0)
    def _():
        state_ref[...] = jnp.zeros_like(state_ref)

    for c in reversed(range(nch)):
        rows = slice(c * SSD_CHUNK, (c + 1) * SSD_CHUNK)
        ys = _ssd_chunk(x_ref[0, rows, :], dt_ref[0, rows, :], bias_ref[...], a_ref[...], state_ref,
                        reverse=True, off=SSD_HEADS)
        for j, y in enumerate(ys):
            ybuf[rows, j * LANES:(j + 1) * LANES] = y
    xs = x_ref[0, :, :SSD_WIDTH].astype(F32)
    y = yf_ref[0] + ybuf[...] + xs * dskip_ref[...]
    gy = y * _silu(z_ref[0].astype(F32))
    gw = SSD_WIDTH // SSD_GROUPS
    for g in range(SSD_GROUPS):
        part = gy[:, g * gw:(g + 1) * gw]
        ms = jnp.mean(part * part, -1, keepdims=True)
        o_ref[0, :, g * gw:(g + 1) * gw] = (part * lax.rsqrt(ms + RMS_EPS) * nw_ref[:, g * gw:(g + 1) * gw]
                                             ).astype(o_ref.dtype)


def _ssd(xc, dt_v, zx_v, dt_bias, a_log, d_skip, norm_w, nch=4):
    b, s, _ = xc.shape
    nch = min(nch, s // SSD_CHUNK)
    R = nch * SSD_CHUNK
    n = s // R
    a_neg = -jnp.exp(a_log.astype(F32))
    pad = LANES - 2 * SSD_HEADS
    bias = jnp.pad(dt_bias.astype(F32).reshape(1, 2 * SSD_HEADS), ((0, 0), (0, pad)))
    a_f = jnp.pad(a_neg[0].reshape(1, SSD_HEADS), ((0, 0), (0, LANES - SSD_HEADS)))
    a_b = jnp.pad(a_neg[1].reshape(1, SSD_HEADS), ((0, 0), (SSD_HEADS, pad)))
    dskip = jnp.repeat(d_skip.astype(F32), HEAD_DIM).reshape(1, SSD_WIDTH)
    state = pltpu.VMEM((SSD_GROUPS, D_STATE, HEADS_PER_GROUP * HEAD_DIM), F32)
    fwd = lambda bi, i: (bi, i, 0)
    rev = lambda bi, i: (bi, n - 1 - i, 0)
    y_f = pl.pallas_call(
        functools.partial(_ssd_fwd_kernel, nch=nch),
        grid=(b, n),
        in_specs=[pl.BlockSpec((1, R, CONV_DIM), fwd), pl.BlockSpec((1, R, LANES), fwd),
                  _const_spec((1, LANES)), _const_spec((1, LANES))],
        out_specs=pl.BlockSpec((1, R, SSD_WIDTH), fwd),
        out_shape=jax.ShapeDtypeStruct((b, s, SSD_WIDTH), F32),
        scratch_shapes=[state],
        compiler_params=_cparams("parallel", "arbitrary"),
        name="ssd_fwd",
    )(xc, dt_v, bias, a_f)
    out = pl.pallas_call(
        functools.partial(_ssd_bwd_kernel, nch=nch),
        grid=(b, n),
        in_specs=[pl.BlockSpec((1, R, CONV_DIM), rev), pl.BlockSpec((1, R, LANES), rev),
                  _const_spec((1, LANES)), _const_spec((1, LANES)),
                  pl.BlockSpec((1, R, SSD_WIDTH), rev),
                  pl.BlockSpec((1, R, SSD_WIDTH), lambda bi, i: (bi, n - 1 - i, CONV_DIM // SSD_WIDTH)),
                  _const_spec((1, SSD_WIDTH)), _const_spec((1, SSD_WIDTH))],
        out_specs=pl.BlockSpec((1, R, SSD_WIDTH), rev),
        out_shape=jax.ShapeDtypeStruct((b, s, SSD_WIDTH), BF16),
        scratch_shapes=[state, pltpu.VMEM((R, SSD_WIDTH), F32)],
        compiler_params=_cparams("parallel", "arbitrary"),
        name="ssd_bwd",
    )(xc, dt_v, bias, a_b, y_f, zx_v, dskip, norm_w.astype(F32).reshape(1, SSD_WIDTH))
    return out


def _natural_order(src_ref, scr_ref):
    dil, n, w = src_ref.shape[1:]
    if dil == 1:
        return src_ref[0, 0].astype(F32)
    for r in range(dil):
        for c in range(w // LANES):
            scr_ref[c, pl.ds(r, n, stride=dil), :] = src_ref[0, r, :, c * LANES:(c + 1) * LANES].astype(F32)
    return jnp.concatenate([scr_ref[c] for c in range(w // LANES)], axis=1)


def _out_proj_kernel(ssd_ref, o0_ref, o1_ref, o2_ref, l0_ref, l1_ref, l2_ref, h_ref, w1_ref, w2_ref,
                     g_ref, b_ref, out_ref, so1, so2, sl1, sl2):
    lses = [_natural_order(r, s) for r, s in ((l0_ref, None), (l1_ref, sl1), (l2_ref, sl2))]
    mx = jnp.maximum(jnp.maximum(lses[0], lses[1]), lses[2])
    es = [jnp.exp(l - mx) for l in lses]
    den = es[0] + es[1] + es[2]
    sel = _head_selector(0)
    att = jnp.zeros((h_ref.shape[1], ATT_WIDTH), F32)
    for e, o_ref, scr in zip(es, (o0_ref, o1_ref, o2_ref), (None, so1, so2)):
        att = att + _expand_heads_mxu(e / den, sel) * _natural_order(o_ref, scr)
    y = jnp.dot(ssd_ref[0], w1_ref[...], preferred_element_type=F32)
    y = y + jnp.dot(att.astype(BF16), w2_ref[...], preferred_element_type=F32)
    out_ref[0] = _ln_rows(DN_ALPHA * h_ref[0] + y, g_ref[...], b_ref[...])


def _out_proj_ln(ssd, os_, lses, h, w_out, g, b, tm=512):
    bsz, s, D = h.shape
    dils = [d for _, d in DILATED_PATTERNS]
    row = lambda n: pl.BlockSpec((1, tm, n), lambda bi, i: (bi, i, 0))
    sub = lambda d, n: pl.BlockSpec((1, d, tm // d, n), lambda bi, i: (bi, 0, i, 0))
    w1 = w_out[:SSD_WIDTH]
    w2 = w_out[SSD_WIDTH:]
    return pl.pallas_call(
        _out_proj_kernel,
        grid=(bsz, s // tm),
        in_specs=[row(SSD_WIDTH)] + [sub(d, ATT_WIDTH) for d in dils] + [sub(d, LANES) for d in dils]
        + [row(D), _const_spec(w1.shape), _const_spec(w2.shape), _const_spec((1, D)), _const_spec((1, D))],
        out_specs=row(D),
        out_shape=jax.ShapeDtypeStruct((bsz, s, D), F32),
        scratch_shapes=[pltpu.VMEM((ATT_WIDTH // LANES, tm, LANES), F32)] * 2
        + [pltpu.VMEM((1, tm, LANES), F32)] * 2,
        compiler_params=_cparams("parallel", "parallel"),
        name="out_proj_ln",
    )(ssd, *os_, *lses, h, w1, w2, g.reshape(1, D), b.reshape(1, D))


def _router_kernel(h_ref, w_ref, id_ref, gate_ref):
    logits = lax.dot_general(w_ref[...], h_ref[...], (((1,), (1,)), ((), ())),
                             precision=lax.Precision.HIGHEST, preferred_element_type=F32)
    tm = logits.shape[1]
    row = lax.broadcasted_iota(I32, (8, tm), 0)
    lg = jnp.where(row < N_EXPERT_GROUPS, logits[0:8], NEG)
    gm = jnp.max(lg, 0, keepdims=True)
    gs = jnp.sum(jnp.exp(lg - gm), 0, keepdims=True)
    g_idx = jnp.min(jnp.where(lg == gm, row, 8), 0, keepdims=True)
    g_prob = 1.0 / gs
    el = jnp.zeros((8, tm), F32)
    for g in range(N_EXPERT_GROUPS):
        el = jnp.where(g_idx == g, logits[8 + 8 * g:16 + 8 * g], el)
    em = jnp.max(el, 0, keepdims=True)
    ee = jnp.exp(el - em)
    p = ee / jnp.sum(ee, 0, keepdims=True)
    p1 = jnp.max(p, 0, keepdims=True)
    i1 = jnp.min(jnp.where(p == p1, row, 8), 0, keepdims=True)
    pr = jnp.where(row == i1, -1.0, p)
    p2 = jnp.max(pr, 0, keepdims=True)
    i2 = jnp.min(jnp.where(pr == p2, row, 8), 0, keepdims=True)
    den = p1 + p2
    base = g_idx * EXPERTS_PER_GROUP
    id_ref[...] = jnp.where(row == 0, base + i1, jnp.where(row == 1, base + i2, 0))
    gate_ref[...] = jnp.where(row == 0, g_prob * p1 / den, jnp.where(row == 1, g_prob * p2 / den, 0.0))


def _router(h, w_router, tm=512):
    T, D = h.shape
    nr = w_router.shape[0]
    return pl.pallas_call(
        _router_kernel,
        grid=(T // tm,),
        in_specs=[pl.BlockSpec((tm, D), lambda i: (i, 0)), _const_spec((nr, D))],
        out_specs=[pl.BlockSpec((8, tm), lambda i: (0, i)), pl.BlockSpec((8, tm), lambda i: (0, i))],
        out_shape=[jax.ShapeDtypeStruct((8, T), I32), jax.ShapeDtypeStruct((8, T), F32)],
        compiler_params=_cparams("parallel"),
        name="router",
    )(h, w_router)


def _expert_kernel(be_ref, nu_ref, tok0_ref, tok1_ref, tok2_ref, dstp_ref, dstc_ref, h_hbm, wg_ref, wu_ref,
                   wd_ref, y_hbm, xbuf, ybuf, gsem, ssem, *, spare_rows):
    i = pl.program_id(0)
    nu = nu_ref[0]

    def gather_row(tok_ref, s, j):
        pltpu.make_async_copy(h_hbm.at[pl.ds(tok_ref[0, 0, j], 1)], xbuf.at[s, pl.ds(j, 1)], gsem.at[s]).start()

    def scatter_row(dst_ref, s, j):
        pltpu.make_async_copy(ybuf.at[s, pl.ds(j, 1)], y_hbm.at[pl.ds(dst_ref[0, 0, j], 1)], ssem.at[s]).start()

    def gather_wait(s):
        pltpu.make_async_copy(h_hbm.at[pl.ds(0, MOE_BLOCK)], xbuf.at[s], gsem.at[s]).wait()

    def scatter_wait(s):
        pltpu.make_async_copy(ybuf.at[s], y_hbm.at[pl.ds(0, MOE_BLOCK)], ssem.at[s]).wait()

    def rolled(fn, ref, s):
        def body(j, c):
            fn(ref, s, j)
            return c
        lax.fori_loop(0, MOE_BLOCK, body, 0, unroll=8)

    @pl.when(i == 0)
    def _():
        rolled(gather_row, tok0_ref, 0)
        rolled(gather_row, tok1_ref, 1)
        ybuf[...] = jnp.zeros(ybuf.shape, F32)
        fills = [pltpu.make_async_copy(ybuf.at[1], y_hbm.at[pl.ds(r, MOE_BLOCK)], ssem.at[1])
                 for r in spare_rows]
        for f in fills:
            f.start()
        for f in fills:
            f.wait()

    def step(cur):
        nxt, nx2 = (cur + 1) % NBUF, (cur + 2) % NBUF
        prv = nx2
        gather_wait(cur)

        @pl.when(i >= NBUF - 1)
        def _():
            scatter_wait(cur)

        for j in range(MOE_BLOCK):
            gather_row(tok2_ref, nx2, j)
            scatter_row(dstp_ref, prv, j)
        x = xbuf[cur].astype(BF16)
        hg = jnp.dot(x, wg_ref[0], preferred_element_type=F32)
        hu = jnp.dot(x, wu_ref[0], preferred_element_type=F32)
        hdn = (_silu(hg) * hu).astype(BF16)
        ybuf[cur] = jnp.dot(hdn, wd_ref[0], preferred_element_type=F32)

        @pl.when(i == nu - 1)
        def _():
            @pl.when(i >= 1)
            def _():
                scatter_wait(nxt)

            rolled(scatter_row, dstc_ref, cur)
            gather_wait(nxt)
            gather_wait(nx2)
            scatter_wait(cur)
            scatter_wait(prv)

    for k in range(NBUF):
        pl.when((i < nu) & (i % NBUF == k))(functools.partial(step, k))


def _expert_ffn(h, slot_token, slot_dst, block_expert, n_used, w_gate, w_up, w_down, n_rows, spare_rows):
    T, D = h.shape
    nb = slot_token.shape[0]
    smem = lambda f: pl.BlockSpec((1, 1, MOE_BLOCK), f, memory_space=pltpu.SMEM)
    ahead = lambda k: (lambda i, be, nu: (jnp.minimum(i + k, nb - 1), 0, 0))
    grid_spec = pltpu.PrefetchScalarGridSpec(
        num_scalar_prefetch=2,
        grid=(nb,),
        in_specs=[smem(ahead(0)), smem(ahead(1)), smem(ahead(2)),
                  smem(lambda i, be, nu: (i, 0, 0)),
                  smem(lambda i, be, nu: (i + 1, 0, 0)),
                  pl.BlockSpec(memory_space=pl.ANY),
                  pl.BlockSpec((1, D, EXPERT_FF), lambda i, be, nu: (be[i], 0, 0)),
                  pl.BlockSpec((1, D, EXPERT_FF), lambda i, be, nu: (be[i], 0, 0)),
                  pl.BlockSpec((1, EXPERT_FF, D), lambda i, be, nu: (be[i], 0, 0))],
        out_specs=pl.BlockSpec(memory_space=pl.ANY),
        scratch_shapes=[pltpu.VMEM((NBUF, MOE_BLOCK, D), F32), pltpu.VMEM((NBUF, MOE_BLOCK, D), F32),
                        pltpu.SemaphoreType.DMA((NBUF,)), pltpu.SemaphoreType.DMA((NBUF,))],
    )
    return pl.pallas_call(
        functools.partial(_expert_kernel, spare_rows=spare_rows),
        grid_spec=grid_spec,
        out_shape=jax.ShapeDtypeStruct((n_rows, D), F32),
        compiler_params=_cparams("arbitrary"),
        name="expert_ffn",
    )(block_expert, n_used, slot_token, slot_token, slot_token, slot_dst, slot_dst, h, w_gate, w_up, w_down)


def _combine_kernel(h_ref, y0_ref, y1_ref, gate_ref, g_ref, b_ref, o_ref):
    gt = gate_ref[...]
    ffn = y0_ref[0] * gt[:, 0:1] + y1_ref[0] * gt[:, 1:2]
    o_ref[...] = _ln_rows(DN_ALPHA * h_ref[...] + ffn, g_ref[...], b_ref[...])


def _combine_ln(h, y2, gates, g, b, tm=512):
    T, D = h.shape
    return pl.pallas_call(
        _combine_kernel,
        grid=(T // tm,),
        in_specs=[pl.BlockSpec((tm, D), lambda i: (i, 0)),
                  pl.BlockSpec((1, tm, D), lambda i: (0, i, 0)), pl.BlockSpec((1, tm, D), lambda i: (1, i, 0)),
                  pl.BlockSpec((tm, TOP_K), lambda i: (i, 0)), _const_spec((1, D)), _const_spec((1, D))],
        out_specs=pl.BlockSpec((tm, D), lambda i: (i, 0)),
        out_shape=jax.ShapeDtypeStruct((T, D), F32),
        compiler_params=_cparams("parallel"),
        name="combine_ln",
    )(h, y2, y2, gates, g.reshape(1, D), b.reshape(1, D))


def _mixer(h, lw, tabs):
    *atts, zx, dt_raw = _in_proj(h, lw["w_att"], lw["w_zx"], lw["w_dt"], tabs)
    os_, lses = [], []
    for gi in range(N_DIL):
        o, lse = _dilated_attention(atts[gi], gi)
        os_.append(o)
        lses.append(lse)
    xc = _conv_silu(zx, lw["conv_w"], lw["conv_b"])
    ssd = _ssd(xc, dt_raw, zx, lw["dt_bias"], lw["a_log"], lw["d_skip"], lw["ssd_norm_w"])
    return _out_proj_ln(ssd, os_, lses, h, lw["w_out"], lw["ln1_g"], lw["ln1_b"])


def _moe(h, lw):
    T, D = h.shape
    A = T * TOP_K
    ids, gates = _router(h, lw["w_router"])
    expert = ids[:TOP_K].T.reshape(A)
    order = jnp.argsort(expert, stable=True).astype(I32)
    onehot = expert[:, None] == jnp.arange(N_EXPERTS, dtype=I32)[None, :]
    counts = jnp.sum(onehot, 0, dtype=I32)
    padded = (counts + MOE_BLOCK - 1) // MOE_BLOCK * MOE_BLOCK
    pad_end = jnp.cumsum(padded)
    pad_start = pad_end - padded
    start = jnp.cumsum(counts) - counts
    n_blocks = -(-A // MOE_BLOCK) + N_EXPERTS
    blk0 = jnp.arange(n_blocks, dtype=I32) * MOE_BLOCK
    block_expert = jnp.minimum(jnp.sum(pad_end[None, :] <= blk0[:, None], -1, dtype=I32), N_EXPERTS - 1)
    n_used = (pad_end[-1] // MOE_BLOCK).astype(I32).reshape(1)
    sel = block_expert[:, None] == jnp.arange(N_EXPERTS, dtype=I32)[None, :]
    pick = lambda v: jnp.sum(jnp.where(sel, v[None, :], 0), -1, dtype=I32)[:, None]
    lane = jnp.arange(MOE_BLOCK, dtype=I32)[None, :]
    off = blk0[:, None] - pick(pad_start) + lane
    valid = off < pick(counts)
    asg = order[jnp.clip(pick(start) + off, 0, A - 1)]
    plane = T + MOE_BLOCK
    spare = (jnp.arange(n_blocks, dtype=I32)[:, None] % 2) * plane + T + lane
    slot_token = jnp.where(valid, asg // TOP_K, 0).reshape(n_blocks, 1, MOE_BLOCK)
    slot_dst = jnp.where(valid, (asg % TOP_K) * plane + asg // TOP_K, spare)
    slot_dst = jnp.concatenate([plane + T + lane, slot_dst], 0).reshape(n_blocks + 1, 1, MOE_BLOCK)
    y2 = _expert_ffn(h, slot_token, slot_dst, block_expert, n_used, lw["w_gate"], lw["w_up"], lw["w_down"],
                     TOP_K * plane, tuple(k * plane + T for k in range(TOP_K)))
    return _combine_ln(h, y2.reshape(TOP_K, plane, D), gates[:TOP_K].T, lw["ln2_g"], lw["ln2_b"])


def _trunk(x, ln_in_g, ln_in_b, layers):
    b, s, D = x.shape
    tabs = _rope_tables(s)
    h = _layer_norm(x.reshape(b * s, D), ln_in_g, ln_in_b)
    for lw in layers:
        h = _mixer(h.reshape(b, s, D), lw, tabs)
        h = _moe(h.reshape(b * s, D), lw)
    return h.reshape(b, s, D)


def _prep_layers(w_in, conv_w, conv_b, a_log, dt_bias, d_skip, ssd_norm_w, w_out, ln1_g, ln1_b,
                 router_group, router_expert, w_gate, w_up, w_down, ln2_g, ln2_b):
    layers = []
    z0 = ATT_PROJ
    x0 = ATT_PROJ + SSD_WIDTH
    d0 = x0 + CONV_DIM
    for i in range(w_in.shape[0]):
        w = w_in[i]
        w_dt = jnp.pad(w[:, d0:], ((0, 0), (0, LANES - 2 * SSD_HEADS))).astype(BF16)
        w_router = jnp.concatenate([
            router_group[i].T, jnp.zeros((8 - N_EXPERT_GROUPS, D_MODEL), F32), router_expert[i].T], 0)
        layers.append(dict(
            w_att=w[:, :z0].astype(BF16),
            w_zx=jnp.concatenate([w[:, x0:d0], w[:, z0:x0]], 1).astype(BF16),
            w_dt=w_dt,
            conv_w=conv_w[i], conv_b=conv_b[i], a_log=a_log[i], dt_bias=dt_bias[i], d_skip=d_skip[i],
            ssd_norm_w=ssd_norm_w[i], w_out=w_out[i].astype(BF16), ln1_g=ln1_g[i], ln1_b=ln1_b[i],
            w_router=w_router, w_gate=w_gate[i].astype(BF16), w_up=w_up[i].astype(BF16),
            w_down=w_down[i].astype(BF16), ln2_g=ln2_g[i], ln2_b=ln2_b[i]))
    return layers


def kernel(x_prompt, x_sample, ln_in_g, ln_in_b, w_in, conv_w, conv_b, a_log, dt_bias, d_skip, ssd_norm_w,
           w_out, ln1_g, ln1_b, router_group, router_expert, w_gate, w_up, w_down, ln2_g, ln2_b):
    layers = _prep_layers(w_in, conv_w, conv_b, a_log, dt_bias, d_skip, ssd_norm_w, w_out, ln1_g, ln1_b,
                          router_group, router_expert, w_gate, w_up, w_down, ln2_g, ln2_b)
    y_prompt = _trunk(x_prompt, ln_in_g, ln_in_b, layers)
    y_sample = _trunk(x_sample, ln_in_g, ln_in_b, layers)
    return (y_prompt, y_sample)
```

```python
import functools

import numpy as np
import jax
import jax.numpy as jnp
from jax import lax
from jax.experimental import pallas as pl
from jax.experimental.pallas import tpu as pltpu

F32 = jnp.float32
BF16 = jnp.bfloat16
I32 = jnp.int32

D_MODEL = 1024
DEPTH = 4
HEAD_DIM = 64
DILATED_PATTERNS = ((128, 1), (512, 4), (2048, 16))
N_DIL = 3
ATT_HEADS = 8
ATT_WIDTH = ATT_HEADS * HEAD_DIM
ATT_PROJ = N_DIL * 3 * ATT_WIDTH
ROPE_THETA = 10000.0
SSD_HEADS = 8
SSD_WIDTH = SSD_HEADS * HEAD_DIM
SSD_GROUPS = 2
HEADS_PER_GROUP = SSD_HEADS // SSD_GROUPS
D_STATE = 128
CONV_K = 5
CONV_DIM = SSD_WIDTH + 2 * SSD_GROUPS * D_STATE
N_EXPERT_GROUPS = 4
EXPERTS_PER_GROUP = 8
N_EXPERTS = N_EXPERT_GROUPS * EXPERTS_PER_GROUP
TOP_K = 2
EXPERT_FF = 512
MOE_BLOCK = 256
DN_ALPHA = (2 * DEPTH) ** 0.25
LN_EPS = 1e-5
RMS_EPS = 1e-5
NEG = -1e30

LANES = 128
SSD_CHUNK = 128
ATT_SUB = 128
ATT_HALO = 64
CONV_HALO = 16
NBUF = 3
VMEM_LIMIT = 48 * 1024 * 1024


def _cparams(*sem):
    return pltpu.CompilerParams(dimension_semantics=sem, vmem_limit_bytes=VMEM_LIMIT)


def _const_spec(shape):
    nd = len(shape)
    return pl.BlockSpec(shape, lambda *_: (0,) * nd)


def _ln_rows(x, g, b):
    mu = jnp.mean(x, -1, keepdims=True)
    xc = x - mu
    var = jnp.mean(xc * xc, -1, keepdims=True)
    return xc * lax.rsqrt(var + LN_EPS) * g + b


def _silu(x):
    return x / (1.0 + jnp.exp(-x))


def _ln_kernel(x_ref, g_ref, b_ref, o_ref):
    o_ref[...] = _ln_rows(x_ref[...], g_ref[...], b_ref[...])


def _layer_norm(x, g, b, tm=512):
    T, D = x.shape
    return pl.pallas_call(
        _ln_kernel,
        grid=(T // tm,),
        in_specs=[pl.BlockSpec((tm, D), lambda i: (i, 0)), _const_spec((1, D)), _const_spec((1, D))],
        out_specs=pl.BlockSpec((tm, D), lambda i: (i, 0)),
        out_shape=jax.ShapeDtypeStruct((T, D), F32),
        compiler_params=_cparams("parallel"),
        name="ln_in",
    )(x, g.reshape(1, D), b.reshape(1, D))


def _rope_chunks(y, tab):
    cos = tab[:, :LANES]
    sin = tab[:, LANES:]
    lane = lax.broadcasted_iota(I32, (1, LANES), 1)
    first = (lane % HEAD_DIM) < (HEAD_DIM // 2)
    out = []
    for c in range(ATT_WIDTH // LANES):
        tc = y[:, c * LANES:(c + 1) * LANES]
        rot = jnp.where(first, pltpu.roll(tc, LANES - HEAD_DIM // 2, 1), pltpu.roll(tc, HEAD_DIM // 2, 1))
        out.append(tc * cos + rot * sin)
    return out


def _proj_kernel(x_ref, wa_ref, wz_ref, wd_ref, t0_ref, t1_ref, t2_ref,
                 a0_ref, a1_ref, a2_ref, oz_ref, od_ref, xc_ref, xs_ref, *, tm, tn):
    xb = x_ref[0].astype(BF16)
    nlc = x_ref.shape[2] // LANES
    for c in range(nlc):
        xc_ref[c] = x_ref[0, :, c * LANES:(c + 1) * LANES]
    for j in range(oz_ref.shape[2] // tn):
        sl = slice(j * tn, (j + 1) * tn)
        oz_ref[0, :, sl] = jnp.dot(xb, wz_ref[:, sl], preferred_element_type=F32).astype(oz_ref.dtype)
    od_ref[0] = jnp.dot(xb, wd_ref[...], preferred_element_type=F32)
    scale = HEAD_DIM ** -0.5
    nchunk = ATT_WIDTH // LANES
    for g, (t_ref, a_ref) in enumerate(((t0_ref, a0_ref), (t1_ref, a1_ref), (t2_ref, a2_ref))):
        dil = DILATED_PATTERNS[g][1]
        n = tm // dil
        if dil == 1:
            xp = xb
        else:
            for r in range(dil):
                for c in range(nlc):
                    xs_ref[r * n:(r + 1) * n, c * LANES:(c + 1) * LANES] = (
                        xc_ref[c, pl.ds(r, n, stride=dil), :].astype(BF16))
            xp = xs_ref[...]
        tab = t_ref[...].reshape(tm, 2 * LANES)
        for j in range(3):
            c0 = (g * 3 + j) * ATT_WIDTH
            y = jnp.dot(xp, wa_ref[:, c0:c0 + ATT_WIDTH], preferred_element_type=F32)
            if j < 2:
                chunks = _rope_chunks(y, tab)
                if j == 0:
                    chunks = [ch * scale for ch in chunks]
            else:
                chunks = [y[:, c * LANES:(c + 1) * LANES] for c in range(nchunk)]
            for c, ch in enumerate(chunks):
                chb = ch.astype(BF16)
                col = j * ATT_WIDTH + c * LANES
                for r in range(dil):
                    a_ref[0, r, :, col:col + LANES] = chb[r * n:(r + 1) * n]


def _in_proj(h, w_att, w_zx, w_dt, tabs, tm=512, tn=512):
    b, s, D = h.shape
    dils = [d for _, d in DILATED_PATTERNS]
    qkv = 3 * ATT_WIDTH
    nz, nd = w_zx.shape[1], w_dt.shape[1]
    row = lambda n: pl.BlockSpec((1, tm, n), lambda bi, i: (bi, i, 0))
    sub = lambda d, n: pl.BlockSpec((1, d, tm // d, n), lambda bi, i: (bi, 0, i, 0))
    return pl.pallas_call(
        functools.partial(_proj_kernel, tm=tm, tn=tn),
        grid=(b, s // tm),
        in_specs=[row(D)] + [_const_spec(w.shape) for w in (w_att, w_zx, w_dt)]
        + [pl.BlockSpec((d, tm // d, 2 * LANES), lambda bi, i: (0, i, 0)) for d in dils],
        out_specs=[sub(d, qkv) for d in dils] + [row(nz), row(nd)],
        out_shape=[jax.ShapeDtypeStruct((b, d, s // d, qkv), BF16) for d in dils]
        + [jax.ShapeDtypeStruct((b, s, nz), BF16), jax.ShapeDtypeStruct((b, s, nd), F32)],
        scratch_shapes=[pltpu.VMEM((D // LANES, tm, LANES), F32), pltpu.VMEM((tm, D), BF16)],
        compiler_params=_cparams("parallel", "parallel"),
        name="in_proj",
    )(h, w_att, w_zx, w_dt, *tabs)


def _attn_kernel(q_ref, kc_ref, kp_ref, kn_ref, vc_ref, vp_ref, vn_ref, o_ref, lse_ref, kbuf, vbuf,
                 *, bq, seq_len, half):
    qi = pl.program_id(2)
    kbuf[0:ATT_HALO] = kp_ref[0, 0]
    kbuf[ATT_HALO:ATT_HALO + bq] = kc_ref[0, 0]
    kbuf[ATT_HALO + bq:] = kn_ref[0, 0]
    vbuf[0:ATT_HALO] = vp_ref[0, 0]
    vbuf[ATT_HALO:ATT_HALO + bq] = vc_ref[0, 0]
    vbuf[ATT_HALO + bq:] = vn_ref[0, 0]

    nk = ATT_SUB + 2 * ATT_HALO
    npair = ATT_HEADS // 2
    ri = lax.broadcasted_iota(I32, (ATT_SUB, nk), 0)
    ci = lax.broadcasted_iota(I32, (ATT_SUB, nk), 1)
    band = jnp.abs(ci - ATT_HALO - ri) <= half
    lane = lax.broadcasted_iota(I32, (1, LANES), 1)
    even = lane < HEAD_DIM
    for sb in range(bq // ATT_SUB):
        r0 = sb * ATT_SUB
        kpos = qi * bq + r0 - ATT_HALO + ci
        mask = band & (kpos >= 0) & (kpos < seq_len)
        ss = []
        for j in range(npair):
            cs = slice(j * LANES, (j + 1) * LANES)
            qp = q_ref[0, 0, r0:r0 + ATT_SUB, cs]
            zero = jnp.zeros_like(qp)
            lhs = jnp.concatenate([jnp.where(even, qp, zero), jnp.where(even, zero, qp)], axis=0)
            ss.append(lax.dot_general(lhs, kbuf[r0:r0 + nk, cs], (((1,), (1,)), ((), ())),
                                      preferred_element_type=F32))
        s = jnp.stack(ss).reshape(ATT_HEADS, ATT_SUB, nk)
        s = jnp.where(mask[None], s, NEG)
        m = jnp.max(s, -1, keepdims=True)
        p = jnp.exp(s - m)
        l = jnp.sum(p, -1, keepdims=True)
        pb = p.astype(BF16).reshape(npair, 2 * ATT_SUB, nk)
        l2 = l.reshape(npair, 2 * ATT_SUB, 1)
        lse = m + jnp.log(l)
        for j in range(npair):
            cs = slice(j * LANES, (j + 1) * LANES)
            o2 = jnp.dot(pb[j], vbuf[r0:r0 + nk, cs], preferred_element_type=F32) / l2[j]
            o_ref[0, 0, r0:r0 + ATT_SUB, cs] = jnp.where(even, o2[:ATT_SUB], o2[ATT_SUB:]).astype(o_ref.dtype)
        lse_tile = jnp.zeros((ATT_SUB, LANES), F32)
        for h in range(ATT_HEADS):
            lse_tile = jnp.where(lane == h, lse[h], lse_tile)
        lse_ref[0, 0, r0:r0 + ATT_SUB, :] = lse_tile


def _dilated_attention(att, gi, bq=512):
    win, dil = DILATED_PATTERNS[gi]
    half = win // (2 * dil)
    assert half <= ATT_HALO
    b, _, L, _ = att.shape
    bq = min(bq, L)
    nq = L // bq
    hb = bq // ATT_HALO
    nhb = L // ATT_HALO
    cur = lambda j: (lambda bi, r, qi: (bi, r, qi, j))
    prev = lambda j: (lambda bi, r, qi: (bi, r, jnp.maximum(qi * hb - 1, 0), j))
    nxt = lambda j: (lambda bi, r, qi: (bi, r, jnp.minimum((qi + 1) * hb, nhb - 1), j))
    blk = lambda n, f: pl.BlockSpec((1, 1, n, ATT_WIDTH), f)
    return pl.pallas_call(
        functools.partial(_attn_kernel, bq=bq, seq_len=L, half=half),
        grid=(b, dil, nq),
        in_specs=[blk(bq, cur(0)),
                  blk(bq, cur(1)), blk(ATT_HALO, prev(1)), blk(ATT_HALO, nxt(1)),
                  blk(bq, cur(2)), blk(ATT_HALO, prev(2)), blk(ATT_HALO, nxt(2))],
        out_specs=[pl.BlockSpec((1, 1, bq, ATT_WIDTH), lambda bi, r, qi: (bi, r, qi, 0)),
                   pl.BlockSpec((1, 1, bq, LANES), lambda bi, r, qi: (bi, r, qi, 0))],
        out_shape=[jax.ShapeDtypeStruct((b, dil, L, ATT_WIDTH), BF16),
                   jax.ShapeDtypeStruct((b, dil, L, LANES), F32)],
        scratch_shapes=[pltpu.VMEM((bq + 2 * ATT_HALO, ATT_WIDTH), BF16),
                        pltpu.VMEM((bq + 2 * ATT_HALO, ATT_WIDTH), BF16)],
        compiler_params=_cparams("parallel", "parallel", "parallel"),
        name=f"dil_attn_{gi}",
    )(att, att, att, att, att, att, att)


def _rope_tables(s):
    half = HEAD_DIM // 2
    inv = ROPE_THETA ** (-jnp.arange(half, dtype=F32) / half)
    ang = jnp.arange(s).astype(F32)[:, None] * inv[None, :]
    cos, sin = jnp.cos(ang), jnp.sin(ang)
    cos_h = jnp.concatenate([cos, cos], -1)
    sin_h = jnp.concatenate([-sin, sin], -1)
    rep = LANES // HEAD_DIM
    tab = jnp.concatenate([jnp.tile(cos_h, (1, rep)), jnp.tile(sin_h, (1, rep))], -1)
    return [tab.reshape(s // d, d, 2 * LANES).transpose(1, 0, 2) for _, d in DILATED_PATTERNS]


def _conv_kernel(c_ref, p_ref, n_ref, w_ref, b_ref, o_ref, buf, *, ts):
    i = pl.program_id(1)
    last = pl.num_programs(1) - 1
    pad = (CONV_K - 1) // 2
    buf[0:CONV_HALO] = jnp.where(i > 0, p_ref[0].astype(F32), 0.0)
    buf[CONV_HALO:CONV_HALO + ts] = c_ref[0].astype(F32)
    buf[CONV_HALO + ts:] = jnp.where(i < last, n_ref[0].astype(F32), 0.0)
    acc = jnp.zeros((ts, CONV_DIM), F32) + b_ref[...]
    for j in range(CONV_K):
        acc = acc + w_ref[j:j + 1, :] * buf[CONV_HALO - pad + j:CONV_HALO - pad + j + ts, :]
    o_ref[0] = _silu(acc).astype(o_ref.dtype)


def _conv_silu(zx_v, conv_w, conv_b, ts=512):
    b, s, _ = zx_v.shape
    ts = min(ts, s)
    r = ts // CONV_HALO
    nh = s // CONV_HALO
    w8 = jnp.zeros((8, CONV_DIM), F32).at[:CONV_K].set(conv_w)
    return pl.pallas_call(
        functools.partial(_conv_kernel, ts=ts),
        grid=(b, s // ts),
        in_specs=[pl.BlockSpec((1, ts, CONV_DIM), lambda bi, i: (bi, i, 0)),
                  pl.BlockSpec((1, CONV_HALO, CONV_DIM), lambda bi, i: (bi, jnp.maximum(i * r - 1, 0), 0)),
                  pl.BlockSpec((1, CONV_HALO, CONV_DIM), lambda bi, i: (bi, jnp.minimum((i + 1) * r, nh - 1), 0)),
                  _const_spec((8, CONV_DIM)), _const_spec((1, CONV_DIM))],
        out_specs=pl.BlockSpec((1, ts, CONV_DIM), lambda bi, i: (bi, i, 0)),
        out_shape=jax.ShapeDtypeStruct((b, s, CONV_DIM), BF16),
        scratch_shapes=[pltpu.VMEM((ts + 2 * CONV_HALO, CONV_DIM), F32)],
        compiler_params=_cparams("parallel", "parallel"),
        name="conv_silu",
    )(zx_v, zx_v, zx_v, w8, conv_b.reshape(1, CONV_DIM))


def _expand_heads(v, off):
    head = lax.broadcasted_iota(I32, (1, SSD_WIDTH), 1) // HEAD_DIM
    out = jnp.zeros((v.shape[0], SSD_WIDTH), F32)
    for h in range(SSD_HEADS):
        out = jnp.where(head == h, v[:, off + h:off + h + 1], out)
    return out


def _head_selector(off):
    r = lax.broadcasted_iota(I32, (LANES, SSD_WIDTH), 0)
    c = lax.broadcasted_iota(I32, (LANES, SSD_WIDTH), 1)
    return (r == c // HEAD_DIM + off).astype(BF16)


def _expand_heads_mxu(v, sel):
    hi = v.astype(BF16)
    lo = (v - hi.astype(F32)).astype(BF16)
    return (jnp.dot(hi, sel, preferred_element_type=F32) + jnp.dot(lo, sel, preferred_element_type=F32))


def _softplus(x):
    return jnp.maximum(x, 0.0) + jnp.log(1.0 + jnp.exp(-jnp.abs(x)))


def _ssd_chunk(xc, dtr, bias, a_row, state_ref, *, reverse, off):
    Q = SSD_CHUNK
    dt = _softplus(dtr + bias)
    a = dt * a_row
    ri = lax.broadcasted_iota(I32, (Q, Q), 0)
    ci = lax.broadcasted_iota(I32, (Q, Q), 1)
    keep = (ci >= ri) if reverse else (ci <= ri)
    tri = keep.astype(BF16)
    a0 = a.astype(BF16)
    a1 = (a - a0.astype(F32)).astype(BF16)
    a2 = (a - a0.astype(F32) - a1.astype(F32)).astype(BF16)
    cum = (jnp.dot(tri, a0, preferred_element_type=F32) + jnp.dot(tri, a1, preferred_element_type=F32)
           + jnp.dot(tri, a2, preferred_element_type=F32))
    cum_t = cum.T
    edge = 0 if reverse else Q - 1
    tot = cum[edge:edge + 1, :]
    sel = _head_selector(off)
    dt512 = _expand_heads_mxu(dt, sel)
    dec512 = _expand_heads_mxu(jnp.exp(tot - cum), sel)
    ecum512 = _expand_heads_mxu(jnp.exp(cum), sel)
    etot512 = _expand_heads(jnp.exp(tot), off)
    xs = xc[:, :SSD_WIDTH].astype(F32)
    xdt = xs * dt512
    xdt_b = xdt.astype(BF16)
    xdd_b = (xdt * dec512).astype(BF16)
    gw = HEADS_PER_GROUP * HEAD_DIM
    lane = lax.broadcasted_iota(I32, (1, LANES), 1)
    even = lane < HEAD_DIM
    ys = []
    for g in range(SSD_GROUPS):
        bg = xc[:, SSD_WIDTH + g * D_STATE:SSD_WIDTH + (g + 1) * D_STATE]
        cg = xc[:, SSD_WIDTH + (SSD_GROUPS + g) * D_STATE:SSD_WIDTH + (SSD_GROUPS + g + 1) * D_STATE]
        cb = lax.dot_general(cg, bg, (((1,), (1,)), ((), ())), preferred_element_type=F32)
        sg = state_ref[g]
        yoff = jnp.dot(cg, sg.astype(BF16), preferred_element_type=F32)
        for pr in range(HEADS_PER_GROUP // 2):
            ms = []
            for hh in (2 * pr, 2 * pr + 1):
                ln = off + g * HEADS_PER_GROUP + hh
                seg = cum[:, ln:ln + 1] - cum_t[ln:ln + 1, :]
                lmat = jnp.where(keep, jnp.exp(jnp.where(keep, seg, 0.0)), 0.0)
                ms.append((cb * lmat).astype(BF16))
            c0 = g * gw + pr * LANES
            yd2 = jnp.dot(jnp.concatenate(ms, axis=0), xdt_b[:, c0:c0 + LANES], preferred_element_type=F32)
            yd = jnp.where(even, yd2[:Q], yd2[Q:])
            ys.append(yd + yoff[:, pr * LANES:(pr + 1) * LANES] * ecum512[:, c0:c0 + LANES])
        bg_t = bg.astype(F32).T.astype(BF16)
        state_ref[g] = sg * etot512[:, g * gw:(g + 1) * gw] + jnp.dot(
            bg_t, xdd_b[:, g * gw:(g + 1) * gw], preferred_element_type=F32)
    return ys


def _ssd_fwd_kernel(x_ref, dt_ref, bias_ref, a_ref, y_ref, state_ref, *, nch):
    @pl.when(pl.program_id(1) == 0)
    def _():
        state_ref[...] = jnp.zeros_like(state_ref)

    for c in range(nch):
        rows = slice(c * SSD_CHUNK, (c + 1) * SSD_CHUNK)
        ys = _ssd_chunk(x_ref[0, rows, :], dt_ref[0, rows, :], bias_ref[...], a_ref[...], state_ref,
                        reverse=False, off=0)
        for j, y in enumerate(ys):
            y_ref[0, rows, j * LANES:(j + 1) * LANES] = y


def _ssd_bwd_kernel(x_ref, dt_ref, bias_ref, a_ref, yf_ref, z_ref, dskip_ref, nw_ref, o_ref, state_ref,
                    ybuf, *, nch):
    @pl.when(pl.program_id(1) == 0)
    def _():
        state_ref[...] = jnp.zeros_like(state_ref)

    for c in reversed(range(nch)):
        rows = slice(c * SSD_CHUNK, (c + 1) * SSD_CHUNK)
        ys = _ssd_chunk(x_ref[0, rows, :], dt_ref[0, rows, :], bias_ref[...], a_ref[...], state_ref,
                        reverse=True, off=SSD_HEADS)
        for j, y in enumerate(ys):
            ybuf[rows, j * LANES:(j + 1) * LANES] = y
    xs = x_ref[0, :, :SSD_WIDTH].astype(F32)
    y = yf_ref[0] + ybuf[...] + xs * dskip_ref[...]
    gy = y * _silu(z_ref[0].astype(F32))
    gw = SSD_WIDTH // SSD_GROUPS
    for g in range(SSD_GROUPS):
        part = gy[:, g * gw:(g + 1) * gw]
        ms = jnp.mean(part * part, -1, keepdims=True)
        o_ref[0, :, g * gw:(g + 1) * gw] = (part * lax.rsqrt(ms + RMS_EPS) * nw_ref[:, g * gw:(g + 1) * gw]
                                             ).astype(o_ref.dtype)


def _ssd(xc, dt_v, zx_v, dt_bias, a_log, d_skip, norm_w, nch=4):
    b, s, _ = xc.shape
    nch = min(nch, s // SSD_CHUNK)
    R = nch * SSD_CHUNK
    n = s // R
    a_neg = -jnp.exp(a_log.astype(F32))
    pad = LANES - 2 * SSD_HEADS
    bias = jnp.pad(dt_bias.astype(F32).reshape(1, 2 * SSD_HEADS), ((0, 0), (0, pad)))
    a_f = jnp.pad(a_neg[0].reshape(1, SSD_HEADS), ((0, 0), (0, LANES - SSD_HEADS)))
    a_b = jnp.pad(a_neg[1].reshape(1, SSD_HEADS), ((0, 0), (SSD_HEADS, pad)))
    dskip = jnp.repeat(d_skip.astype(F32), HEAD_DIM).reshape(1, SSD_WIDTH)
    state = pltpu.VMEM((SSD_GROUPS, D_STATE, HEADS_PER_GROUP * HEAD_DIM), F32)
    fwd = lambda bi, i: (bi, i, 0)
    rev = lambda bi, i: (bi, n - 1 - i, 0)
    y_f = pl.pallas_call(
        functools.partial(_ssd_fwd_kernel, nch=nch),
        grid=(b, n),
        in_specs=[pl.BlockSpec((1, R, CONV_DIM), fwd), pl.BlockSpec((1, R, LANES), fwd),
                  _const_spec((1, LANES)), _const_spec((1, LANES))],
        out_specs=pl.BlockSpec((1, R, SSD_WIDTH), fwd),
        out_shape=jax.ShapeDtypeStruct((b, s, SSD_WIDTH), F32),
        scratch_shapes=[state],
        compiler_params=_cparams("parallel", "arbitrary"),
        name="ssd_fwd",
    )(xc, dt_v, bias, a_f)
    out = pl.pallas_call(
        functools.partial(_ssd_bwd_kernel, nch=nch),
        grid=(b, n),
        in_specs=[pl.BlockSpec((1, R, CONV_DIM), rev), pl.BlockSpec((1, R, LANES), rev),
                  _const_spec((1, LANES)), _const_spec((1, LANES)),
                  pl.BlockSpec((1, R, SSD_WIDTH), rev),
                  pl.BlockSpec((1, R, SSD_WIDTH), lambda bi, i: (bi, n - 1 - i, CONV_DIM // SSD_WIDTH)),
                  _const_spec((1, SSD_WIDTH)), _const_spec((1, SSD_WIDTH))],
        out_specs=pl.BlockSpec((1, R, SSD_WIDTH), rev),
        out_shape=jax.ShapeDtypeStruct((b, s, SSD_WIDTH), BF16),
        scratch_shapes=[state, pltpu.VMEM((R, SSD_WIDTH), F32)],
        compiler_params=_cparams("parallel", "arbitrary"),
        name="ssd_bwd",
    )(xc, dt_v, bias, a_b, y_f, zx_v, dskip, norm_w.astype(F32).reshape(1, SSD_WIDTH))
    return out


def _natural_order(src_ref, scr_ref):
    dil, n, w = src_ref.shape[1:]
    if dil == 1:
        return src_ref[0, 0].astype(F32)
    for r in range(dil):
        for c in range(w // LANES):
            scr_ref[c, pl.ds(r, n, stride=dil), :] = src_ref[0, r, :, c * LANES:(c + 1) * LANES].astype(F32)
    return jnp.concatenate([scr_ref[c] for c in range(w // LANES)], axis=1)


def _out_proj_kernel(ssd_ref, o0_ref, o1_ref, o2_ref, l0_ref, l1_ref, l2_ref, h_ref, w1_ref, w2_ref,
                     g_ref, b_ref, out_ref, so1, so2, sl1, sl2):
    lses = [_natural_order(r, s) for r, s in ((l0_ref, None), (l1_ref, sl1), (l2_ref, sl2))]
    mx = jnp.maximum(jnp.maximum(lses[0], lses[1]), lses[2])
    es = [jnp.exp(l - mx) for l in lses]
    den = es[0] + es[1] + es[2]
    sel = _head_selector(0)
    att = jnp.zeros((h_ref.shape[1], ATT_WIDTH), F32)
    for e, o_ref, scr in zip(es, (o0_ref, o1_ref, o2_ref), (None, so1, so2)):
        att = att + _expand_heads_mxu(e / den, sel) * _natural_order(o_ref, scr)
    y = jnp.dot(ssd_ref[0], w1_ref[...], preferred_element_type=F32)
    y = y + jnp.dot(att.astype(BF16), w2_ref[...], preferred_element_type=F32)
    out_ref[0] = _ln_rows(DN_ALPHA * h_ref[0] + y, g_ref[...], b_ref[...])


def _out_proj_ln(ssd, os_, lses, h, w_out, g, b, tm=512):
    bsz, s, D = h.shape
    dils = [d for _, d in DILATED_PATTERNS]
    row = lambda n: pl.BlockSpec((1, tm, n), lambda bi, i: (bi, i, 0))
    sub = lambda d, n: pl.BlockSpec((1, d, tm // d, n), lambda bi, i: (bi, 0, i, 0))
    w1 = w_out[:SSD_WIDTH]
    w2 = w_out[SSD_WIDTH:]
    return pl.pallas_call(
        _out_proj_kernel,
        grid=(bsz, s // tm),
        in_specs=[row(SSD_WIDTH)] + [sub(d, ATT_WIDTH) for d in dils] + [sub(d, LANES) for d in dils]
        + [row(D), _const_spec(w1.shape), _const_spec(w2.shape), _const_spec((1, D)), _const_spec((1, D))],
        out_specs=row(D),
        out_shape=jax.ShapeDtypeStruct((bsz, s, D), F32),
        scratch_shapes=[pltpu.VMEM((ATT_WIDTH // LANES, tm, LANES), F32)] * 2
        + [pltpu.VMEM((1, tm, LANES), F32)] * 2,
        compiler_params=_cparams("parallel", "parallel"),
        name="out_proj_ln",
    )(ssd, *os_, *lses, h, w1, w2, g.reshape(1, D), b.reshape(1, D))


def _router_kernel(h_ref, w_ref, id_ref, gate_ref):
    nt = (((1,), (1,)), ((), ()))
    h = h_ref[...]
    h0 = h.astype(BF16)
    h1 = (h - h0.astype(F32)).astype(BF16)
    nr = w_ref.shape[0] // 2
    r0 = lax.dot_general(w_ref[...], h0, nt, preferred_element_type=F32)
    r1 = lax.dot_general(w_ref[0:nr, :], h1, nt, preferred_element_type=F32)
    logits = r0[0:nr] + r0[nr:] + r1
    tm = logits.shape[1]
    row = lax.broadcasted_iota(I32, (8, tm), 0)
    lg = jnp.where(row < N_EXPERT_GROUPS, logits[0:8], NEG)
    gm = jnp.max(lg, 0, keepdims=True)
    gs = jnp.sum(jnp.exp(lg - gm), 0, keepdims=True)
    g_idx = jnp.min(jnp.where(lg == gm, row, 8), 0, keepdims=True)
    g_prob = 1.0 / gs
    el = jnp.zeros((8, tm), F32)
    for g in range(N_EXPERT_GROUPS):
        el = jnp.where(g_idx == g, logits[8 + 8 * g:16 + 8 * g], el)
    em = jnp.max(el, 0, keepdims=True)
    ee = jnp.exp(el - em)
    p = ee / jnp.sum(ee, 0, keepdims=True)
    p1 = jnp.max(p, 0, keepdims=True)
    i1 = jnp.min(jnp.where(p == p1, row, 8), 0, keepdims=True)
    pr = jnp.where(row == i1, -1.0, p)
    p2 = jnp.max(pr, 0, keepdims=True)
    i2 = jnp.min(jnp.where(pr == p2, row, 8), 0, keepdims=True)
    den = p1 + p2
    base = g_idx * EXPERTS_PER_GROUP
    id_ref[...] = jnp.where(row == 0, base + i1, jnp.where(row == 1, base + i2, 0))
    gate_ref[...] = jnp.where(row == 0, g_prob * p1 / den, jnp.where(row == 1, g_prob * p2 / den, 0.0))


def _router(h, w_router, tm=512):
    T, D = h.shape
    w0 = w_router.astype(BF16)
    w_router = jnp.concatenate([w0, (w_router - w0.astype(F32)).astype(BF16)], 0)
    nr = w_router.shape[0]
    return pl.pallas_call(
        _router_kernel,
        grid=(T // tm,),
        in_specs=[pl.BlockSpec((tm, D), lambda i: (i, 0)), _const_spec((nr, D))],
        out_specs=[pl.BlockSpec((8, tm), lambda i: (0, i)), pl.BlockSpec((8, tm), lambda i: (0, i))],
        out_shape=[jax.ShapeDtypeStruct((8, T), I32), jax.ShapeDtypeStruct((8, T), F32)],
        compiler_params=_cparams("parallel"),
        name="router",
    )(h, w_router)


def _expert_kernel(be_ref, nu_ref, tok0_ref, tok1_ref, tok2_ref, h_hbm, wg_ref, wu_ref, wd_ref, o_ref,
                   xbuf, gsem):
    i = pl.program_id(0)
    nu = nu_ref[0]

    def gather_row(tok_ref, s, j):
        pltpu.make_async_copy(h_hbm.at[pl.ds(tok_ref[0, 0, j], 1)], xbuf.at[s, pl.ds(j, 1)], gsem.at[s]).start()

    def gather_wait(s):
        pltpu.make_async_copy(h_hbm.at[pl.ds(0, MOE_BLOCK)], xbuf.at[s], gsem.at[s]).wait()

    def rolled(ref, s):
        def body(j, c):
            gather_row(ref, s, j)
            return c
        lax.fori_loop(0, MOE_BLOCK, body, 0, unroll=8)

    @pl.when(i == 0)
    def _():
        rolled(tok0_ref, 0)
        rolled(tok1_ref, 1)

    @pl.when(i >= nu)
    def _():
        o_ref[...] = jnp.zeros(o_ref.shape, F32)

    def step(cur):
        nxt, nx2 = (cur + 1) % NBUF, (cur + 2) % NBUF
        gather_wait(cur)

        def issue(part, nparts=4):
            n = MOE_BLOCK // nparts
            for j in range(part * n, (part + 1) * n):
                gather_row(tok2_ref, nx2, j)

        x = xbuf[cur].astype(BF16)
        issue(0)
        hg = jnp.dot(x, wg_ref[0], preferred_element_type=F32)
        issue(1)
        hu = jnp.dot(x, wu_ref[0], preferred_element_type=F32)
        issue(2)
        hdn = (_silu(hg) * hu).astype(BF16)
        issue(3)
        o_ref[...] = jnp.dot(hdn, wd_ref[0], preferred_element_type=F32)

        @pl.when(i == nu - 1)
        def _():
            gather_wait(nxt)
            gather_wait(nx2)

    for k in range(NBUF):
        pl.when((i < nu) & (i % NBUF == k))(functools.partial(step, k))


def _expert_ffn(h, slot_token, block_expert, n_used, w_gate, w_up, w_down):
    T, D = h.shape
    nb = slot_token.shape[0]
    smem = lambda f: pl.BlockSpec((1, 1, MOE_BLOCK), f, memory_space=pltpu.SMEM)
    ahead = lambda k: (lambda i, be, nu: (jnp.minimum(i + k, nb - 1), 0, 0))
    grid_spec = pltpu.PrefetchScalarGridSpec(
        num_scalar_prefetch=2,
        grid=(nb,),
        in_specs=[smem(ahead(0)), smem(ahead(1)), smem(ahead(2)),
                  pl.BlockSpec(memory_space=pl.ANY),
                  pl.BlockSpec((1, D, EXPERT_FF), lambda i, be, nu: (be[i], 0, 0)),
                  pl.BlockSpec((1, D, EXPERT_FF), lambda i, be, nu: (be[i], 0, 0)),
                  pl.BlockSpec((1, EXPERT_FF, D), lambda i, be, nu: (be[i], 0, 0))],
        out_specs=pl.BlockSpec((MOE_BLOCK, D), lambda i, be, nu: (i, 0)),
        scratch_shapes=[pltpu.VMEM((NBUF, MOE_BLOCK, D), F32), pltpu.SemaphoreType.DMA((NBUF,))],
    )
    return pl.pallas_call(
        _expert_kernel,
        grid_spec=grid_spec,
        out_shape=jax.ShapeDtypeStruct((nb * MOE_BLOCK, D), F32),
        compiler_params=_cparams("arbitrary"),
        name="expert_ffn",
    )(block_expert, n_used, slot_token, slot_token, slot_token, h, w_gate, w_up, w_down)


def _combine_kernel(dc_ref, dn_ref, h_ref, gate_ref, g_ref, b_ref, y_hbm, o_ref, ybuf, sem, *, tm):
    i = pl.program_id(0)
    last = pl.num_programs(0) - 1

    def row(d_ref, s, t, k):
        pltpu.make_async_copy(y_hbm.at[pl.ds(d_ref[0, 0, TOP_K * t + k], 1)], ybuf.at[s, k, pl.ds(t, 1)],
                              sem.at[s]).start()

    def wait(s):
        for k in range(TOP_K):
            pltpu.make_async_copy(y_hbm.at[pl.ds(0, tm)], ybuf.at[s, k], sem.at[s]).wait()

    @pl.when(i == 0)
    def _():
        def body(t, c):
            for k in range(TOP_K):
                row(dc_ref, 0, t, k)
            return c
        lax.fori_loop(0, tm, body, 0, unroll=8)

    def step(cur):
        wait(cur)
        for t in range(tm):
            for k in range(TOP_K):
                row(dn_ref, 1 - cur, t, k)
        gt = gate_ref[...]
        ffn = ybuf[cur, 0] * gt[:, 0:1] + ybuf[cur, 1] * gt[:, 1:2]
        o_ref[...] = _ln_rows(DN_ALPHA * h_ref[...] + ffn, g_ref[...], b_ref[...])

        @pl.when(i == last)
        def _():
            wait(1 - cur)

    for s in range(2):
        pl.when(i % 2 == s)(functools.partial(step, s))


def _combine_ln(h, y, dest, gates, g, b, tm=512):
    T, D = h.shape
    n = T // tm
    dest = dest.reshape(n, 1, TOP_K * tm)
    smem = lambda f: pl.BlockSpec((1, 1, TOP_K * tm), f, memory_space=pltpu.SMEM)
    return pl.pallas_call(
        functools.partial(_combine_kernel, tm=tm),
        grid=(n,),
        in_specs=[smem(lambda i: (i, 0, 0)), smem(lambda i: (jnp.minimum(i + 1, n - 1), 0, 0)),
                  pl.BlockSpec((tm, D), lambda i: (i, 0)), pl.BlockSpec((tm, TOP_K), lambda i: (i, 0)),
                  _const_spec((1, D)), _const_spec((1, D)), pl.BlockSpec(memory_space=pl.ANY)],
        out_specs=pl.BlockSpec((tm, D), lambda i: (i, 0)),
        out_shape=jax.ShapeDtypeStruct((T, D), F32),
        scratch_shapes=[pltpu.VMEM((2, TOP_K, tm, D), F32), pltpu.SemaphoreType.DMA((2,))],
        compiler_params=_cparams("arbitrary"),
        name="combine_ln",
    )(dest, dest, h, gates, g.reshape(1, D), b.reshape(1, D), y)


def _mixer(h, lw, tabs):
    *atts, zx, dt_raw = _in_proj(h, lw["w_att"], lw["w_zx"], lw["w_dt"], tabs)
    os_, lses = [], []
    for gi in range(N_DIL):
        o, lse = _dilated_attention(atts[gi], gi)
        os_.append(o)
        lses.append(lse)
    xc = _conv_silu(zx, lw["conv_w"], lw["conv_b"])
    ssd = _ssd(xc, dt_raw, zx, lw["dt_bias"], lw["a_log"], lw["d_skip"], lw["ssd_norm_w"])
    return _out_proj_ln(ssd, os_, lses, h, lw["w_out"], lw["ln1_g"], lw["ln1_b"])


def _moe(h, lw):
    T, D = h.shape
    A = T * TOP_K
    ids, gates = _router(h, lw["w_router"])
    expert = ids[:TOP_K].T.reshape(A)
    order = jnp.argsort(expert, stable=True).astype(I32)
    inv = jnp.argsort(order).astype(I32)
    onehot = expert[:, None] == jnp.arange(N_EXPERTS, dtype=I32)[None, :]
    counts = jnp.sum(onehot, 0, dtype=I32)
    padded = (counts + MOE_BLOCK - 1) // MOE_BLOCK * MOE_BLOCK
    pad_end = jnp.cumsum(padded)
    pad_start = pad_end - padded
    start = jnp.cumsum(counts) - counts
    dest = inv + jnp.sum(jnp.where(onehot, (pad_start - start)[None, :], 0), -1, dtype=I32)
    n_blocks = -(-A // MOE_BLOCK) + N_EXPERTS
    blk0 = jnp.arange(n_blocks, dtype=I32) * MOE_BLOCK
    block_expert = jnp.minimum(jnp.sum(pad_end[None, :] <= blk0[:, None], -1, dtype=I32), N_EXPERTS - 1)
    n_used = (pad_end[-1] // MOE_BLOCK).astype(I32).reshape(1)
    sel = block_expert[:, None] == jnp.arange(N_EXPERTS, dtype=I32)[None, :]
    pick = lambda v: jnp.sum(jnp.where(sel, v[None, :], 0), -1, dtype=I32)[:, None]
    off = blk0[:, None] - pick(pad_start) + jnp.arange(MOE_BLOCK, dtype=I32)[None, :]
    asg = order[jnp.clip(pick(start) + off, 0, A - 1)]
    slot_token = jnp.where(off < pick(counts), asg // TOP_K, 0).reshape(n_blocks, 1, MOE_BLOCK)
    y = _expert_ffn(h, slot_token, block_expert, n_used, lw["w_gate"], lw["w_up"], lw["w_down"])
    return _combine_ln(h, y, dest, gates[:TOP_K].T, lw["ln2_g"], lw["ln2_b"])


def _trunk(x, ln_in_g, ln_in_b, layers):
    b, s, D = x.shape
    tabs = _rope_tables(s)
    h = _layer_norm(x.reshape(b * s, D), ln_in_g, ln_in_b)
    for lw in layers:
        h = _mixer(h.reshape(b, s, D), lw, tabs)
        h = _moe(h.reshape(b * s, D), lw)
    return h.reshape(b, s, D)


def _prep_layers(w_in, conv_w, conv_b, a_log, dt_bias, d_skip, ssd_norm_w, w_out, ln1_g, ln1_b,
                 router_group, router_expert, w_gate, w_up, w_down, ln2_g, ln2_b):
    layers = []
    z0 = ATT_PROJ
    x0 = ATT_PROJ + SSD_WIDTH
    d0 = x0 + CONV_DIM
    for i in range(w_in.shape[0]):
        w = w_in[i]
        w_dt = jnp.pad(w[:, d0:], ((0, 0), (0, LANES - 2 * SSD_HEADS))).astype(BF16)
        w_router = jnp.concatenate([
            router_group[i].T, jnp.zeros((8 - N_EXPERT_GROUPS, D_MODEL), F32), router_expert[i].T], 0)
        layers.append(dict(
            w_att=w[:, :z0].astype(BF16),
            w_zx=jnp.concatenate([w[:, x0:d0], w[:, z0:x0]], 1).astype(BF16),
            w_dt=w_dt,
            conv_w=conv_w[i], conv_b=conv_b[i], a_log=a_log[i], dt_bias=dt_bias[i], d_skip=d_skip[i],
            ssd_norm_w=ssd_norm_w[i], w_out=w_out[i].astype(BF16), ln1_g=ln1_g[i], ln1_b=ln1_b[i],
            w_router=w_router, w_gate=w_gate[i].astype(BF16), w_up=w_up[i].astype(BF16),
            w_down=w_down[i].astype(BF16), ln2_g=ln2_g[i], ln2_b=ln2_b[i]))
    return layers


def kernel(x_prompt, x_sample, ln_in_g, ln_in_b, w_in, conv_w, conv_b, a_log, dt_bias, d_skip, ssd_norm_w,
           w_out, ln1_g, ln1_b, router_group, router_expert, w_gate, w_up, w_down, ln2_g, ln2_b):
    layers = _prep_layers(w_in, conv_w, conv_b, a_log, dt_bias, d_skip, ssd_norm_w, w_out, ln1_g, ln1_b,
                          router_group, router_expert, w_gate, w_up, w_down, ln2_g, ln2_b)
    y_prompt = _trunk(x_prompt, ln_in_g, ln_in_b, layers)
    y_sample = _trunk(x_sample, ln_in_g, ln_in_b, layers)
    return (y_prompt, y_sample)
```

```python
import functools

import numpy as np
import jax
import jax.numpy as jnp
from jax import lax
from jax.experimental import pallas as pl
from jax.experimental.pallas import tpu as pltpu

F32 = jnp.float32
BF16 = jnp.bfloat16
I32 = jnp.int32

D_MODEL = 1024
DEPTH = 4
HEAD_DIM = 64
DILATED_PATTERNS = ((128, 1), (512, 4), (2048, 16))
N_DIL = 3
ATT_HEADS = 8
ATT_WIDTH = ATT_HEADS * HEAD_DIM
ATT_PROJ = N_DIL * 3 * ATT_WIDTH
ROPE_THETA = 10000.0
SSD_HEADS = 8
SSD_WIDTH = SSD_HEADS * HEAD_DIM
SSD_GROUPS = 2
HEADS_PER_GROUP = SSD_HEADS // SSD_GROUPS
D_STATE = 128
CONV_K = 5
CONV_DIM = SSD_WIDTH + 2 * SSD_GROUPS * D_STATE
N_EXPERT_GROUPS = 4
EXPERTS_PER_GROUP = 8
N_EXPERTS = N_EXPERT_GROUPS * EXPERTS_PER_GROUP
TOP_K = 2
EXPERT_FF = 512
MOE_BLOCK = 256
DN_ALPHA = (2 * DEPTH) ** 0.25
LN_EPS = 1e-5
RMS_EPS = 1e-5
NEG = -1e30

LANES = 128
SSD_CHUNK = 128
ATT_SUB = 128
ATT_HALO = 64
CONV_HALO = 16
NBUF = 3
VMEM_LIMIT = 48 * 1024 * 1024


def _cparams(*sem):
    return pltpu.CompilerParams(dimension_semantics=sem, vmem_limit_bytes=VMEM_LIMIT)


def _const_spec(shape):
    nd = len(shape)
    return pl.BlockSpec(shape, lambda *_: (0,) * nd)


def _ln_rows(x, g, b):
    mu = jnp.mean(x, -1, keepdims=True)
    xc = x - mu
    var = jnp.mean(xc * xc, -1, keepdims=True)
    return xc * lax.rsqrt(var + LN_EPS) * g + b


def _silu(x):
    return x / (1.0 + jnp.exp(-x))


def _ln_kernel(x_ref, g_ref, b_ref, o_ref):
    o_ref[...] = _ln_rows(x_ref[...], g_ref[...], b_ref[...])


def _layer_norm(x, g, b, tm=512):
    T, D = x.shape
    return pl.pallas_call(
        _ln_kernel,
        grid=(T // tm,),
        in_specs=[pl.BlockSpec((tm, D), lambda i: (i, 0)), _const_spec((1, D)), _const_spec((1, D))],
        out_specs=pl.BlockSpec((tm, D), lambda i: (i, 0)),
        out_shape=jax.ShapeDtypeStruct((T, D), F32),
        compiler_params=_cparams("parallel"),
        name="ln_in",
    )(x, g.reshape(1, D), b.reshape(1, D))


def _rope_chunks(y, tab):
    cos = tab[:, :LANES]
    sin = tab[:, LANES:]
    lane = lax.broadcasted_iota(I32, (1, LANES), 1)
    first = (lane % HEAD_DIM) < (HEAD_DIM // 2)
    out = []
    for c in range(ATT_WIDTH // LANES):
        tc = y[:, c * LANES:(c + 1) * LANES]
        rot = jnp.where(first, pltpu.roll(tc, LANES - HEAD_DIM // 2, 1), pltpu.roll(tc, HEAD_DIM // 2, 1))
        out.append(tc * cos + rot * sin)
    return out


def _proj_kernel(x_ref, wa_ref, wz_ref, wd_ref, t0_ref, t1_ref, t2_ref,
                 a0_ref, a1_ref, a2_ref, oz_ref, od_ref, xc_ref, xs_ref, *, tm, tn):
    xb = x_ref[0].astype(BF16)
    nlc = x_ref.shape[2] // LANES
    for c in range(nlc):
        xc_ref[c] = x_ref[0, :, c * LANES:(c + 1) * LANES]
    for j in range(oz_ref.shape[2] // tn):
        sl = slice(j * tn, (j + 1) * tn)
        oz_ref[0, :, sl] = jnp.dot(xb, wz_ref[:, sl], preferred_element_type=F32).astype(oz_ref.dtype)
    od_ref[0] = jnp.dot(xb, wd_ref[...], preferred_element_type=F32)
    scale = HEAD_DIM ** -0.5
    nchunk = ATT_WIDTH // LANES
    for g, (t_ref, a_ref) in enumerate(((t0_ref, a0_ref), (t1_ref, a1_ref), (t2_ref, a2_ref))):
        dil = DILATED_PATTERNS[g][1]
        n = tm // dil
        if dil == 1:
            xp = xb
        else:
            for r in range(dil):
                for c in range(nlc):
                    xs_ref[r * n:(r + 1) * n, c * LANES:(c + 1) * LANES] = (
                        xc_ref[c, pl.ds(r, n, stride=dil), :].astype(BF16))
            xp = xs_ref[...]
        tab = t_ref[...].reshape(tm, 2 * LANES)
        for j in range(3):
            c0 = (g * 3 + j) * ATT_WIDTH
            y = jnp.dot(xp, wa_ref[:, c0:c0 + ATT_WIDTH], preferred_element_type=F32)
            if j < 2:
                chunks = _rope_chunks(y, tab)
                if j == 0:
                    chunks = [ch * scale for ch in chunks]
            else:
                chunks = [y[:, c * LANES:(c + 1) * LANES] for c in range(nchunk)]
            for c, ch in enumerate(chunks):
                chb = ch.astype(BF16)
                col = j * ATT_WIDTH + c * LANES
                for r in range(dil):
                    a_ref[0, r, :, col:col + LANES] = chb[r * n:(r + 1) * n]


def _in_proj(h, w_att, w_zx, w_dt, tabs, tm=512, tn=512):
    b, s, D = h.shape
    dils = [d for _, d in DILATED_PATTERNS]
    qkv = 3 * ATT_WIDTH
    nz, nd = w_zx.shape[1], w_dt.shape[1]
    row = lambda n: pl.BlockSpec((1, tm, n), lambda bi, i: (bi, i, 0))
    sub = lambda d, n: pl.BlockSpec((1, d, tm // d, n), lambda bi, i: (bi, 0, i, 0))
    return pl.pallas_call(
        functools.partial(_proj_kernel, tm=tm, tn=tn),
        grid=(b, s // tm),
        in_specs=[row(D)] + [_const_spec(w.shape) for w in (w_att, w_zx, w_dt)]
        + [pl.BlockSpec((d, tm // d, 2 * LANES), lambda bi, i: (0, i, 0)) for d in dils],
        out_specs=[sub(d, qkv) for d in dils] + [row(nz), row(nd)],
        out_shape=[jax.ShapeDtypeStruct((b, d, s // d, qkv), BF16) for d in dils]
        + [jax.ShapeDtypeStruct((b, s, nz), BF16), jax.ShapeDtypeStruct((b, s, nd), F32)],
        scratch_shapes=[pltpu.VMEM((D // LANES, tm, LANES), F32), pltpu.VMEM((tm, D), BF16)],
        compiler_params=_cparams("parallel", "parallel"),
        name="in_proj",
    )(h, w_att, w_zx, w_dt, *tabs)


def _attn_kernel(q_ref, kc_ref, kp_ref, kn_ref, vc_ref, vp_ref, vn_ref, o_ref, lse_ref, kbuf, vbuf,
                 *, bq, seq_len, half):
    qi = pl.program_id(2)
    kbuf[0:ATT_HALO] = kp_ref[0, 0]
    kbuf[ATT_HALO:ATT_HALO + bq] = kc_ref[0, 0]
    kbuf[ATT_HALO + bq:] = kn_ref[0, 0]
    vbuf[0:ATT_HALO] = vp_ref[0, 0]
    vbuf[ATT_HALO:ATT_HALO + bq] = vc_ref[0, 0]
    vbuf[ATT_HALO + bq:] = vn_ref[0, 0]

    nk = ATT_SUB + 2 * ATT_HALO
    npair = ATT_HEADS // 2
    ri = lax.broadcasted_iota(I32, (ATT_SUB, nk), 0)
    ci = lax.broadcasted_iota(I32, (ATT_SUB, nk), 1)
    band = jnp.abs(ci - ATT_HALO - ri) <= half
    lane = lax.broadcasted_iota(I32, (1, LANES), 1)
    even = lane < HEAD_DIM
    for sb in range(bq // ATT_SUB):
        r0 = sb * ATT_SUB
        kpos = qi * bq + r0 - ATT_HALO + ci
        mask = band & (kpos >= 0) & (kpos < seq_len)
        ss = []
        for j in range(npair):
            cs = slice(j * LANES, (j + 1) * LANES)
            qp = q_ref[0, 0, r0:r0 + ATT_SUB, cs]
            zero = jnp.zeros_like(qp)
            lhs = jnp.concatenate([jnp.where(even, qp, zero), jnp.where(even, zero, qp)], axis=0)
            ss.append(lax.dot_general(lhs, kbuf[r0:r0 + nk, cs], (((1,), (1,)), ((), ())),
                                      preferred_element_type=F32))
        s = jnp.stack(ss).reshape(ATT_HEADS, ATT_SUB, nk)
        s = jnp.where(mask[None], s, NEG)
        m = jnp.max(s, -1, keepdims=True)
        p = jnp.exp(s - m)
        l = jnp.sum(p, -1, keepdims=True)
        pb = p.astype(BF16).reshape(npair, 2 * ATT_SUB, nk)
        l2 = l.reshape(npair, 2 * ATT_SUB, 1)
        lse = m + jnp.log(l)
        for j in range(npair):
            cs = slice(j * LANES, (j + 1) * LANES)
            o2 = jnp.dot(pb[j], vbuf[r0:r0 + nk, cs], preferred_element_type=F32) / l2[j]
            o_ref[0, 0, r0:r0 + ATT_SUB, cs] = jnp.where(even, o2[:ATT_SUB], o2[ATT_SUB:]).astype(o_ref.dtype)
        lse_tile = jnp.zeros((ATT_SUB, LANES), F32)
        for h in range(ATT_HEADS):
            lse_tile = jnp.where(lane == h, lse[h], lse_tile)
        lse_ref[0, 0, r0:r0 + ATT_SUB, :] = lse_tile


def _dilated_attention(att, gi, bq=512):
    win, dil = DILATED_PATTERNS[gi]
    half = win // (2 * dil)
    assert half <= ATT_HALO
    b, _, L, _ = att.shape
    bq = min(bq, L)
    nq = L // bq
    hb = bq // ATT_HALO
    nhb = L // ATT_HALO
    cur = lambda j: (lambda bi, r, qi: (bi, r, qi, j))
    prev = lambda j: (lambda bi, r, qi: (bi, r, jnp.maximum(qi * hb - 1, 0), j))
    nxt = lambda j: (lambda bi, r, qi: (bi, r, jnp.minimum((qi + 1) * hb, nhb - 1), j))
    blk = lambda n, f: pl.BlockSpec((1, 1, n, ATT_WIDTH), f)
    return pl.pallas_call(
        functools.partial(_attn_kernel, bq=bq, seq_len=L, half=half),
        grid=(b, dil, nq),
        in_specs=[blk(bq, cur(0)),
                  blk(bq, cur(1)), blk(ATT_HALO, prev(1)), blk(ATT_HALO, nxt(1)),
                  blk(bq, cur(2)), blk(ATT_HALO, prev(2)), blk(ATT_HALO, nxt(2))],
        out_specs=[pl.BlockSpec((1, 1, bq, ATT_WIDTH), lambda bi, r, qi: (bi, r, qi, 0)),
                   pl.BlockSpec((1, 1, bq, LANES), lambda bi, r, qi: (bi, r, qi, 0))],
        out_shape=[jax.ShapeDtypeStruct((b, dil, L, ATT_WIDTH), BF16),
                   jax.ShapeDtypeStruct((b, dil, L, LANES), F32)],
        scratch_shapes=[pltpu.VMEM((bq + 2 * ATT_HALO, ATT_WIDTH), BF16),
                        pltpu.VMEM((bq + 2 * ATT_HALO, ATT_WIDTH), BF16)],
        compiler_params=_cparams("parallel", "parallel", "parallel"),
        name=f"dil_attn_{gi}",
    )(att, att, att, att, att, att, att)


def _rope_tables(s):
    half = HEAD_DIM // 2
    inv = ROPE_THETA ** (-jnp.arange(half, dtype=F32) / half)
    ang = jnp.arange(s).astype(F32)[:, None] * inv[None, :]
    cos, sin = jnp.cos(ang), jnp.sin(ang)
    cos_h = jnp.concatenate([cos, cos], -1)
    sin_h = jnp.concatenate([-sin, sin], -1)
    rep = LANES // HEAD_DIM
    tab = jnp.concatenate([jnp.tile(cos_h, (1, rep)), jnp.tile(sin_h, (1, rep))], -1)
    return [tab.reshape(s // d, d, 2 * LANES).transpose(1, 0, 2) for _, d in DILATED_PATTERNS]


def _conv_kernel(c_ref, p_ref, n_ref, w_ref, b_ref, o_ref, buf, *, ts):
    i = pl.program_id(1)
    last = pl.num_programs(1) - 1
    pad = (CONV_K - 1) // 2
    buf[0:CONV_HALO] = jnp.where(i > 0, p_ref[0].astype(F32), 0.0)
    buf[CONV_HALO:CONV_HALO + ts] = c_ref[0].astype(F32)
    buf[CONV_HALO + ts:] = jnp.where(i < last, n_ref[0].astype(F32), 0.0)
    acc = jnp.zeros((ts, CONV_DIM), F32) + b_ref[...]
    for j in range(CONV_K):
        acc = acc + w_ref[j:j + 1, :] * buf[CONV_HALO - pad + j:CONV_HALO - pad + j + ts, :]
    o_ref[0] = _silu(acc).astype(o_ref.dtype)


def _conv_silu(zx_v, conv_w, conv_b, ts=512):
    b, s, _ = zx_v.shape
    ts = min(ts, s)
    r = ts // CONV_HALO
    nh = s // CONV_HALO
    w8 = jnp.zeros((8, CONV_DIM), F32).at[:CONV_K].set(conv_w)
    return pl.pallas_call(
        functools.partial(_conv_kernel, ts=ts),
        grid=(b, s // ts),
        in_specs=[pl.BlockSpec((1, ts, CONV_DIM), lambda bi, i: (bi, i, 0)),
                  pl.BlockSpec((1, CONV_HALO, CONV_DIM), lambda bi, i: (bi, jnp.maximum(i * r - 1, 0), 0)),
                  pl.BlockSpec((1, CONV_HALO, CONV_DIM), lambda bi, i: (bi, jnp.minimum((i + 1) * r, nh - 1), 0)),
                  _const_spec((8, CONV_DIM)), _const_spec((1, CONV_DIM))],
        out_specs=pl.BlockSpec((1, ts, CONV_DIM), lambda bi, i: (bi, i, 0)),
        out_shape=jax.ShapeDtypeStruct((b, s, CONV_DIM), BF16),
        scratch_shapes=[pltpu.VMEM((ts + 2 * CONV_HALO, CONV_DIM), F32)],
        compiler_params=_cparams("parallel", "parallel"),
        name="conv_silu",
    )(zx_v, zx_v, zx_v, w8, conv_b.reshape(1, CONV_DIM))


def _expand_heads(v, off):
    head = lax.broadcasted_iota(I32, (1, SSD_WIDTH), 1) // HEAD_DIM
    out = jnp.zeros((v.shape[0], SSD_WIDTH), F32)
    for h in range(SSD_HEADS):
        out = jnp.where(head == h, v[:, off + h:off + h + 1], out)
    return out


def _head_selector(off):
    r = lax.broadcasted_iota(I32, (LANES, SSD_WIDTH), 0)
    c = lax.broadcasted_iota(I32, (LANES, SSD_WIDTH), 1)
    return (r == c // HEAD_DIM + off).astype(BF16)


def _expand_heads_mxu(v, sel):
    hi = v.astype(BF16)
    lo = (v - hi.astype(F32)).astype(BF16)
    return (jnp.dot(hi, sel, preferred_element_type=F32) + jnp.dot(lo, sel, preferred_element_type=F32))


def _softplus(x):
    return jnp.maximum(x, 0.0) + jnp.log(1.0 + jnp.exp(-jnp.abs(x)))


def _ssd_chunk(xc, dtr, bias, a_row, state_ref, *, reverse, off):
    Q = SSD_CHUNK
    dt = _softplus(dtr + bias)
    a = dt * a_row
    ri = lax.broadcasted_iota(I32, (Q, Q), 0)
    ci = lax.broadcasted_iota(I32, (Q, Q), 1)
    keep = (ci >= ri) if reverse else (ci <= ri)
    tri = keep.astype(BF16)
    a0 = a.astype(BF16)
    a1 = (a - a0.astype(F32)).astype(BF16)
    a2 = (a - a0.astype(F32) - a1.astype(F32)).astype(BF16)
    cum = (jnp.dot(tri, a0, preferred_element_type=F32) + jnp.dot(tri, a1, preferred_element_type=F32)
           + jnp.dot(tri, a2, preferred_element_type=F32))
    cum_t = cum.T
    edge = 0 if reverse else Q - 1
    tot = cum[edge:edge + 1, :]
    sel = _head_selector(off)
    dt512 = _expand_heads_mxu(dt, sel)
    dec512 = _expand_heads_mxu(jnp.exp(tot - cum), sel)
    ecum512 = _expand_heads_mxu(jnp.exp(cum), sel)
    etot512 = _expand_heads(jnp.exp(tot), off)
    xs = xc[:, :SSD_WIDTH].astype(F32)
    xdt = xs * dt512
    xdt_b = xdt.astype(BF16)
    xdd_b = (xdt * dec512).astype(BF16)
    gw = HEADS_PER_GROUP * HEAD_DIM
    lane = lax.broadcasted_iota(I32, (1, LANES), 1)
    even = lane < HEAD_DIM
    ys = []
    for g in range(SSD_GROUPS):
        bg = xc[:, SSD_WIDTH + g * D_STATE:SSD_WIDTH + (g + 1) * D_STATE]
        cg = xc[:, SSD_WIDTH + (SSD_GROUPS + g) * D_STATE:SSD_WIDTH + (SSD_GROUPS + g + 1) * D_STATE]
        cb = lax.dot_general(cg, bg, (((1,), (1,)), ((), ())), preferred_element_type=F32)
        sg = state_ref[g]
        yoff = jnp.dot(cg, sg.astype(BF16), preferred_element_type=F32)
        for pr in range(HEADS_PER_GROUP // 2):
            ms = []
            for hh in (2 * pr, 2 * pr + 1):
                ln = off + g * HEADS_PER_GROUP + hh
                seg = cum[:, ln:ln + 1] - cum_t[ln:ln + 1, :]
                lmat = jnp.where(keep, jnp.exp(jnp.where(keep, seg, 0.0)), 0.0)
                ms.append((cb * lmat).astype(BF16))
            c0 = g * gw + pr * LANES
            yd2 = jnp.dot(jnp.concatenate(ms, axis=0), xdt_b[:, c0:c0 + LANES], preferred_element_type=F32)
            yd = jnp.where(even, yd2[:Q], yd2[Q:])
            ys.append(yd + yoff[:, pr * LANES:(pr + 1) * LANES] * ecum512[:, c0:c0 + LANES])
        bg_t = bg.astype(F32).T.astype(BF16)
        state_ref[g] = sg * etot512[:, g * gw:(g + 1) * gw] + jnp.dot(
            bg_t, xdd_b[:, g * gw:(g + 1) * gw], preferred_element_type=F32)
    return ys


def _ssd_fwd_kernel(x_ref, dt_ref, bias_ref, a_ref, y_ref, state_ref, *, nch):
    @pl.when(pl.program_id(1) == 0)
    def _():
        state_ref[...] = jnp.zeros_like(state_ref)

    for c in range(nch):
        rows = slice(c * SSD_CHUNK, (c + 1) * SSD_CHUNK)
        ys = _ssd_chunk(x_ref[0, rows, :], dt_ref[0, rows, :], bias_ref[...], a_ref[...], state_ref,
                        reverse=False, off=0)
        for j, y in enumerate(ys):
            y_ref[0, rows, j * LANES:(j + 1) * LANES] = y


def _ssd_bwd_kernel(x_ref, dt_ref, bias_ref, a_ref, yf_ref, z_ref, dskip_ref, nw_ref, o_ref, state_ref,
                    ybuf, *, nch):
    @pl.when(pl.program_id(1) == 0)
    def _():
        state_ref[...] = jnp.zeros_like(state_ref)

    for c in reversed(range(nch)):
        rows = slice(c * SSD_CHUNK, (c + 1) * SSD_CHUNK)
        ys = _ssd_chunk(x_ref[0, rows, :], dt_ref[0, rows, :], bias_ref[...], a_ref[...], state_ref,
                        reverse=True, off=SSD_HEADS)
        for j, y in enumerate(ys):
            ybuf[rows, j * LANES:(j + 1) * LANES] = y
    xs = x_ref[0, :, :SSD_WIDTH].astype(F32)
    y = yf_ref[0] + ybuf[...] + xs * dskip_ref[...]
    gy = y * _silu(z_ref[0].astype(F32))
    gw = SSD_WIDTH // SSD_GROUPS
    for g in range(SSD_GROUPS):
        part = gy[:, g * gw:(g + 1) * gw]
        ms = jnp.mean(part * part, -1, keepdims=True)
        o_ref[0, :, g * gw:(g + 1) * gw] = (part * lax.rsqrt(ms + RMS_EPS) * nw_ref[:, g * gw:(g + 1) * gw]
                                             ).astype(o_ref.dtype)


def _ssd(xc, dt_v, zx_v, dt_bias, a_log, d_skip, norm_w, nch=4):
    b, s, _ = xc.shape
    nch = min(nch, s // SSD_CHUNK)
    R = nch * SSD_CHUNK
    n = s // R
    a_neg = -jnp.exp(a_log.astype(F32))
    pad = LANES - 2 * SSD_HEADS
    bias = jnp.pad(dt_bias.astype(F32).reshape(1, 2 * SSD_HEADS), ((0, 0), (0, pad)))
    a_f = jnp.pad(a_neg[0].reshape(1, SSD_HEADS), ((0, 0), (0, LANES - SSD_HEADS)))
    a_b = jnp.pad(a_neg[1].reshape(1, SSD_HEADS), ((0, 0), (SSD_HEADS, pad)))
    dskip = jnp.repeat(d_skip.astype(F32), HEAD_DIM).reshape(1, SSD_WIDTH)
    state = pltpu.VMEM((SSD_GROUPS, D_STATE, HEADS_PER_GROUP * HEAD_DIM), F32)
    fwd = lambda bi, i: (bi, i, 0)
    rev = lambda bi, i: (bi, n - 1 - i, 0)
    y_f = pl.pallas_call(
        functools.partial(_ssd_fwd_kernel, nch=nch),
        grid=(b, n),
        in_specs=[pl.BlockSpec((1, R, CONV_DIM), fwd), pl.BlockSpec((1, R, LANES), fwd),
                  _const_spec((1, LANES)), _const_spec((1, LANES))],
        out_specs=pl.BlockSpec((1, R, SSD_WIDTH), fwd),
        out_shape=jax.ShapeDtypeStruct((b, s, SSD_WIDTH), F32),
        scratch_shapes=[state],
        compiler_params=_cparams("parallel", "arbitrary"),
        name="ssd_fwd",
    )(xc, dt_v, bias, a_f)
    out = pl.pallas_call(
        functools.partial(_ssd_bwd_kernel, nch=nch),
        grid=(b, n),
        in_specs=[pl.BlockSpec((1, R, CONV_DIM), rev), pl.BlockSpec((1, R, LANES), rev),
                  _const_spec((1, LANES)), _const_spec((1, LANES)),
                  pl.BlockSpec((1, R, SSD_WIDTH), rev),
                  pl.BlockSpec((1, R, SSD_WIDTH), lambda bi, i: (bi, n - 1 - i, CONV_DIM // SSD_WIDTH)),
                  _const_spec((1, SSD_WIDTH)), _const_spec((1, SSD_WIDTH))],
        out_specs=pl.BlockSpec((1, R, SSD_WIDTH), rev),
        out_shape=jax.ShapeDtypeStruct((b, s, SSD_WIDTH), BF16),
        scratch_shapes=[state, pltpu.VMEM((R, SSD_WIDTH), F32)],
        compiler_params=_cparams("parallel", "arbitrary"),
        name="ssd_bwd",
    )(xc, dt_v, bias, a_b, y_f, zx_v, dskip, norm_w.astype(F32).reshape(1, SSD_WIDTH))
    return out


def _natural_order(src_ref, scr_ref):
    dil, n, w = src_ref.shape[1:]
    if dil == 1:
        return src_ref[0, 0].astype(F32)
    for r in range(dil):
        for c in range(w // LANES):
            scr_ref[c, pl.ds(r, n, stride=dil), :] = src_ref[0, r, :, c * LANES:(c + 1) * LANES].astype(F32)
    return jnp.concatenate([scr_ref[c] for c in range(w // LANES)], axis=1)


def _out_proj_kernel(ssd_ref, o0_ref, o1_ref, o2_ref, l0_ref, l1_ref, l2_ref, h_ref, w1_ref, w2_ref,
                     g_ref, b_ref, out_ref, so1, so2, sl1, sl2):
    lses = [_natural_order(r, s) for r, s in ((l0_ref, None), (l1_ref, sl1), (l2_ref, sl2))]
    mx = jnp.maximum(jnp.maximum(lses[0], lses[1]), lses[2])
    es = [jnp.exp(l - mx) for l in lses]
    den = es[0] + es[1] + es[2]
    sel = _head_selector(0)
    att = jnp.zeros((h_ref.shape[1], ATT_WIDTH), F32)
    for e, o_ref, scr in zip(es, (o0_ref, o1_ref, o2_ref), (None, so1, so2)):
        att = att + _expand_heads_mxu(e / den, sel) * _natural_order(o_ref, scr)
    y = jnp.dot(ssd_ref[0], w1_ref[...], preferred_element_type=F32)
    y = y + jnp.dot(att.astype(BF16), w2_ref[...], preferred_element_type=F32)
    out_ref[0] = _ln_rows(DN_ALPHA * h_ref[0] + y, g_ref[...], b_ref[...])


def _out_proj_ln(ssd, os_, lses, h, w_out, g, b, tm=512):
    bsz, s, D = h.shape
    dils = [d for _, d in DILATED_PATTERNS]
    row = lambda n: pl.BlockSpec((1, tm, n), lambda bi, i: (bi, i, 0))
    sub = lambda d, n: pl.BlockSpec((1, d, tm // d, n), lambda bi, i: (bi, 0, i, 0))
    w1 = w_out[:SSD_WIDTH]
    w2 = w_out[SSD_WIDTH:]
    return pl.pallas_call(
        _out_proj_kernel,
        grid=(bsz, s // tm),
        in_specs=[row(SSD_WIDTH)] + [sub(d, ATT_WIDTH) for d in dils] + [sub(d, LANES) for d in dils]
        + [row(D), _const_spec(w1.shape), _const_spec(w2.shape), _const_spec((1, D)), _const_spec((1, D))],
        out_specs=row(D),
        out_shape=jax.ShapeDtypeStruct((bsz, s, D), F32),
        scratch_shapes=[pltpu.VMEM((ATT_WIDTH // LANES, tm, LANES), F32)] * 2
        + [pltpu.VMEM((1, tm, LANES), F32)] * 2,
        compiler_params=_cparams("parallel", "parallel"),
        name="out_proj_ln",
    )(ssd, *os_, *lses, h, w1, w2, g.reshape(1, D), b.reshape(1, D))


def _router_kernel(h_ref, w_ref, id_ref, gate_ref):
    nt = (((1,), (1,)), ((), ()))
    h = h_ref[...]
    h0 = h.astype(BF16)
    h1 = (h - h0.astype(F32)).astype(BF16)
    nr = w_ref.shape[0] // 2
    r0 = lax.dot_general(w_ref[...], h0, nt, preferred_element_type=F32)
    r1 = lax.dot_general(w_ref[0:nr, :], h1, nt, preferred_element_type=F32)
    logits = r0[0:nr] + r0[nr:] + r1
    tm = logits.shape[1]
    row = lax.broadcasted_iota(I32, (8, tm), 0)
    lg = jnp.where(row < N_EXPERT_GROUPS, logits[0:8], NEG)
    gm = jnp.max(lg, 0, keepdims=True)
    gs = jnp.sum(jnp.exp(lg - gm), 0, keepdims=True)
    g_idx = jnp.min(jnp.where(lg == gm, row, 8), 0, keepdims=True)
    g_prob = 1.0 / gs
    el = jnp.zeros((8, tm), F32)
    for g in range(N_EXPERT_GROUPS):
        el = jnp.where(g_idx == g, logits[8 + 8 * g:16 + 8 * g], el)
    em = jnp.max(el, 0, keepdims=True)
    ee = jnp.exp(el - em)
    p = ee / jnp.sum(ee, 0, keepdims=True)
    p1 = jnp.max(p, 0, keepdims=True)
    i1 = jnp.min(jnp.where(p == p1, row, 8), 0, keepdims=True)
    pr = jnp.where(row == i1, -1.0, p)
    p2 = jnp.max(pr, 0, keepdims=True)
    i2 = jnp.min(jnp.where(pr == p2, row, 8), 0, keepdims=True)
    den = p1 + p2
    base = g_idx * EXPERTS_PER_GROUP
    id_ref[...] = jnp.where(row == 0, base + i1, jnp.where(row == 1, base + i2, 0))
    gate_ref[...] = jnp.where(row == 0, g_prob * p1 / den, jnp.where(row == 1, g_prob * p2 / den, 0.0))


def _router(h, w_router, tm=512):
    T, D = h.shape
    w0 = w_router.astype(BF16)
    w_router = jnp.concatenate([w0, (w_router - w0.astype(F32)).astype(BF16)], 0)
    nr = w_router.shape[0]
    return pl.pallas_call(
        _router_kernel,
        grid=(T // tm,),
        in_specs=[pl.BlockSpec((tm, D), lambda i: (i, 0)), _const_spec((nr, D))],
        out_specs=[pl.BlockSpec((8, tm), lambda i: (0, i)), pl.BlockSpec((8, tm), lambda i: (0, i))],
        out_shape=[jax.ShapeDtypeStruct((8, T), I32), jax.ShapeDtypeStruct((8, T), F32)],
        compiler_params=_cparams("parallel"),
        name="router",
    )(h, w_router)


def _expert_kernel(be_ref, nu_ref, tok0_ref, tok1_ref, tok2_ref, h_hbm, wg_ref, wu_ref, wd_ref, o_ref,
                   xbuf, gsem):
    i = pl.program_id(0)
    nu = nu_ref[0]

    def gather_row(tok_ref, s, j, q=0):
        pltpu.async_copy(h_hbm.at[pl.ds(tok_ref[0, 0, j], 1)], xbuf.at[s, pl.ds(j, 1)], gsem.at[s], priority=q)

    def gather_wait(s):
        pltpu.make_async_copy(h_hbm.at[pl.ds(0, MOE_BLOCK)], xbuf.at[s], gsem.at[s]).wait()

    def rolled(ref, s):
        def body(j, c):
            gather_row(ref, s, j)
            return c
        lax.fori_loop(0, MOE_BLOCK, body, 0, unroll=8)

    @pl.when(i == 0)
    def _():
        rolled(tok0_ref, 0)
        rolled(tok1_ref, 1)

    @pl.when(i >= nu)
    def _():
        o_ref[...] = jnp.zeros(o_ref.shape, F32)

    def step(cur):
        nxt, nx2 = (cur + 1) % NBUF, (cur + 2) % NBUF
        gather_wait(cur)

        def issue(part, nparts=4):
            n = MOE_BLOCK // nparts
            for j in range(part * n, (part + 1) * n):
                gather_row(tok2_ref, nx2, j, j % 2)

        x = xbuf[cur].astype(BF16)
        issue(0)
        hg = jnp.dot(x, wg_ref[0], preferred_element_type=F32)
        issue(1)
        hu = jnp.dot(x, wu_ref[0], preferred_element_type=F32)
        issue(2)
        hdn = (_silu(hg) * hu).astype(BF16)
        issue(3)
        o_ref[...] = jnp.dot(hdn, wd_ref[0], preferred_element_type=F32)

        @pl.when(i == nu - 1)
        def _():
            gather_wait(nxt)
            gather_wait(nx2)

    for k in range(NBUF):
        pl.when((i < nu) & (i % NBUF == k))(functools.partial(step, k))


def _expert_ffn(h, slot_token, block_expert, n_used, w_gate, w_up, w_down):
    T, D = h.shape
    nb = slot_token.shape[0]
    smem = lambda f: pl.BlockSpec((1, 1, MOE_BLOCK), f, memory_space=pltpu.SMEM)
    ahead = lambda k: (lambda i, be, nu: (jnp.minimum(i + k, nb - 1), 0, 0))
    grid_spec = pltpu.PrefetchScalarGridSpec(
        num_scalar_prefetch=2,
        grid=(nb,),
        in_specs=[smem(ahead(0)), smem(ahead(1)), smem(ahead(2)),
                  pl.BlockSpec(memory_space=pl.ANY),
                  pl.BlockSpec((1, D, EXPERT_FF), lambda i, be, nu: (be[i], 0, 0)),
                  pl.BlockSpec((1, D, EXPERT_FF), lambda i, be, nu: (be[i], 0, 0)),
                  pl.BlockSpec((1, EXPERT_FF, D), lambda i, be, nu: (be[i], 0, 0))],
        out_specs=pl.BlockSpec((MOE_BLOCK, D), lambda i, be, nu: (i, 0)),
        scratch_shapes=[pltpu.VMEM((NBUF, MOE_BLOCK, D), F32), pltpu.SemaphoreType.DMA((NBUF,))],
    )
    return pl.pallas_call(
        _expert_kernel,
        grid_spec=grid_spec,
        out_shape=jax.ShapeDtypeStruct((nb * MOE_BLOCK, D), F32),
        compiler_params=_cparams("arbitrary"),
        name="expert_ffn",
    )(block_expert, n_used, slot_token, slot_token, slot_token, h, w_gate, w_up, w_down)


def _combine_kernel(dc_ref, dn_ref, h_ref, gate_ref, g_ref, b_ref, y_hbm, o_ref, ybuf, sem, *, tm):
    i = pl.program_id(0)
    last = pl.num_programs(0) - 1

    def row(d_ref, s, t, k):
        pltpu.async_copy(y_hbm.at[pl.ds(d_ref[0, 0, TOP_K * t + k], 1)], ybuf.at[s, k, pl.ds(t, 1)],
                         sem.at[s], priority=k)

    def wait(s):
        for k in range(TOP_K):
            pltpu.make_async_copy(y_hbm.at[pl.ds(0, tm)], ybuf.at[s, k], sem.at[s]).wait()

    @pl.when(i == 0)
    def _():
        def body(t, c):
            for k in range(TOP_K):
                row(dc_ref, 0, t, k)
            return c
        lax.fori_loop(0, tm, body, 0, unroll=8)

    def step(cur):
        wait(cur)
        for t in range(tm):
            for k in range(TOP_K):
                row(dn_ref, 1 - cur, t, k)
        gt = gate_ref[...]
        ffn = ybuf[cur, 0] * gt[:, 0:1] + ybuf[cur, 1] * gt[:, 1:2]
        o_ref[...] = _ln_rows(DN_ALPHA * h_ref[...] + ffn, g_ref[...], b_ref[...])

        @pl.when(i == last)
        def _():
            wait(1 - cur)

    for s in range(2):
        pl.when(i % 2 == s)(functools.partial(step, s))


def _combine_ln(h, y, dest, gates, g, b, tm=512):
    T, D = h.shape
    n = T // tm
    dest = dest.reshape(n, 1, TOP_K * tm)
    smem = lambda f: pl.BlockSpec((1, 1, TOP_K * tm), f, memory_space=pltpu.SMEM)
    return pl.pallas_call(
        functools.partial(_combine_kernel, tm=tm),
        grid=(n,),
        in_specs=[smem(lambda i: (i, 0, 0)), smem(lambda i: (jnp.minimum(i + 1, n - 1), 0, 0)),
                  pl.BlockSpec((tm, D), lambda i: (i, 0)), pl.BlockSpec((tm, TOP_K), lambda i: (i, 0)),
                  _const_spec((1, D)), _const_spec((1, D)), pl.BlockSpec(memory_space=pl.ANY)],
        out_specs=pl.BlockSpec((tm, D), lambda i: (i, 0)),
        out_shape=jax.ShapeDtypeStruct((T, D), F32),
        scratch_shapes=[pltpu.VMEM((2, TOP_K, tm, D), F32), pltpu.SemaphoreType.DMA((2,))],
        compiler_params=_cparams("arbitrary"),
        name="combine_ln",
    )(dest, dest, h, gates, g.reshape(1, D), b.reshape(1, D), y)


def _mixer(h, lw, tabs):
    *atts, zx, dt_raw = _in_proj(h, lw["w_att"], lw["w_zx"], lw["w_dt"], tabs)
    os_, lses = [], []
    for gi in range(N_DIL):
        o, lse = _dilated_attention(atts[gi], gi)
        os_.append(o)
        lses.append(lse)
    xc = _conv_silu(zx, lw["conv_w"], lw["conv_b"])
    ssd = _ssd(xc, dt_raw, zx, lw["dt_bias"], lw["a_log"], lw["d_skip"], lw["ssd_norm_w"])
    return _out_proj_ln(ssd, os_, lses, h, lw["w_out"], lw["ln1_g"], lw["ln1_b"])


def _moe(h, lw):
    T, D = h.shape
    A = T * TOP_K
    ids, gates = _router(h, lw["w_router"])
    expert = ids[:TOP_K].T.reshape(A)
    order = jnp.argsort(expert, stable=True).astype(I32)
    inv = jnp.argsort(order).astype(I32)
    onehot = expert[:, None] == jnp.arange(N_EXPERTS, dtype=I32)[None, :]
    counts = jnp.sum(onehot, 0, dtype=I32)
    padded = (counts + MOE_BLOCK - 1) // MOE_BLOCK * MOE_BLOCK
    pad_end = jnp.cumsum(padded)
    pad_start = pad_end - padded
    start = jnp.cumsum(counts) - counts
    dest = inv + jnp.sum(jnp.where(onehot, (pad_start - start)[None, :], 0), -1, dtype=I32)
    n_blocks = -(-A // MOE_BLOCK) + N_EXPERTS
    blk0 = jnp.arange(n_blocks, dtype=I32) * MOE_BLOCK
    block_expert = jnp.minimum(jnp.sum(pad_end[None, :] <= blk0[:, None], -1, dtype=I32), N_EXPERTS - 1)
    n_used = (pad_end[-1] // MOE_BLOCK).astype(I32).reshape(1)
    sel = block_expert[:, None] == jnp.arange(N_EXPERTS, dtype=I32)[None, :]
    pick = lambda v: jnp.sum(jnp.where(sel, v[None, :], 0), -1, dtype=I32)[:, None]
    off = blk0[:, None] - pick(pad_start) + jnp.arange(MOE_BLOCK, dtype=I32)[None, :]
    asg = order[jnp.clip(pick(start) + off, 0, A - 1)]
    slot_token = jnp.where(off < pick(counts), asg // TOP_K, 0).reshape(n_blocks, 1, MOE_BLOCK)
    y = _expert_ffn(h, slot_token, block_expert, n_used, lw["w_gate"], lw["w_up"], lw["w_down"])
    return _combine_ln(h, y, dest, gates[:TOP_K].T, lw["ln2_g"], lw["ln2_b"])


def _trunk(x, ln_in_g, ln_in_b, layers):
    b, s, D = x.shape
    tabs = _rope_tables(s)
    h = _layer_norm(x.reshape(b * s, D), ln_in_g, ln_in_b)
    for lw in layers:
        h = _mixer(h.reshape(b, s, D), lw, tabs)
        h = _moe(h.reshape(b * s, D), lw)
    return h.reshape(b, s, D)


def _prep_layers(w_in, conv_w, conv_b, a_log, dt_bias, d_skip, ssd_norm_w, w_out, ln1_g, ln1_b,
                 router_group, router_expert, w_gate, w_up, w_down, ln2_g, ln2_b):
    layers = []
    z0 = ATT_PROJ
    x0 = ATT_PROJ + SSD_WIDTH
    d0 = x0 + CONV_DIM
    for i in range(w_in.shape[0]):
        w = w_in[i]
        w_dt = jnp.pad(w[:, d0:], ((0, 0), (0, LANES - 2 * SSD_HEADS))).astype(BF16)
        w_router = jnp.concatenate([
            router_group[i].T, jnp.zeros((8 - N_EXPERT_GROUPS, D_MODEL), F32), router_expert[i].T], 0)
        layers.append(dict(
            w_att=w[:, :z0].astype(BF16),
            w_zx=jnp.concatenate([w[:, x0:d0], w[:, z0:x0]], 1).astype(BF16),
            w_dt=w_dt,
            conv_w=conv_w[i], conv_b=conv_b[i], a_log=a_log[i], dt_bias=dt_bias[i], d_skip=d_skip[i],
            ssd_norm_w=ssd_norm_w[i], w_out=w_out[i].astype(BF16), ln1_g=ln1_g[i], ln1_b=ln1_b[i],
            w_router=w_router, w_gate=w_gate[i].astype(BF16), w_up=w_up[i].astype(BF16),
            w_down=w_down[i].astype(BF16), ln2_g=ln2_g[i], ln2_b=ln2_b[i]))
    return layers


def kernel(x_prompt, x_sample, ln_in_g, ln_in_b, w_in, conv_w, conv_b, a_log, dt_bias, d_skip, ssd_norm_w,
           w_out, ln1_g, ln1_b, router_group, router_expert, w_gate, w_up, w_down, ln2_g, ln2_b):
    layers = _prep_layers(w_in, conv_w, conv_b, a_log, dt_bias, d_skip, ssd_norm_w, w_out, ln1_g, ln1_b,
                          router_group, router_expert, w_gate, w_up, w_down, ln2_g, ln2_b)
    y_prompt = _trunk(x_prompt, ln_in_g, ln_in_b, layers)
    y_sample = _trunk(x_sample, ln_in_g, ln_in_b, layers)
    return (y_prompt, y_sample)
```

```python
import functools

import numpy as np
import jax
import jax.numpy as jnp
from jax import lax
from jax.experimental import pallas as pl
from jax.experimental.pallas import tpu as pltpu

F32 = jnp.float32
BF16 = jnp.bfloat16
I32 = jnp.int32

D_MODEL = 1024
DEPTH = 4
HEAD_DIM = 64
DILATED_PATTERNS = ((128, 1), (512, 4), (2048, 16))
N_DIL = 3
ATT_HEADS = 8
ATT_WIDTH = ATT_HEADS * HEAD_DIM
ATT_PROJ = N_DIL * 3 * ATT_WIDTH
ROPE_THETA = 10000.0
SSD_HEADS = 8
SSD_WIDTH = SSD_HEADS * HEAD_DIM
SSD_GROUPS = 2
HEADS_PER_GROUP = SSD_HEADS // SSD_GROUPS
D_STATE = 128
CONV_K = 5
CONV_DIM = SSD_WIDTH + 2 * SSD_GROUPS * D_STATE
N_EXPERT_GROUPS = 4
EXPERTS_PER_GROUP = 8
N_EXPERTS = N_EXPERT_GROUPS * EXPERTS_PER_GROUP
TOP_K = 2
EXPERT_FF = 512
MOE_BLOCK = 256
DN_ALPHA = (2 * DEPTH) ** 0.25
LN_EPS = 1e-5
RMS_EPS = 1e-5
NEG = -1e30

LANES = 128
SSD_CHUNK = 128
ATT_SUB = 128
ATT_HALO = 64
CONV_HALO = 16
NBUF = 3
VMEM_LIMIT = 48 * 1024 * 1024


def _cparams(*sem):
    return pltpu.CompilerParams(dimension_semantics=sem, vmem_limit_bytes=VMEM_LIMIT)


def _const_spec(shape):
    nd = len(shape)
    return pl.BlockSpec(shape, lambda *_: (0,) * nd)


def _ln_rows(x, g, b):
    mu = jnp.mean(x, -1, keepdims=True)
    xc = x - mu
    var = jnp.mean(xc * xc, -1, keepdims=True)
    return xc * lax.rsqrt(var + LN_EPS) * g + b


def _silu(x):
    return x / (1.0 + jnp.exp(-x))


def _ln_kernel(x_ref, g_ref, b_ref, o_ref):
    o_ref[...] = _ln_rows(x_ref[...], g_ref[...], b_ref[...])


def _layer_norm(x, g, b, tm=512):
    T, D = x.shape
    return pl.pallas_call(
        _ln_kernel,
        grid=(T // tm,),
        in_specs=[pl.BlockSpec((tm, D), lambda i: (i, 0)), _const_spec((1, D)), _const_spec((1, D))],
        out_specs=pl.BlockSpec((tm, D), lambda i: (i, 0)),
        out_shape=jax.ShapeDtypeStruct((T, D), F32),
        compiler_params=_cparams("parallel"),
        name="ln_in",
    )(x, g.reshape(1, D), b.reshape(1, D))


def _rope_chunks(y, tab):
    cos = tab[:, :LANES]
    sin = tab[:, LANES:]
    lane = lax.broadcasted_iota(I32, (1, LANES), 1)
    first = (lane % HEAD_DIM) < (HEAD_DIM // 2)
    out = []
    for c in range(ATT_WIDTH // LANES):
        tc = y[:, c * LANES:(c + 1) * LANES]
        rot = jnp.where(first, pltpu.roll(tc, LANES - HEAD_DIM // 2, 1), pltpu.roll(tc, HEAD_DIM // 2, 1))
        out.append(tc * cos + rot * sin)
    return out


def _proj_kernel(x_ref, wa_ref, wz_ref, wd_ref, t0_ref, t1_ref, t2_ref,
                 a0_ref, a1_ref, a2_ref, oz_ref, od_ref, xc_ref, xs_ref, *, tm, tn):
    xb = x_ref[0].astype(BF16)
    nlc = x_ref.shape[2] // LANES
    for c in range(nlc):
        xc_ref[c] = x_ref[0, :, c * LANES:(c + 1) * LANES]
    for j in range(oz_ref.shape[2] // tn):
        sl = slice(j * tn, (j + 1) * tn)
        oz_ref[0, :, sl] = jnp.dot(xb, wz_ref[:, sl], preferred_element_type=F32).astype(oz_ref.dtype)
    od_ref[0] = jnp.dot(xb, wd_ref[...], preferred_element_type=F32)
    scale = HEAD_DIM ** -0.5
    nchunk = ATT_WIDTH // LANES
    for g, (t_ref, a_ref) in enumerate(((t0_ref, a0_ref), (t1_ref, a1_ref), (t2_ref, a2_ref))):
        dil = DILATED_PATTERNS[g][1]
        n = tm // dil
        if dil == 1:
            xp = xb
        else:
            for r in range(dil):
                for c in range(nlc):
                    xs_ref[r * n:(r + 1) * n, c * LANES:(c + 1) * LANES] = (
                        xc_ref[c, pl.ds(r, n, stride=dil), :].astype(BF16))
            xp = xs_ref[...]
        tab = t_ref[...].reshape(tm, 2 * LANES)
        for j in range(3):
            c0 = (g * 3 + j) * ATT_WIDTH
            y = jnp.dot(xp, wa_ref[:, c0:c0 + ATT_WIDTH], preferred_element_type=F32)
            if j < 2:
                chunks = _rope_chunks(y, tab)
                if j == 0:
                    chunks = [ch * scale for ch in chunks]
            else:
                chunks = [y[:, c * LANES:(c + 1) * LANES] for c in range(nchunk)]
            for c, ch in enumerate(chunks):
                chb = ch.astype(BF16)
                col = j * ATT_WIDTH + c * LANES
                for r in range(dil):
                    a_ref[0, r, :, col:col + LANES] = chb[r * n:(r + 1) * n]


def _in_proj(h, w_att, w_zx, w_dt, tabs, tm=512, tn=512):
    b, s, D = h.shape
    dils = [d for _, d in DILATED_PATTERNS]
    qkv = 3 * ATT_WIDTH
    nz, nd = w_zx.shape[1], w_dt.shape[1]
    row = lambda n: pl.BlockSpec((1, tm, n), lambda bi, i: (bi, i, 0))
    sub = lambda d, n: pl.BlockSpec((1, d, tm // d, n), lambda bi, i: (bi, 0, i, 0))
    return pl.pallas_call(
        functools.partial(_proj_kernel, tm=tm, tn=tn),
        grid=(b, s // tm),
        in_specs=[row(D)] + [_const_spec(w.shape) for w in (w_att, w_zx, w_dt)]
        + [pl.BlockSpec((d, tm // d, 2 * LANES), lambda bi, i: (0, i, 0)) for d in dils],
        out_specs=[sub(d, qkv) for d in dils] + [row(nz), row(nd)],
        out_shape=[jax.ShapeDtypeStruct((b, d, s // d, qkv), BF16) for d in dils]
        + [jax.ShapeDtypeStruct((b, s, nz), BF16), jax.ShapeDtypeStruct((b, s, nd), F32)],
        scratch_shapes=[pltpu.VMEM((D // LANES, tm, LANES), F32), pltpu.VMEM((tm, D), BF16)],
        compiler_params=_cparams("parallel", "parallel"),
        name="in_proj",
    )(h, w_att, w_zx, w_dt, *tabs)


def _attn_kernel(q_ref, kc_ref, kp_ref, kn_ref, vc_ref, vp_ref, vn_ref, o_ref, lse_ref, kbuf, vbuf,
                 *, bq, seq_len, half):
    qi = pl.program_id(2)
    kbuf[0:ATT_HALO] = kp_ref[0, 0]
    kbuf[ATT_HALO:ATT_HALO + bq] = kc_ref[0, 0]
    kbuf[ATT_HALO + bq:] = kn_ref[0, 0]
    vbuf[0:ATT_HALO] = vp_ref[0, 0]
    vbuf[ATT_HALO:ATT_HALO + bq] = vc_ref[0, 0]
    vbuf[ATT_HALO + bq:] = vn_ref[0, 0]

    nk = ATT_SUB + 2 * ATT_HALO
    npair = ATT_HEADS // 2
    ri = lax.broadcasted_iota(I32, (ATT_SUB, nk), 0)
    ci = lax.broadcasted_iota(I32, (ATT_SUB, nk), 1)
    band = jnp.abs(ci - ATT_HALO - ri) <= half
    krow = lax.broadcasted_iota(I32, (1, nk), 1)
    lane = lax.broadcasted_iota(I32, (1, LANES), 1)
    even = lane < HEAD_DIM
    for sb in range(bq // ATT_SUB):
        r0 = sb * ATT_SUB
        kpos = qi * bq + r0 - ATT_HALO + krow
        mask = band & ((kpos >= 0) & (kpos < seq_len))
        ss = []
        for j in range(npair):
            cs = slice(j * LANES, (j + 1) * LANES)
            qp = q_ref[0, 0, r0:r0 + ATT_SUB, cs]
            zero = jnp.zeros_like(qp)
            lhs = jnp.concatenate([jnp.where(even, qp, zero), jnp.where(even, zero, qp)], axis=0)
            ss.append(lax.dot_general(lhs, kbuf[r0:r0 + nk, cs], (((1,), (1,)), ((), ())),
                                      preferred_element_type=F32))
        s = jnp.stack(ss).reshape(ATT_HEADS, ATT_SUB, nk)
        s = jnp.where(mask[None], s, NEG)
        m = jnp.max(s, -1, keepdims=True)
        p = jnp.exp(s - m)
        l = jnp.sum(p, -1, keepdims=True)
        pb = p.astype(BF16).reshape(npair, 2 * ATT_SUB, nk)
        l2 = l.reshape(npair, 2 * ATT_SUB, 1)
        for j in range(npair):
            cs = slice(j * LANES, (j + 1) * LANES)
            o2 = jnp.dot(pb[j], vbuf[r0:r0 + nk, cs], preferred_element_type=F32) / l2[j]
            o_ref[0, 0, r0:r0 + ATT_SUB, cs] = jnp.where(even, o2[:ATT_SUB], o2[ATT_SUB:]).astype(o_ref.dtype)
        m_tile = jnp.zeros((ATT_SUB, LANES), F32)
        l_tile = jnp.ones((ATT_SUB, LANES), F32)
        for h in range(ATT_HEADS):
            m_tile = jnp.where(lane == h, m[h], m_tile)
            l_tile = jnp.where(lane == h, l[h], l_tile)
        lse_ref[0, 0, r0:r0 + ATT_SUB, :] = m_tile + jnp.log(l_tile)


def _dilated_attention(att, gi, bq=512):
    win, dil = DILATED_PATTERNS[gi]
    half = win // (2 * dil)
    assert half <= ATT_HALO
    b, _, L, _ = att.shape
    bq = min(bq, L)
    nq = L // bq
    hb = bq // ATT_HALO
    nhb = L // ATT_HALO
    cur = lambda j: (lambda bi, r, qi: (bi, r, qi, j))
    prev = lambda j: (lambda bi, r, qi: (bi, r, jnp.maximum(qi * hb - 1, 0), j))
    nxt = lambda j: (lambda bi, r, qi: (bi, r, jnp.minimum((qi + 1) * hb, nhb - 1), j))
    blk = lambda n, f: pl.BlockSpec((1, 1, n, ATT_WIDTH), f)
    return pl.pallas_call(
        functools.partial(_attn_kernel, bq=bq, seq_len=L, half=half),
        grid=(b, dil, nq),
        in_specs=[blk(bq, cur(0)),
                  blk(bq, cur(1)), blk(ATT_HALO, prev(1)), blk(ATT_HALO, nxt(1)),
                  blk(bq, cur(2)), blk(ATT_HALO, prev(2)), blk(ATT_HALO, nxt(2))],
        out_specs=[pl.BlockSpec((1, 1, bq, ATT_WIDTH), lambda bi, r, qi: (bi, r, qi, 0)),
                   pl.BlockSpec((1, 1, bq, LANES), lambda bi, r, qi: (bi, r, qi, 0))],
        out_shape=[jax.ShapeDtypeStruct((b, dil, L, ATT_WIDTH), BF16),
                   jax.ShapeDtypeStruct((b, dil, L, LANES), F32)],
        scratch_shapes=[pltpu.VMEM((bq + 2 * ATT_HALO, ATT_WIDTH), BF16),
                        pltpu.VMEM((bq + 2 * ATT_HALO, ATT_WIDTH), BF16)],
        compiler_params=_cparams("parallel", "parallel", "parallel"),
        name=f"dil_attn_{gi}",
    )(att, att, att, att, att, att, att)


def _rope_tables(s):
    half = HEAD_DIM // 2
    inv = ROPE_THETA ** (-jnp.arange(half, dtype=F32) / half)
    ang = jnp.arange(s).astype(F32)[:, None] * inv[None, :]
    cos, sin = jnp.cos(ang), jnp.sin(ang)
    cos_h = jnp.concatenate([cos, cos], -1)
    sin_h = jnp.concatenate([-sin, sin], -1)
    rep = LANES // HEAD_DIM
    tab = jnp.concatenate([jnp.tile(cos_h, (1, rep)), jnp.tile(sin_h, (1, rep))], -1)
    return [tab.reshape(s // d, d, 2 * LANES).transpose(1, 0, 2) for _, d in DILATED_PATTERNS]


def _conv_kernel(c_ref, p_ref, n_ref, w_ref, b_ref, o_ref, buf, *, ts):
    i = pl.program_id(1)
    last = pl.num_programs(1) - 1
    pad = (CONV_K - 1) // 2
    buf[0:CONV_HALO] = jnp.where(i > 0, p_ref[0].astype(F32), 0.0)
    buf[CONV_HALO:CONV_HALO + ts] = c_ref[0].astype(F32)
    buf[CONV_HALO + ts:] = jnp.where(i < last, n_ref[0].astype(F32), 0.0)
    acc = jnp.zeros((ts, CONV_DIM), F32) + b_ref[...]
    for j in range(CONV_K):
        acc = acc + w_ref[j:j + 1, :] * buf[CONV_HALO - pad + j:CONV_HALO - pad + j + ts, :]
    o_ref[0] = _silu(acc).astype(o_ref.dtype)


def _conv_silu(zx_v, conv_w, conv_b, ts=512):
    b, s, _ = zx_v.shape
    ts = min(ts, s)
    r = ts // CONV_HALO
    nh = s // CONV_HALO
    w8 = jnp.zeros((8, CONV_DIM), F32).at[:CONV_K].set(conv_w)
    return pl.pallas_call(
        functools.partial(_conv_kernel, ts=ts),
        grid=(b, s // ts),
        in_specs=[pl.BlockSpec((1, ts, CONV_DIM), lambda bi, i: (bi, i, 0)),
                  pl.BlockSpec((1, CONV_HALO, CONV_DIM), lambda bi, i: (bi, jnp.maximum(i * r - 1, 0), 0)),
                  pl.BlockSpec((1, CONV_HALO, CONV_DIM), lambda bi, i: (bi, jnp.minimum((i + 1) * r, nh - 1), 0)),
                  _const_spec((8, CONV_DIM)), _const_spec((1, CONV_DIM))],
        out_specs=pl.BlockSpec((1, ts, CONV_DIM), lambda bi, i: (bi, i, 0)),
        out_shape=jax.ShapeDtypeStruct((b, s, CONV_DIM), BF16),
        scratch_shapes=[pltpu.VMEM((ts + 2 * CONV_HALO, CONV_DIM), F32)],
        compiler_params=_cparams("parallel", "parallel"),
        name="conv_silu",
    )(zx_v, zx_v, zx_v, w8, conv_b.reshape(1, CONV_DIM))


def _expand_heads(v, off):
    head = lax.broadcasted_iota(I32, (1, SSD_WIDTH), 1) // HEAD_DIM
    out = jnp.zeros((v.shape[0], SSD_WIDTH), F32)
    for h in range(SSD_HEADS):
        out = jnp.where(head == h, v[:, off + h:off + h + 1], out)
    return out


def _head_selector(off):
    r = lax.broadcasted_iota(I32, (LANES, SSD_WIDTH), 0)
    c = lax.broadcasted_iota(I32, (LANES, SSD_WIDTH), 1)
    return (r == c // HEAD_DIM + off).astype(BF16)


def _expand_heads_mxu(v, sel):
    hi = v.astype(BF16)
    lo = (v - hi.astype(F32)).astype(BF16)
    return (jnp.dot(hi, sel, preferred_element_type=F32) + jnp.dot(lo, sel, preferred_element_type=F32))


def _softplus(x):
    return jnp.maximum(x, 0.0) + jnp.log(1.0 + jnp.exp(-jnp.abs(x)))


def _ssd_chunk(xc, dtr, bias, a_row, state_ref, *, reverse, off):
    Q = SSD_CHUNK
    dt = _softplus(dtr + bias)
    a = dt * a_row
    ri = lax.broadcasted_iota(I32, (Q, Q), 0)
    ci = lax.broadcasted_iota(I32, (Q, Q), 1)
    keep = (ci >= ri) if reverse else (ci <= ri)
    tri = keep.astype(BF16)
    a0 = a.astype(BF16)
    a1 = (a - a0.astype(F32)).astype(BF16)
    a2 = (a - a0.astype(F32) - a1.astype(F32)).astype(BF16)
    cum = (jnp.dot(tri, a0, preferred_element_type=F32) + jnp.dot(tri, a1, preferred_element_type=F32)
           + jnp.dot(tri, a2, preferred_element_type=F32))
    cum_t = cum.T
    edge = 0 if reverse else Q - 1
    tot = cum[edge:edge + 1, :]
    sel = _head_selector(off)
    dt512 = _expand_heads_mxu(dt, sel)
    dec512 = _expand_heads_mxu(jnp.exp(tot - cum), sel)
    ecum512 = _expand_heads_mxu(jnp.exp(cum), sel)
    etot512 = _expand_heads(jnp.exp(tot), off)
    xs = xc[:, :SSD_WIDTH].astype(F32)
    xdt = xs * dt512
    xdt_b = xdt.astype(BF16)
    xdd_b = (xdt * dec512).astype(BF16)
    gw = HEADS_PER_GROUP * HEAD_DIM
    lane = lax.broadcasted_iota(I32, (1, LANES), 1)
    even = lane < HEAD_DIM
    ys = []
    for g in range(SSD_GROUPS):
        bg = xc[:, SSD_WIDTH + g * D_STATE:SSD_WIDTH + (g + 1) * D_STATE]
        cg = xc[:, SSD_WIDTH + (SSD_GROUPS + g) * D_STATE:SSD_WIDTH + (SSD_GROUPS + g + 1) * D_STATE]
        cb = lax.dot_general(cg, bg, (((1,), (1,)), ((), ())), preferred_element_type=F32)
        sg = state_ref[g]
        yoff = jnp.dot(cg, sg.astype(BF16), preferred_element_type=F32)
        for pr in range(HEADS_PER_GROUP // 2):
            ms = []
            for hh in (2 * pr, 2 * pr + 1):
                ln = off + g * HEADS_PER_GROUP + hh
                seg = cum[:, ln:ln + 1] - cum_t[ln:ln + 1, :]
                lmat = jnp.where(keep, jnp.exp(jnp.where(keep, seg, 0.0)), 0.0)
                ms.append((cb * lmat).astype(BF16))
            c0 = g * gw + pr * LANES
            yd2 = jnp.dot(jnp.concatenate(ms, axis=0), xdt_b[:, c0:c0 + LANES], preferred_element_type=F32)
            yd = jnp.where(even, yd2[:Q], yd2[Q:])
            ys.append(yd + yoff[:, pr * LANES:(pr + 1) * LANES] * ecum512[:, c0:c0 + LANES])
        bg_t = bg.astype(F32).T.astype(BF16)
        state_ref[g] = sg * etot512[:, g * gw:(g + 1) * gw] + jnp.dot(
            bg_t, xdd_b[:, g * gw:(g + 1) * gw], preferred_element_type=F32)
    return ys


def _ssd_fwd_kernel(x_ref, dt_ref, bias_ref, a_ref, y_ref, state_ref, *, nch):
    @pl.when(pl.program_id(1) == 0)
    def _():
        state_ref[...] = jnp.zeros_like(state_ref)

    for c in range(nch):
        rows = slice(c * SSD_CHUNK, (c + 1) * SSD_CHUNK)
        ys = _ssd_chunk(x_ref[0, rows, :], dt_ref[0, rows, :], bias_ref[...], a_ref[...], state_ref,
                        reverse=False, off=0)
        for j, y in enumerate(ys):
            y_ref[0, rows, j * LANES:(j + 1) * LANES] = y


def _ssd_bwd_kernel(x_ref, dt_ref, bias_ref, a_ref, yf_ref, z_ref, dskip_ref, nw_ref, o_ref, state_ref,
                    ybuf, *, nch):
    @pl.when(pl.program_id(1) == 0)
    def _():
        state_ref[...] = jnp.zeros_like(state_ref)

    for c in reversed(range(nch)):
        rows = slice(c * SSD_CHUNK, (c + 1) * SSD_CHUNK)
        ys = _ssd_chunk(x_ref[0, rows, :], dt_ref[0, rows, :], bias_ref[...], a_ref[...], state_ref,
                        reverse=True, off=SSD_HEADS)
        for j, y in enumerate(ys):
            ybuf[rows, j * LANES:(j + 1) * LANES] = y
    xs = x_ref[0, :, :SSD_WIDTH].astype(F32)
    y = yf_ref[0] + ybuf[...] + xs * dskip_ref[...]
    gy = y * _silu(z_ref[0].astype(F32))
    gw = SSD_WIDTH // SSD_GROUPS
    for g in range(SSD_GROUPS):
        part = gy[:, g * gw:(g + 1) * gw]
        ms = jnp.mean(part * part, -1, keepdims=True)
        o_ref[0, :, g * gw:(g + 1) * gw] = (part * lax.rsqrt(ms + RMS_EPS) * nw_ref[:, g * gw:(g + 1) * gw]
                                             ).astype(o_ref.dtype)


def _ssd(xc, dt_v, zx_v, dt_bias, a_log, d_skip, norm_w, nch=4):
    b, s, _ = xc.shape
    nch = min(nch, s // SSD_CHUNK)
    R = nch * SSD_CHUNK
    n = s // R
    a_neg = -jnp.exp(a_log.astype(F32))
    pad = LANES - 2 * SSD_HEADS
    bias = jnp.pad(dt_bias.astype(F32).reshape(1, 2 * SSD_HEADS), ((0, 0), (0, pad)))
    a_f = jnp.pad(a_neg[0].reshape(1, SSD_HEADS), ((0, 0), (0, LANES - SSD_HEADS)))
    a_b = jnp.pad(a_neg[1].reshape(1, SSD_HEADS), ((0, 0), (SSD_HEADS, pad)))
    dskip = jnp.repeat(d_skip.astype(F32), HEAD_DIM).reshape(1, SSD_WIDTH)
    state = pltpu.VMEM((SSD_GROUPS, D_STATE, HEADS_PER_GROUP * HEAD_DIM), F32)
    fwd = lambda bi, i: (bi, i, 0)
    rev = lambda bi, i: (bi, n - 1 - i, 0)
    y_f = pl.pallas_call(
        functools.partial(_ssd_fwd_kernel, nch=nch),
        grid=(b, n),
        in_specs=[pl.BlockSpec((1, R, CONV_DIM), fwd), pl.BlockSpec((1, R, LANES), fwd),
                  _const_spec((1, LANES)), _const_spec((1, LANES))],
        out_specs=pl.BlockSpec((1, R, SSD_WIDTH), fwd),
        out_shape=jax.ShapeDtypeStruct((b, s, SSD_WIDTH), F32),
        scratch_shapes=[state],
        compiler_params=_cparams("parallel", "arbitrary"),
        name="ssd_fwd",
    )(xc, dt_v, bias, a_f)
    out = pl.pallas_call(
        functools.partial(_ssd_bwd_kernel, nch=nch),
        grid=(b, n),
        in_specs=[pl.BlockSpec((1, R, CONV_DIM), rev), pl.BlockSpec((1, R, LANES), rev),
                  _const_spec((1, LANES)), _const_spec((1, LANES)),
                  pl.BlockSpec((1, R, SSD_WIDTH), rev),
                  pl.BlockSpec((1, R, SSD_WIDTH), lambda bi, i: (bi, n - 1 - i, CONV_DIM // SSD_WIDTH)),
                  _const_spec((1, SSD_WIDTH)), _const_spec((1, SSD_WIDTH))],
        out_specs=pl.BlockSpec((1, R, SSD_WIDTH), rev),
        out_shape=jax.ShapeDtypeStruct((b, s, SSD_WIDTH), BF16),
        scratch_shapes=[state, pltpu.VMEM((R, SSD_WIDTH), F32)],
        compiler_params=_cparams("parallel", "arbitrary"),
        name="ssd_bwd",
    )(xc, dt_v, bias, a_b, y_f, zx_v, dskip, norm_w.astype(F32).reshape(1, SSD_WIDTH))
    return out


def _natural_order(src_ref, scr_ref):
    dil, n, w = src_ref.shape[1:]
    if dil == 1:
        return src_ref[0, 0].astype(F32)
    for r in range(dil):
        for c in range(w // LANES):
            scr_ref[c, pl.ds(r, n, stride=dil), :] = src_ref[0, r, :, c * LANES:(c + 1) * LANES].astype(F32)
    return jnp.concatenate([scr_ref[c] for c in range(w // LANES)], axis=1)


def _out_proj_kernel(ssd_ref, o0_ref, o1_ref, o2_ref, l0_ref, l1_ref, l2_ref, h_ref, w1_ref, w2_ref,
                     g_ref, b_ref, wr_ref, out_ref, id_ref, gate_ref, so1, so2, sl1, sl2):
    lses = [_natural_order(r, s) for r, s in ((l0_ref, None), (l1_ref, sl1), (l2_ref, sl2))]
    mx = jnp.maximum(jnp.maximum(lses[0], lses[1]), lses[2])
    es = [jnp.exp(l - mx) for l in lses]
    den = es[0] + es[1] + es[2]
    sel = _head_selector(0)
    att = jnp.zeros((h_ref.shape[1], ATT_WIDTH), F32)
    for e, o_ref, scr in zip(es, (o0_ref, o1_ref, o2_ref), (None, so1, so2)):
        att = att + _expand_heads_mxu(e / den, sel) * _natural_order(o_ref, scr)
    y = jnp.dot(ssd_ref[0], w1_ref[...], preferred_element_type=F32)
    y = y + jnp.dot(att.astype(BF16), w2_ref[...], preferred_element_type=F32)
    hn = _ln_rows(DN_ALPHA * h_ref[0] + y, g_ref[...], b_ref[...])
    out_ref[0] = hn
    id_ref[...], gate_ref[...] = _route(hn, wr_ref)


def _out_proj_ln(ssd, os_, lses, h, w_out, g, b, w_router, tm=512):
    bsz, s, D = h.shape
    w0 = w_router.astype(BF16)
    w_router = jnp.concatenate([w0, (w_router - w0.astype(F32)).astype(BF16)], 0)
    n = s // tm
    route = pl.BlockSpec((8, tm), lambda bi, i: (0, bi * n + i))
    dils = [d for _, d in DILATED_PATTERNS]
    row = lambda n: pl.BlockSpec((1, tm, n), lambda bi, i: (bi, i, 0))
    sub = lambda d, n: pl.BlockSpec((1, d, tm // d, n), lambda bi, i: (bi, 0, i, 0))
    w1 = w_out[:SSD_WIDTH]
    w2 = w_out[SSD_WIDTH:]
    return pl.pallas_call(
        _out_proj_kernel,
        grid=(bsz, s // tm),
        in_specs=[row(SSD_WIDTH)] + [sub(d, ATT_WIDTH) for d in dils] + [sub(d, LANES) for d in dils]
        + [row(D), _const_spec(w1.shape), _const_spec(w2.shape), _const_spec((1, D)), _const_spec((1, D)),
           _const_spec(w_router.shape)],
        out_specs=[row(D), route, route],
        out_shape=[jax.ShapeDtypeStruct((bsz, s, D), F32), jax.ShapeDtypeStruct((8, bsz * s), I32),
                   jax.ShapeDtypeStruct((8, bsz * s), F32)],
        scratch_shapes=[pltpu.VMEM((ATT_WIDTH // LANES, tm, LANES), F32)] * 2
        + [pltpu.VMEM((1, tm, LANES), F32)] * 2,
        compiler_params=_cparams("parallel", "parallel"),
        name="out_proj_ln",
    )(ssd, *os_, *lses, h, w1, w2, g.reshape(1, D), b.reshape(1, D), w_router)


def _route(h, w_ref):
    nt = (((1,), (1,)), ((), ()))
    h0 = h.astype(BF16)
    h1 = (h - h0.astype(F32)).astype(BF16)
    nr = w_ref.shape[0] // 2
    r0 = lax.dot_general(w_ref[...], h0, nt, preferred_element_type=F32)
    r1 = lax.dot_general(w_ref[0:nr, :], h1, nt, preferred_element_type=F32)
    logits = r0[0:nr] + r0[nr:] + r1
    tm = logits.shape[1]
    row = lax.broadcasted_iota(I32, (8, tm), 0)
    lg = jnp.where(row < N_EXPERT_GROUPS, logits[0:8], NEG)
    gm = jnp.max(lg, 0, keepdims=True)
    gs = jnp.sum(jnp.exp(lg - gm), 0, keepdims=True)
    g_idx = jnp.min(jnp.where(lg == gm, row, 8), 0, keepdims=True)
    g_prob = 1.0 / gs
    el = jnp.zeros((8, tm), F32)
    for g in range(N_EXPERT_GROUPS):
        el = jnp.where(g_idx == g, logits[8 + 8 * g:16 + 8 * g], el)
    em = jnp.max(el, 0, keepdims=True)
    ee = jnp.exp(el - em)
    p = ee / jnp.sum(ee, 0, keepdims=True)
    p1 = jnp.max(p, 0, keepdims=True)
    i1 = jnp.min(jnp.where(p == p1, row, 8), 0, keepdims=True)
    pr = jnp.where(row == i1, -1.0, p)
    p2 = jnp.max(pr, 0, keepdims=True)
    i2 = jnp.min(jnp.where(pr == p2, row, 8), 0, keepdims=True)
    den = p1 + p2
    base = g_idx * EXPERTS_PER_GROUP
    ids = jnp.where(row == 0, base + i1, jnp.where(row == 1, base + i2, 0))
    gates = jnp.where(row == 0, g_prob * p1 / den, jnp.where(row == 1, g_prob * p2 / den, 0.0))
    return ids, gates


def _expert_kernel(be_ref, nu_ref, tok0_ref, tok1_ref, tok2_ref, h_hbm, wg_ref, wu_ref, wd_ref, o_ref,
                   xbuf, gsem):
    i = pl.program_id(0)
    nu = nu_ref[0]

    def gather_row(tok_ref, s, j, q=0):
        pltpu.async_copy(h_hbm.at[pl.ds(tok_ref[0, 0, j], 1)], xbuf.at[s, pl.ds(j, 1)], gsem.at[s], priority=q)

    def gather_wait(s):
        pltpu.make_async_copy(h_hbm.at[pl.ds(0, MOE_BLOCK)], xbuf.at[s], gsem.at[s]).wait()

    def rolled(ref, s):
        def body(j, c):
            gather_row(ref, s, j)
            return c
        lax.fori_loop(0, MOE_BLOCK, body, 0, unroll=8)

    @pl.when(i == 0)
    def _():
        rolled(tok0_ref, 0)
        rolled(tok1_ref, 1)

    @pl.when(i >= nu)
    def _():
        o_ref[...] = jnp.zeros(o_ref.shape, F32)

    def step(cur):
        nxt, nx2 = (cur + 1) % NBUF, (cur + 2) % NBUF
        gather_wait(cur)

        def issue(part, nparts=4):
            n = MOE_BLOCK // nparts
            for j in range(part * n, (part + 1) * n):
                gather_row(tok2_ref, nx2, j, j % 2)

        x = xbuf[cur].astype(BF16)
        issue(0)
        hg = jnp.dot(x, wg_ref[0], preferred_element_type=F32)
        issue(1)
        hu = jnp.dot(x, wu_ref[0], preferred_element_type=F32)
        issue(2)
        hdn = (_silu(hg) * hu).astype(BF16)
        issue(3)
        o_ref[...] = jnp.dot(hdn, wd_ref[0], preferred_element_type=F32)

        @pl.when(i == nu - 1)
        def _():
            gather_wait(nxt)
            gather_wait(nx2)

    for k in range(NBUF):
        pl.when((i < nu) & (i % NBUF == k))(functools.partial(step, k))


def _expert_ffn(h, slot_token, block_expert, n_used, w_gate, w_up, w_down):
    T, D = h.shape
    nb = slot_token.shape[0]
    smem = lambda f: pl.BlockSpec((1, 1, MOE_BLOCK), f, memory_space=pltpu.SMEM)
    ahead = lambda k: (lambda i, be, nu: (jnp.minimum(i + k, nb - 1), 0, 0))
    grid_spec = pltpu.PrefetchScalarGridSpec(
        num_scalar_prefetch=2,
        grid=(nb,),
        in_specs=[smem(ahead(0)), smem(ahead(1)), smem(ahead(2)),
                  pl.BlockSpec(memory_space=pl.ANY),
                  pl.BlockSpec((1, D, EXPERT_FF), lambda i, be, nu: (be[i], 0, 0)),
                  pl.BlockSpec((1, D, EXPERT_FF), lambda i, be, nu: (be[i], 0, 0)),
                  pl.BlockSpec((1, EXPERT_FF, D), lambda i, be, nu: (be[i], 0, 0))],
        out_specs=pl.BlockSpec((MOE_BLOCK, D), lambda i, be, nu: (i, 0)),
        scratch_shapes=[pltpu.VMEM((NBUF, MOE_BLOCK, D), F32), pltpu.SemaphoreType.DMA((NBUF,))],
    )
    return pl.pallas_call(
        _expert_kernel,
        grid_spec=grid_spec,
        out_shape=jax.ShapeDtypeStruct((nb * MOE_BLOCK, D), F32),
        compiler_params=_cparams("arbitrary"),
        name="expert_ffn",
    )(block_expert, n_used, slot_token, slot_token, slot_token, h, w_gate, w_up, w_down)


def _combine_kernel(dc_ref, dn_ref, h_ref, gate_ref, g_ref, b_ref, y_hbm, o_ref, ybuf, sem, *, tm):
    i = pl.program_id(0)
    last = pl.num_programs(0) - 1

    def row(d_ref, s, t, k):
        pltpu.async_copy(y_hbm.at[pl.ds(d_ref[0, 0, TOP_K * t + k], 1)], ybuf.at[s, k, pl.ds(t, 1)],
                         sem.at[s], priority=k)

    def wait(s):
        for k in range(TOP_K):
            pltpu.make_async_copy(y_hbm.at[pl.ds(0, tm)], ybuf.at[s, k], sem.at[s]).wait()

    @pl.when(i == 0)
    def _():
        def body(t, c):
            for k in range(TOP_K):
                row(dc_ref, 0, t, k)
            return c
        lax.fori_loop(0, tm, body, 0, unroll=8)

    def step(cur):
        wait(cur)
        for t in range(tm):
            for k in range(TOP_K):
                row(dn_ref, 1 - cur, t, k)
        gt = gate_ref[...]
        ffn = ybuf[cur, 0] * gt[:, 0:1] + ybuf[cur, 1] * gt[:, 1:2]
        o_ref[...] = _ln_rows(DN_ALPHA * h_ref[...] + ffn, g_ref[...], b_ref[...])

        @pl.when(i == last)
        def _():
            wait(1 - cur)

    for s in range(2):
        pl.when(i % 2 == s)(functools.partial(step, s))


def _combine_ln(h, y, dest, gates, g, b, tm=512):
    T, D = h.shape
    n = T // tm
    dest = dest.reshape(n, 1, TOP_K * tm)
    smem = lambda f: pl.BlockSpec((1, 1, TOP_K * tm), f, memory_space=pltpu.SMEM)
    return pl.pallas_call(
        functools.partial(_combine_kernel, tm=tm),
        grid=(n,),
        in_specs=[smem(lambda i: (i, 0, 0)), smem(lambda i: (jnp.minimum(i + 1, n - 1), 0, 0)),
                  pl.BlockSpec((tm, D), lambda i: (i, 0)), pl.BlockSpec((tm, TOP_K), lambda i: (i, 0)),
                  _const_spec((1, D)), _const_spec((1, D)), pl.BlockSpec(memory_space=pl.ANY)],
        out_specs=pl.BlockSpec((tm, D), lambda i: (i, 0)),
        out_shape=jax.ShapeDtypeStruct((T, D), F32),
        scratch_shapes=[pltpu.VMEM((2, TOP_K, tm, D), F32), pltpu.SemaphoreType.DMA((2,))],
        compiler_params=_cparams("arbitrary"),
        name="combine_ln",
    )(dest, dest, h, gates, g.reshape(1, D), b.reshape(1, D), y)


def _mixer(h, lw, tabs):
    *atts, zx, dt_raw = _in_proj(h, lw["w_att"], lw["w_zx"], lw["w_dt"], tabs)
    os_, lses = [], []
    for gi in range(N_DIL):
        o, lse = _dilated_attention(atts[gi], gi)
        os_.append(o)
        lses.append(lse)
    xc = _conv_silu(zx, lw["conv_w"], lw["conv_b"])
    ssd = _ssd(xc, dt_raw, zx, lw["dt_bias"], lw["a_log"], lw["d_skip"], lw["ssd_norm_w"])
    return _out_proj_ln(ssd, os_, lses, h, lw["w_out"], lw["ln1_g"], lw["ln1_b"], lw["w_router"])


def _moe(h, ids, gates, lw):
    T, D = h.shape
    A = T * TOP_K
    expert = ids[:TOP_K].T.reshape(A)
    order = jnp.argsort(expert, stable=True).astype(I32)
    inv = jnp.argsort(order).astype(I32)
    onehot = expert[:, None] == jnp.arange(N_EXPERTS, dtype=I32)[None, :]
    counts = jnp.sum(onehot, 0, dtype=I32)
    padded = (counts + MOE_BLOCK - 1) // MOE_BLOCK * MOE_BLOCK
    pad_end = jnp.cumsum(padded)
    pad_start = pad_end - padded
    start = jnp.cumsum(counts) - counts
    dest = inv + jnp.sum(jnp.where(onehot, (pad_start - start)[None, :], 0), -1, dtype=I32)
    n_blocks = -(-A // MOE_BLOCK) + N_EXPERTS
    blk0 = jnp.arange(n_blocks, dtype=I32) * MOE_BLOCK
    block_expert = jnp.minimum(jnp.sum(pad_end[None, :] <= blk0[:, None], -1, dtype=I32), N_EXPERTS - 1)
    n_used = (pad_end[-1] // MOE_BLOCK).astype(I32).reshape(1)
    sel = block_expert[:, None] == jnp.arange(N_EXPERTS, dtype=I32)[None, :]
    pick = lambda v: jnp.sum(jnp.where(sel, v[None, :], 0), -1, dtype=I32)[:, None]
    off = blk0[:, None] - pick(pad_start) + jnp.arange(MOE_BLOCK, dtype=I32)[None, :]
    asg = order[jnp.clip(pick(start) + off, 0, A - 1)]
    slot_token = jnp.where(off < pick(counts), asg // TOP_K, 0).reshape(n_blocks, 1, MOE_BLOCK)
    y = _expert_ffn(h, slot_token, block_expert, n_used, lw["w_gate"], lw["w_up"], lw["w_down"])
    return _combine_ln(h, y, dest, gates[:TOP_K].T, lw["ln2_g"], lw["ln2_b"])


def _trunk(x, ln_in_g, ln_in_b, layers):
    b, s, D = x.shape
    tabs = _rope_tables(s)
    h = _layer_norm(x.reshape(b * s, D), ln_in_g, ln_in_b)
    for lw in layers:
        h, ids, gates = _mixer(h.reshape(b, s, D), lw, tabs)
        h = _moe(h.reshape(b * s, D), ids, gates, lw)
    return h.reshape(b, s, D)


def _prep_layers(w_in, conv_w, conv_b, a_log, dt_bias, d_skip, ssd_norm_w, w_out, ln1_g, ln1_b,
                 router_group, router_expert, w_gate, w_up, w_down, ln2_g, ln2_b):
    layers = []
    z0 = ATT_PROJ
    x0 = ATT_PROJ + SSD_WIDTH
    d0 = x0 + CONV_DIM
    for i in range(w_in.shape[0]):
        w = w_in[i]
        w_dt = jnp.pad(w[:, d0:], ((0, 0), (0, LANES - 2 * SSD_HEADS))).astype(BF16)
        w_router = jnp.concatenate([
            router_group[i].T, jnp.zeros((8 - N_EXPERT_GROUPS, D_MODEL), F32), router_expert[i].T], 0)
        layers.append(dict(
            w_att=w[:, :z0].astype(BF16),
            w_zx=jnp.concatenate([w[:, x0:d0], w[:, z0:x0]], 1).astype(BF16),
            w_dt=w_dt,
            conv_w=conv_w[i], conv_b=conv_b[i], a_log=a_log[i], dt_bias=dt_bias[i], d_skip=d_skip[i],
            ssd_norm_w=ssd_norm_w[i], w_out=w_out[i].astype(BF16), ln1_g=ln1_g[i], ln1_b=ln1_b[i],
            w_router=w_router, w_gate=w_gate[i].astype(BF16), w_up=w_up[i].astype(BF16),
            w_down=w_down[i].astype(BF16), ln2_g=ln2_g[i], ln2_b=ln2_b[i]))
    return layers


def kernel(x_prompt, x_sample, ln_in_g, ln_in_b, w_in, conv_w, conv_b, a_log, dt_bias, d_skip, ssd_norm_w,
           w_out, ln1_g, ln1_b, router_group, router_expert, w_gate, w_up, w_down, ln2_g, ln2_b):
    layers = _prep_layers(w_in, conv_w, conv_b, a_log, dt_bias, d_skip, ssd_norm_w, w_out, ln1_g, ln1_b,
                          router_group, router_expert, w_gate, w_up, w_down, ln2_g, ln2_b)
    y_prompt = _trunk(x_prompt, ln_in_g, ln_in_b, layers)
    y_sample = _trunk(x_sample, ln_in_g, ln_in_b, layers)
    return (y_prompt, y_sample)
```

```python
import functools

import numpy as np
import jax
import jax.numpy as jnp
from jax import lax
from jax.experimental import pallas as pl
from jax.experimental.pallas import tpu as pltpu

F32 = jnp.float32
BF16 = jnp.bfloat16
I32 = jnp.int32

D_MODEL = 1024
DEPTH = 4
HEAD_DIM = 64
DILATED_PATTERNS = ((128, 1), (512, 4), (2048, 16))
N_DIL = 3
ATT_HEADS = 8
ATT_WIDTH = ATT_HEADS * HEAD_DIM
ATT_PROJ = N_DIL * 3 * ATT_WIDTH
ROPE_THETA = 10000.0
SSD_HEADS = 8
SSD_WIDTH = SSD_HEADS * HEAD_DIM
SSD_GROUPS = 2
HEADS_PER_GROUP = SSD_HEADS // SSD_GROUPS
D_STATE = 128
CONV_K = 5
CONV_DIM = SSD_WIDTH + 2 * SSD_GROUPS * D_STATE
N_EXPERT_GROUPS = 4
EXPERTS_PER_GROUP = 8
N_EXPERTS = N_EXPERT_GROUPS * EXPERTS_PER_GROUP
TOP_K = 2
EXPERT_FF = 512
MOE_BLOCK = 256
DN_ALPHA = (2 * DEPTH) ** 0.25
LN_EPS = 1e-5
RMS_EPS = 1e-5
NEG = -1e30

LANES = 128
SSD_CHUNK = 128
ATT_SUB = 128
ATT_HALO = 64
CONV_HALO = 16
NBUF = 3
VMEM_LIMIT = 48 * 1024 * 1024


def _cparams(*sem):
    return pltpu.CompilerParams(dimension_semantics=sem, vmem_limit_bytes=VMEM_LIMIT)


def _const_spec(shape):
    nd = len(shape)
    return pl.BlockSpec(shape, lambda *_: (0,) * nd)


def _ln_rows(x, g, b):
    mu = jnp.mean(x, -1, keepdims=True)
    xc = x - mu
    var = jnp.mean(xc * xc, -1, keepdims=True)
    return xc * lax.rsqrt(var + LN_EPS) * g + b


def _silu(x):
    return x / (1.0 + jnp.exp(-x))


def _ln_kernel(x_ref, g_ref, b_ref, o_ref):
    o_ref[...] = _ln_rows(x_ref[...], g_ref[...], b_ref[...])


def _layer_norm(x, g, b, tm=512):
    T, D = x.shape
    return pl.pallas_call(
        _ln_kernel,
        grid=(T // tm,),
        in_specs=[pl.BlockSpec((tm, D), lambda i: (i, 0)), _const_spec((1, D)), _const_spec((1, D))],
        out_specs=pl.BlockSpec((tm, D), lambda i: (i, 0)),
        out_shape=jax.ShapeDtypeStruct((T, D), F32),
        compiler_params=_cparams("parallel"),
        name="ln_in",
    )(x, g.reshape(1, D), b.reshape(1, D))


def _rope_chunks(y, tab):
    cos = tab[:, :LANES]
    sin = tab[:, LANES:]
    lane = lax.broadcasted_iota(I32, (1, LANES), 1)
    first = (lane % HEAD_DIM) < (HEAD_DIM // 2)
    out = []
    for c in range(ATT_WIDTH // LANES):
        tc = y[:, c * LANES:(c + 1) * LANES]
        rot = jnp.where(first, pltpu.roll(tc, LANES - HEAD_DIM // 2, 1), pltpu.roll(tc, HEAD_DIM // 2, 1))
        out.append(tc * cos + rot * sin)
    return out


def _proj_kernel(x_ref, wa_ref, wz_ref, wd_ref, t0_ref, t1_ref, t2_ref,
                 a0_ref, a1_ref, a2_ref, oz_ref, od_ref, xc_ref, xs_ref, *, tm, tn):
    xb = x_ref[0].astype(BF16)
    nlc = x_ref.shape[2] // LANES
    for c in range(nlc):
        xc_ref[c] = x_ref[0, :, c * LANES:(c + 1) * LANES]
    for j in range(oz_ref.shape[2] // tn):
        sl = slice(j * tn, (j + 1) * tn)
        oz_ref[0, :, sl] = jnp.dot(xb, wz_ref[:, sl], preferred_element_type=F32).astype(oz_ref.dtype)
    od_ref[0] = jnp.dot(xb, wd_ref[...], preferred_element_type=F32)
    scale = HEAD_DIM ** -0.5
    nchunk = ATT_WIDTH // LANES
    for g, (t_ref, a_ref) in enumerate(((t0_ref, a0_ref), (t1_ref, a1_ref), (t2_ref, a2_ref))):
        dil = DILATED_PATTERNS[g][1]
        n = tm // dil
        if dil == 1:
            xp = xb
        else:
            for r in range(dil):
                for c in range(nlc):
                    xs_ref[r * n:(r + 1) * n, c * LANES:(c + 1) * LANES] = (
                        xc_ref[c, pl.ds(r, n, stride=dil), :].astype(BF16))
            xp = xs_ref[...]
        tab = t_ref[...].reshape(tm, 2 * LANES)
        for j in range(3):
            c0 = (g * 3 + j) * ATT_WIDTH
            y = jnp.dot(xp, wa_ref[:, c0:c0 + ATT_WIDTH], preferred_element_type=F32)
            if j < 2:
                chunks = _rope_chunks(y, tab)
                if j == 0:
                    chunks = [ch * scale for ch in chunks]
            else:
                chunks = [y[:, c * LANES:(c + 1) * LANES] for c in range(nchunk)]
            for c, ch in enumerate(chunks):
                chb = ch.astype(BF16)
                col = j * ATT_WIDTH + c * LANES
                for r in range(dil):
                    a_ref[0, r, :, col:col + LANES] = chb[r * n:(r + 1) * n]


def _in_proj(h, w_att, w_zx, w_dt, tabs, tm=512, tn=512):
    b, s, D = h.shape
    dils = [d for _, d in DILATED_PATTERNS]
    qkv = 3 * ATT_WIDTH
    nz, nd = w_zx.shape[1], w_dt.shape[1]
    row = lambda n: pl.BlockSpec((1, tm, n), lambda bi, i: (bi, i, 0))
    sub = lambda d, n: pl.BlockSpec((1, d, tm // d, n), lambda bi, i: (bi, 0, i, 0))
    return pl.pallas_call(
        functools.partial(_proj_kernel, tm=tm, tn=tn),
        grid=(b, s // tm),
        in_specs=[row(D)] + [_const_spec(w.shape) for w in (w_att, w_zx, w_dt)]
        + [pl.BlockSpec((d, tm // d, 2 * LANES), lambda bi, i: (0, i, 0)) for d in dils],
        out_specs=[sub(d, qkv) for d in dils] + [row(nz), row(nd)],
        out_shape=[jax.ShapeDtypeStruct((b, d, s // d, qkv), BF16) for d in dils]
        + [jax.ShapeDtypeStruct((b, s, nz), BF16), jax.ShapeDtypeStruct((b, s, nd), F32)],
        scratch_shapes=[pltpu.VMEM((D // LANES, tm, LANES), F32), pltpu.VMEM((tm, D), BF16)],
        compiler_params=_cparams("parallel", "parallel"),
        name="in_proj",
    )(h, w_att, w_zx, w_dt, *tabs)


def _attn_kernel(q_ref, kc_ref, kp_ref, kn_ref, vc_ref, vp_ref, vn_ref, o_ref, lse_ref, kbuf, vbuf,
                 *, bq, seq_len, half):
    qi = pl.program_id(2)
    kbuf[0:ATT_HALO] = kp_ref[0, 0]
    kbuf[ATT_HALO:ATT_HALO + bq] = kc_ref[0, 0]
    kbuf[ATT_HALO + bq:] = kn_ref[0, 0]
    vbuf[0:ATT_HALO] = vp_ref[0, 0]
    vbuf[ATT_HALO:ATT_HALO + bq] = vc_ref[0, 0]
    vbuf[ATT_HALO + bq:] = vn_ref[0, 0]

    nk = ATT_SUB + 2 * ATT_HALO
    npair = ATT_HEADS // 2
    ri = lax.broadcasted_iota(I32, (ATT_SUB, nk), 0)
    ci = lax.broadcasted_iota(I32, (ATT_SUB, nk), 1)
    band = jnp.abs(ci - ATT_HALO - ri) <= half
    krow = lax.broadcasted_iota(I32, (1, nk), 1)
    lane = lax.broadcasted_iota(I32, (1, LANES), 1)
    even = lane < HEAD_DIM
    for sb in range(bq // ATT_SUB):
        r0 = sb * ATT_SUB
        kpos = qi * bq + r0 - ATT_HALO + krow
        mask = band & ((kpos >= 0) & (kpos < seq_len))
        ss = []
        for j in range(npair):
            cs = slice(j * LANES, (j + 1) * LANES)
            qp = q_ref[0, 0, r0:r0 + ATT_SUB, cs]
            zero = jnp.zeros_like(qp)
            lhs = jnp.concatenate([jnp.where(even, qp, zero), jnp.where(even, zero, qp)], axis=0)
            ss.append(lax.dot_general(lhs, kbuf[r0:r0 + nk, cs], (((1,), (1,)), ((), ())),
                                      preferred_element_type=F32))
        s = jnp.stack(ss).reshape(ATT_HEADS, ATT_SUB, nk)
        s = jnp.where(mask[None], s, NEG)
        m = jnp.max(s, -1, keepdims=True)
        p = jnp.exp(s - m)
        l = jnp.sum(p, -1, keepdims=True)
        pb = p.astype(BF16).reshape(npair, 2 * ATT_SUB, nk)
        l2 = l.reshape(npair, 2 * ATT_SUB, 1)
        for j in range(npair):
            cs = slice(j * LANES, (j + 1) * LANES)
            o2 = jnp.dot(pb[j], vbuf[r0:r0 + nk, cs], preferred_element_type=F32) / l2[j]
            o_ref[0, 0, r0:r0 + ATT_SUB, cs] = jnp.where(even, o2[:ATT_SUB], o2[ATT_SUB:]).astype(o_ref.dtype)
        m_tile = jnp.zeros((ATT_SUB, LANES), F32)
        l_tile = jnp.ones((ATT_SUB, LANES), F32)
        for h in range(ATT_HEADS):
            m_tile = jnp.where(lane == h, m[h], m_tile)
            l_tile = jnp.where(lane == h, l[h], l_tile)
        lse_ref[0, 0, r0:r0 + ATT_SUB, :] = m_tile + jnp.log(l_tile)


def _dilated_attention(att, gi, bq=512):
    win, dil = DILATED_PATTERNS[gi]
    half = win // (2 * dil)
    assert half <= ATT_HALO
    b, _, L, _ = att.shape
    bq = min(bq, L)
    nq = L // bq
    hb = bq // ATT_HALO
    nhb = L // ATT_HALO
    cur = lambda j: (lambda bi, r, qi: (bi, r, qi, j))
    prev = lambda j: (lambda bi, r, qi: (bi, r, jnp.maximum(qi * hb - 1, 0), j))
    nxt = lambda j: (lambda bi, r, qi: (bi, r, jnp.minimum((qi + 1) * hb, nhb - 1), j))
    blk = lambda n, f: pl.BlockSpec((1, 1, n, ATT_WIDTH), f)
    return pl.pallas_call(
        functools.partial(_attn_kernel, bq=bq, seq_len=L, half=half),
        grid=(b, dil, nq),
        in_specs=[blk(bq, cur(0)),
                  blk(bq, cur(1)), blk(ATT_HALO, prev(1)), blk(ATT_HALO, nxt(1)),
                  blk(bq, cur(2)), blk(ATT_HALO, prev(2)), blk(ATT_HALO, nxt(2))],
        out_specs=[pl.BlockSpec((1, 1, bq, ATT_WIDTH), lambda bi, r, qi: (bi, r, qi, 0)),
                   pl.BlockSpec((1, 1, bq, LANES), lambda bi, r, qi: (bi, r, qi, 0))],
        out_shape=[jax.ShapeDtypeStruct((b, dil, L, ATT_WIDTH), BF16),
                   jax.ShapeDtypeStruct((b, dil, L, LANES), F32)],
        scratch_shapes=[pltpu.VMEM((bq + 2 * ATT_HALO, ATT_WIDTH), BF16),
                        pltpu.VMEM((bq + 2 * ATT_HALO, ATT_WIDTH), BF16)],
        compiler_params=_cparams("parallel", "parallel", "parallel"),
        name=f"dil_attn_{gi}",
    )(att, att, att, att, att, att, att)


def _rope_tables(s):
    half = HEAD_DIM // 2
    inv = ROPE_THETA ** (-jnp.arange(half, dtype=F32) / half)
    ang = jnp.arange(s).astype(F32)[:, None] * inv[None, :]
    cos, sin = jnp.cos(ang), jnp.sin(ang)
    cos_h = jnp.concatenate([cos, cos], -1)
    sin_h = jnp.concatenate([-sin, sin], -1)
    rep = LANES // HEAD_DIM
    tab = jnp.concatenate([jnp.tile(cos_h, (1, rep)), jnp.tile(sin_h, (1, rep))], -1)
    return [tab.reshape(s // d, d, 2 * LANES).transpose(1, 0, 2) for _, d in DILATED_PATTERNS]


def _conv_kernel(c_ref, p_ref, n_ref, w_ref, b_ref, o_ref, buf, *, ts):
    i = pl.program_id(1)
    last = pl.num_programs(1) - 1
    pad = (CONV_K - 1) // 2
    buf[0:CONV_HALO] = jnp.where(i > 0, p_ref[0].astype(F32), 0.0)
    buf[CONV_HALO:CONV_HALO + ts] = c_ref[0].astype(F32)
    buf[CONV_HALO + ts:] = jnp.where(i < last, n_ref[0].astype(F32), 0.0)
    acc = jnp.zeros((ts, CONV_DIM), F32) + b_ref[...]
    for j in range(CONV_K):
        acc = acc + w_ref[j:j + 1, :] * buf[CONV_HALO - pad + j:CONV_HALO - pad + j + ts, :]
    o_ref[0] = _silu(acc).astype(o_ref.dtype)


def _conv_silu(zx_v, conv_w, conv_b, ts=512):
    b, s, _ = zx_v.shape
    ts = min(ts, s)
    r = ts // CONV_HALO
    nh = s // CONV_HALO
    w8 = jnp.zeros((8, CONV_DIM), F32).at[:CONV_K].set(conv_w)
    return pl.pallas_call(
        functools.partial(_conv_kernel, ts=ts),
        grid=(b, s // ts),
        in_specs=[pl.BlockSpec((1, ts, CONV_DIM), lambda bi, i: (bi, i, 0)),
                  pl.BlockSpec((1, CONV_HALO, CONV_DIM), lambda bi, i: (bi, jnp.maximum(i * r - 1, 0), 0)),
                  pl.BlockSpec((1, CONV_HALO, CONV_DIM), lambda bi, i: (bi, jnp.minimum((i + 1) * r, nh - 1), 0)),
                  _const_spec((8, CONV_DIM)), _const_spec((1, CONV_DIM))],
        out_specs=pl.BlockSpec((1, ts, CONV_DIM), lambda bi, i: (bi, i, 0)),
        out_shape=jax.ShapeDtypeStruct((b, s, CONV_DIM), BF16),
        scratch_shapes=[pltpu.VMEM((ts + 2 * CONV_HALO, CONV_DIM), F32)],
        compiler_params=_cparams("parallel", "parallel"),
        name="conv_silu",
    )(zx_v, zx_v, zx_v, w8, conv_b.reshape(1, CONV_DIM))


def _expand_heads(v, off):
    head = lax.broadcasted_iota(I32, (1, SSD_WIDTH), 1) // HEAD_DIM
    out = jnp.zeros((v.shape[0], SSD_WIDTH), F32)
    for h in range(SSD_HEADS):
        out = jnp.where(head == h, v[:, off + h:off + h + 1], out)
    return out


def _head_selector(off):
    r = lax.broadcasted_iota(I32, (LANES, SSD_WIDTH), 0)
    c = lax.broadcasted_iota(I32, (LANES, SSD_WIDTH), 1)
    return (r == c // HEAD_DIM + off).astype(BF16)


def _expand_heads_mxu(v, sel):
    hi = v.astype(BF16)
    lo = (v - hi.astype(F32)).astype(BF16)
    return (jnp.dot(hi, sel, preferred_element_type=F32) + jnp.dot(lo, sel, preferred_element_type=F32))


def _softplus(x):
    return jnp.maximum(x, 0.0) + jnp.log(1.0 + jnp.exp(-jnp.abs(x)))


def _ssd_chunk(xc, dtr, bias, a_row, state_ref, *, reverse, off):
    Q = SSD_CHUNK
    dt = _softplus(dtr + bias)
    a = dt * a_row
    ri = lax.broadcasted_iota(I32, (Q, Q), 0)
    ci = lax.broadcasted_iota(I32, (Q, Q), 1)
    keep = (ci >= ri) if reverse else (ci <= ri)
    tri = keep.astype(BF16)
    a0 = a.astype(BF16)
    a1 = (a - a0.astype(F32)).astype(BF16)
    a2 = (a - a0.astype(F32) - a1.astype(F32)).astype(BF16)
    cum = (jnp.dot(tri, a0, preferred_element_type=F32) + jnp.dot(tri, a1, preferred_element_type=F32)
           + jnp.dot(tri, a2, preferred_element_type=F32))
    cum_t = cum.T
    edge = 0 if reverse else Q - 1
    tot = cum[edge:edge + 1, :]
    sel = _head_selector(off)
    dt512 = _expand_heads_mxu(dt, sel)
    dec512 = _expand_heads_mxu(jnp.exp(tot - cum), sel)
    ecum512 = _expand_heads_mxu(jnp.exp(cum), sel)
    etot512 = _expand_heads(jnp.exp(tot), off)
    xs = xc[:, :SSD_WIDTH].astype(F32)
    xdt = xs * dt512
    xdt_b = xdt.astype(BF16)
    xdd_b = (xdt * dec512).astype(BF16)
    gw = HEADS_PER_GROUP * HEAD_DIM
    lane = lax.broadcasted_iota(I32, (1, LANES), 1)
    even = lane < HEAD_DIM
    ys = []
    for g in range(SSD_GROUPS):
        bg = xc[:, SSD_WIDTH + g * D_STATE:SSD_WIDTH + (g + 1) * D_STATE]
        cg = xc[:, SSD_WIDTH + (SSD_GROUPS + g) * D_STATE:SSD_WIDTH + (SSD_GROUPS + g + 1) * D_STATE]
        cb = lax.dot_general(cg, bg, (((1,), (1,)), ((), ())), preferred_element_type=F32)
        sg = state_ref[g]
        yoff = jnp.dot(cg, sg.astype(BF16), preferred_element_type=F32)
        for pr in range(HEADS_PER_GROUP // 2):
            ms = []
            for hh in (2 * pr, 2 * pr + 1):
                ln = off + g * HEADS_PER_GROUP + hh
                seg = cum[:, ln:ln + 1] - cum_t[ln:ln + 1, :]
                lmat = jnp.where(keep, jnp.exp(jnp.where(keep, seg, 0.0)), 0.0)
                ms.append((cb * lmat).astype(BF16))
            c0 = g * gw + pr * LANES
            yd2 = jnp.dot(jnp.concatenate(ms, axis=0), xdt_b[:, c0:c0 + LANES], preferred_element_type=F32)
            yd = jnp.where(even, yd2[:Q], yd2[Q:])
            ys.append(yd + yoff[:, pr * LANES:(pr + 1) * LANES] * ecum512[:, c0:c0 + LANES])
        bg_t = bg.astype(F32).T.astype(BF16)
        state_ref[g] = sg * etot512[:, g * gw:(g + 1) * gw] + jnp.dot(
            bg_t, xdd_b[:, g * gw:(g + 1) * gw], preferred_element_type=F32)
    return ys


def _ssd_fwd_kernel(x_ref, dt_ref, bias_ref, a_ref, y_ref, state_ref, *, nch):
    @pl.when(pl.program_id(1) == 0)
    def _():
        state_ref[...] = jnp.zeros_like(state_ref)

    for c in range(nch):
        rows = slice(c * SSD_CHUNK, (c + 1) * SSD_CHUNK)
        ys = _ssd_chunk(x_ref[0, rows, :], dt_ref[0, rows, :], bias_ref[...], a_ref[...], state_ref,
                        reverse=False, off=0)
        for j, y in enumerate(ys):
            y_ref[0, rows, j * LANES:(j + 1) * LANES] = y


def _ssd_bwd_kernel(x_ref, dt_ref, bias_ref, a_ref, yf_ref, z_ref, dskip_ref, nw_ref, o_ref, state_ref,
                    ybuf, *, nch):
    @pl.when(pl.program_id(1) == 0)
    def _():
        state_ref[...] = jnp.zeros_like(state_ref)

    for c in reversed(range(nch)):
        rows = slice(c * SSD_CHUNK, (c + 1) * SSD_CHUNK)
        ys = _ssd_chunk(x_ref[0, rows, :], dt_ref[0, rows, :], bias_ref[...], a_ref[...], state_ref,
                        reverse=True, off=SSD_HEADS)
        for j, y in enumerate(ys):
            ybuf[rows, j * LANES:(j + 1) * LANES] = y
    xs = x_ref[0, :, :SSD_WIDTH].astype(F32)
    y = yf_ref[0] + ybuf[...] + xs * dskip_ref[...]
    gy = y * _silu(z_ref[0].astype(F32))
    gw = SSD_WIDTH // SSD_GROUPS
    for g in range(SSD_GROUPS):
        part = gy[:, g * gw:(g + 1) * gw]
        ms = jnp.mean(part * part, -1, keepdims=True)
        o_ref[0, :, g * gw:(g + 1) * gw] = (part * lax.rsqrt(ms + RMS_EPS) * nw_ref[:, g * gw:(g + 1) * gw]
                                             ).astype(o_ref.dtype)


def _ssd(xc, dt_v, zx_v, dt_bias, a_log, d_skip, norm_w, nch=8):
    b, s, _ = xc.shape
    nch = min(nch, s // SSD_CHUNK)
    R = nch * SSD_CHUNK
    n = s // R
    a_neg = -jnp.exp(a_log.astype(F32))
    pad = LANES - 2 * SSD_HEADS
    bias = jnp.pad(dt_bias.astype(F32).reshape(1, 2 * SSD_HEADS), ((0, 0), (0, pad)))
    a_f = jnp.pad(a_neg[0].reshape(1, SSD_HEADS), ((0, 0), (0, LANES - SSD_HEADS)))
    a_b = jnp.pad(a_neg[1].reshape(1, SSD_HEADS), ((0, 0), (SSD_HEADS, pad)))
    dskip = jnp.repeat(d_skip.astype(F32), HEAD_DIM).reshape(1, SSD_WIDTH)
    state = pltpu.VMEM((SSD_GROUPS, D_STATE, HEADS_PER_GROUP * HEAD_DIM), F32)
    fwd = lambda bi, i: (bi, i, 0)
    rev = lambda bi, i: (bi, n - 1 - i, 0)
    y_f = pl.pallas_call(
        functools.partial(_ssd_fwd_kernel, nch=nch),
        grid=(b, n),
        in_specs=[pl.BlockSpec((1, R, CONV_DIM), fwd), pl.BlockSpec((1, R, LANES), fwd),
                  _const_spec((1, LANES)), _const_spec((1, LANES))],
        out_specs=pl.BlockSpec((1, R, SSD_WIDTH), fwd),
        out_shape=jax.ShapeDtypeStruct((b, s, SSD_WIDTH), F32),
        scratch_shapes=[state],
        compiler_params=_cparams("parallel", "arbitrary"),
        name="ssd_fwd",
    )(xc, dt_v, bias, a_f)
    out = pl.pallas_call(
        functools.partial(_ssd_bwd_kernel, nch=nch),
        grid=(b, n),
        in_specs=[pl.BlockSpec((1, R, CONV_DIM), rev), pl.BlockSpec((1, R, LANES), rev),
                  _const_spec((1, LANES)), _const_spec((1, LANES)),
                  pl.BlockSpec((1, R, SSD_WIDTH), rev),
                  pl.BlockSpec((1, R, SSD_WIDTH), lambda bi, i: (bi, n - 1 - i, CONV_DIM // SSD_WIDTH)),
                  _const_spec((1, SSD_WIDTH)), _const_spec((1, SSD_WIDTH))],
        out_specs=pl.BlockSpec((1, R, SSD_WIDTH), rev),
        out_shape=jax.ShapeDtypeStruct((b, s, SSD_WIDTH), BF16),
        scratch_shapes=[state, pltpu.VMEM((R, SSD_WIDTH), F32)],
        compiler_params=_cparams("parallel", "arbitrary"),
        name="ssd_bwd",
    )(xc, dt_v, bias, a_b, y_f, zx_v, dskip, norm_w.astype(F32).reshape(1, SSD_WIDTH))
    return out


def _natural_order(src_ref, scr_ref):
    dil, n, w = src_ref.shape[1:]
    if dil == 1:
        return src_ref[0, 0].astype(F32)
    for r in range(dil):
        for c in range(w // LANES):
            scr_ref[c, pl.ds(r, n, stride=dil), :] = src_ref[0, r, :, c * LANES:(c + 1) * LANES].astype(F32)
    return jnp.concatenate([scr_ref[c] for c in range(w // LANES)], axis=1)


def _out_proj_kernel(ssd_ref, o0_ref, o1_ref, o2_ref, l0_ref, l1_ref, l2_ref, h_ref, w1_ref, w2_ref,
                     g_ref, b_ref, wr_ref, out_ref, id_ref, gate_ref, so1, so2, sl1, sl2):
    lses = [_natural_order(r, s) for r, s in ((l0_ref, None), (l1_ref, sl1), (l2_ref, sl2))]
    mx = jnp.maximum(jnp.maximum(lses[0], lses[1]), lses[2])
    es = [jnp.exp(l - mx) for l in lses]
    den = es[0] + es[1] + es[2]
    sel = _head_selector(0)
    att = jnp.zeros((h_ref.shape[1], ATT_WIDTH), F32)
    for e, o_ref, scr in zip(es, (o0_ref, o1_ref, o2_ref), (None, so1, so2)):
        att = att + _expand_heads_mxu(e / den, sel) * _natural_order(o_ref, scr)
    y = jnp.dot(ssd_ref[0], w1_ref[...], preferred_element_type=F32)
    y = y + jnp.dot(att.astype(BF16), w2_ref[...], preferred_element_type=F32)
    hn = _ln_rows(DN_ALPHA * h_ref[0] + y, g_ref[...], b_ref[...])
    out_ref[0] = hn
    id_ref[...], gate_ref[...] = _route(hn, wr_ref)


def _out_proj_ln(ssd, os_, lses, h, w_out, g, b, w_router, tm=512):
    bsz, s, D = h.shape
    w0 = w_router.astype(BF16)
    w_router = jnp.concatenate([w0, (w_router - w0.astype(F32)).astype(BF16)], 0)
    n = s // tm
    route = pl.BlockSpec((8, tm), lambda bi, i: (0, bi * n + i))
    dils = [d for _, d in DILATED_PATTERNS]
    row = lambda n: pl.BlockSpec((1, tm, n), lambda bi, i: (bi, i, 0))
    sub = lambda d, n: pl.BlockSpec((1, d, tm // d, n), lambda bi, i: (bi, 0, i, 0))
    w1 = w_out[:SSD_WIDTH]
    w2 = w_out[SSD_WIDTH:]
    return pl.pallas_call(
        _out_proj_kernel,
        grid=(bsz, s // tm),
        in_specs=[row(SSD_WIDTH)] + [sub(d, ATT_WIDTH) for d in dils] + [sub(d, LANES) for d in dils]
        + [row(D), _const_spec(w1.shape), _const_spec(w2.shape), _const_spec((1, D)), _const_spec((1, D)),
           _const_spec(w_router.shape)],
        out_specs=[row(D), route, route],
        out_shape=[jax.ShapeDtypeStruct((bsz, s, D), F32), jax.ShapeDtypeStruct((8, bsz * s), I32),
                   jax.ShapeDtypeStruct((8, bsz * s), F32)],
        scratch_shapes=[pltpu.VMEM((ATT_WIDTH // LANES, tm, LANES), F32)] * 2
        + [pltpu.VMEM((1, tm, LANES), F32)] * 2,
        compiler_params=_cparams("parallel", "parallel"),
        name="out_proj_ln",
    )(ssd, *os_, *lses, h, w1, w2, g.reshape(1, D), b.reshape(1, D), w_router)


def _route(h, w_ref):
    nt = (((1,), (1,)), ((), ()))
    h0 = h.astype(BF16)
    h1 = (h - h0.astype(F32)).astype(BF16)
    nr = w_ref.shape[0] // 2
    r0 = lax.dot_general(w_ref[...], h0, nt, preferred_element_type=F32)
    r1 = lax.dot_general(w_ref[0:nr, :], h1, nt, preferred_element_type=F32)
    logits = r0[0:nr] + r0[nr:] + r1
    tm = logits.shape[1]
    row = lax.broadcasted_iota(I32, (8, tm), 0)
    lg = jnp.where(row < N_EXPERT_GROUPS, logits[0:8], NEG)
    gm = jnp.max(lg, 0, keepdims=True)
    gs = jnp.sum(jnp.exp(lg - gm), 0, keepdims=True)
    g_idx = jnp.min(jnp.where(lg == gm, row, 8), 0, keepdims=True)
    g_prob = 1.0 / gs
    el = jnp.zeros((8, tm), F32)
    for g in range(N_EXPERT_GROUPS):
        el = jnp.where(g_idx == g, logits[8 + 8 * g:16 + 8 * g], el)
    em = jnp.max(el, 0, keepdims=True)
    ee = jnp.exp(el - em)
    p = ee / jnp.sum(ee, 0, keepdims=True)
    p1 = jnp.max(p, 0, keepdims=True)
    i1 = jnp.min(jnp.where(p == p1, row, 8), 0, keepdims=True)
    pr = jnp.where(row == i1, -1.0, p)
    p2 = jnp.max(pr, 0, keepdims=True)
    i2 = jnp.min(jnp.where(pr == p2, row, 8), 0, keepdims=True)
    den = p1 + p2
    base = g_idx * EXPERTS_PER_GROUP
    ids = jnp.where(row == 0, base + i1, jnp.where(row == 1, base + i2, 0))
    gates = jnp.where(row == 0, g_prob * p1 / den, jnp.where(row == 1, g_prob * p2 / den, 0.0))
    return ids, gates


def _expert_kernel(be_ref, nu_ref, tok0_ref, tok1_ref, tok2_ref, h_hbm, wg_ref, wu_ref, wd_ref, o_ref,
                   xbuf, gsem):
    i = pl.program_id(0)
    nu = nu_ref[0]

    def gather_row(tok_ref, s, j, q=0):
        pltpu.async_copy(h_hbm.at[pl.ds(tok_ref[0, 0, j], 1)], xbuf.at[s, pl.ds(j, 1)], gsem.at[s], priority=q)

    def gather_wait(s):
        pltpu.make_async_copy(h_hbm.at[pl.ds(0, MOE_BLOCK)], xbuf.at[s], gsem.at[s]).wait()

    def rolled(ref, s):
        def body(j, c):
            gather_row(ref, s, j)
            return c
        lax.fori_loop(0, MOE_BLOCK, body, 0, unroll=8)

    @pl.when(i == 0)
    def _():
        rolled(tok0_ref, 0)
        rolled(tok1_ref, 1)

    @pl.when(i >= nu)
    def _():
        o_ref[...] = jnp.zeros(o_ref.shape, F32)

    def step(cur):
        nxt, nx2 = (cur + 1) % NBUF, (cur + 2) % NBUF
        gather_wait(cur)

        def issue(part, nparts=4):
            n = MOE_BLOCK // nparts
            for j in range(part * n, (part + 1) * n):
                gather_row(tok2_ref, nx2, j, j % 2)

        x = xbuf[cur].astype(BF16)
        issue(0)
        hg = jnp.dot(x, wg_ref[0], preferred_element_type=F32)
        issue(1)
        hu = jnp.dot(x, wu_ref[0], preferred_element_type=F32)
        issue(2)
        hdn = (_silu(hg) * hu).astype(BF16)
        issue(3)
        o_ref[...] = jnp.dot(hdn, wd_ref[0], preferred_element_type=F32)

        @pl.when(i == nu - 1)
        def _():
            gather_wait(nxt)
            gather_wait(nx2)

    for k in range(NBUF):
        pl.when((i < nu) & (i % NBUF == k))(functools.partial(step, k))


def _expert_ffn(h, slot_token, block_expert, n_used, w_gate, w_up, w_down):
    T, D = h.shape
    nb = slot_token.shape[0]
    smem = lambda f: pl.BlockSpec((1, 1, MOE_BLOCK), f, memory_space=pltpu.SMEM)
    ahead = lambda k: (lambda i, be, nu: (jnp.minimum(i + k, nb - 1), 0, 0))
    grid_spec = pltpu.PrefetchScalarGridSpec(
        num_scalar_prefetch=2,
        grid=(nb,),
        in_specs=[smem(ahead(0)), smem(ahead(1)), smem(ahead(2)),
                  pl.BlockSpec(memory_space=pl.ANY),
                  pl.BlockSpec((1, D, EXPERT_FF), lambda i, be, nu: (be[i], 0, 0)),
                  pl.BlockSpec((1, D, EXPERT_FF), lambda i, be, nu: (be[i], 0, 0)),
                  pl.BlockSpec((1, EXPERT_FF, D), lambda i, be, nu: (be[i], 0, 0))],
        out_specs=pl.BlockSpec((MOE_BLOCK, D), lambda i, be, nu: (i, 0)),
        scratch_shapes=[pltpu.VMEM((NBUF, MOE_BLOCK, D), F32), pltpu.SemaphoreType.DMA((NBUF,))],
    )
    return pl.pallas_call(
        _expert_kernel,
        grid_spec=grid_spec,
        out_shape=jax.ShapeDtypeStruct((nb * MOE_BLOCK, D), F32),
        compiler_params=_cparams("arbitrary"),
        name="expert_ffn",
    )(block_expert, n_used, slot_token, slot_token, slot_token, h, w_gate, w_up, w_down)


def _combine_kernel(dc_ref, dn_ref, h_ref, gate_ref, g_ref, b_ref, y_hbm, o_ref, yb0, yb1, sem, *, tm):
    i = pl.program_id(0)
    last = pl.num_programs(0) - 1
    ybuf = (yb0, yb1)

    def row(d_ref, s, t, k):
        pltpu.async_copy(y_hbm.at[pl.ds(d_ref[0, 0, TOP_K * t + k], 1)], ybuf[s].at[k, pl.ds(t, 1)],
                         sem.at[s], priority=k)

    def wait(s):
        for k in range(TOP_K):
            pltpu.make_async_copy(y_hbm.at[pl.ds(0, tm)], ybuf[s].at[k], sem.at[s]).wait()

    @pl.when(i == 0)
    def _():
        def body(t, c):
            for k in range(TOP_K):
                row(dc_ref, 0, t, k)
            return c
        lax.fori_loop(0, tm, body, 0, unroll=8)

    def step(cur):
        wait(cur)
        for t in range(tm):
            for k in range(TOP_K):
                row(dn_ref, 1 - cur, t, k)
        gt = gate_ref[...]
        ffn = ybuf[cur][0] * gt[:, 0:1] + ybuf[cur][1] * gt[:, 1:2]
        o_ref[...] = _ln_rows(DN_ALPHA * h_ref[...] + ffn, g_ref[...], b_ref[...])

        @pl.when(i == last)
        def _():
            wait(1 - cur)

    for s in range(2):
        pl.when(i % 2 == s)(functools.partial(step, s))


def _combine_ln(h, y, dest, gates, g, b, tm=512):
    T, D = h.shape
    n = T // tm
    dest = dest.reshape(n, 1, TOP_K * tm)
    smem = lambda f: pl.BlockSpec((1, 1, TOP_K * tm), f, memory_space=pltpu.SMEM)
    return pl.pallas_call(
        functools.partial(_combine_kernel, tm=tm),
        grid=(n,),
        in_specs=[smem(lambda i: (i, 0, 0)), smem(lambda i: (jnp.minimum(i + 1, n - 1), 0, 0)),
                  pl.BlockSpec((tm, D), lambda i: (i, 0)), pl.BlockSpec((tm, TOP_K), lambda i: (i, 0)),
                  _const_spec((1, D)), _const_spec((1, D)), pl.BlockSpec(memory_space=pl.ANY)],
        out_specs=pl.BlockSpec((tm, D), lambda i: (i, 0)),
        out_shape=jax.ShapeDtypeStruct((T, D), F32),
        scratch_shapes=[pltpu.VMEM((TOP_K, tm, D), F32)] * 2 + [pltpu.SemaphoreType.DMA((2,))],
        compiler_params=_cparams("arbitrary"),
        name="combine_ln",
    )(dest, dest, h, gates, g.reshape(1, D), b.reshape(1, D), y)


def _mixer(h, lw, tabs):
    *atts, zx, dt_raw = _in_proj(h, lw["w_att"], lw["w_zx"], lw["w_dt"], tabs)
    os_, lses = [], []
    for gi in range(N_DIL):
        o, lse = _dilated_attention(atts[gi], gi)
        os_.append(o)
        lses.append(lse)
    xc = _conv_silu(zx, lw["conv_w"], lw["conv_b"])
    ssd = _ssd(xc, dt_raw, zx, lw["dt_bias"], lw["a_log"], lw["d_skip"], lw["ssd_norm_w"])
    return _out_proj_ln(ssd, os_, lses, h, lw["w_out"], lw["ln1_g"], lw["ln1_b"], lw["w_router"])


def _moe(h, ids, gates, lw):
    T, D = h.shape
    A = T * TOP_K
    expert = ids[:TOP_K].T.reshape(A)
    order = jnp.argsort(expert, stable=True).astype(I32)
    inv = jnp.argsort(order).astype(I32)
    onehot = expert[:, None] == jnp.arange(N_EXPERTS, dtype=I32)[None, :]
    counts = jnp.sum(onehot, 0, dtype=I32)
    padded = (counts + MOE_BLOCK - 1) // MOE_BLOCK * MOE_BLOCK
    pad_end = jnp.cumsum(padded)
    pad_start = pad_end - padded
    start = jnp.cumsum(counts) - counts
    dest = inv + jnp.sum(jnp.where(onehot, (pad_start - start)[None, :], 0), -1, dtype=I32)
    n_blocks = -(-A // MOE_BLOCK) + N_EXPERTS
    blk0 = jnp.arange(n_blocks, dtype=I32) * MOE_BLOCK
    block_expert = jnp.minimum(jnp.sum(pad_end[None, :] <= blk0[:, None], -1, dtype=I32), N_EXPERTS - 1)
    n_used = (pad_end[-1] // MOE_BLOCK).astype(I32).reshape(1)
    sel = block_expert[:, None] == jnp.arange(N_EXPERTS, dtype=I32)[None, :]
    pick = lambda v: jnp.sum(jnp.where(sel, v[None, :], 0), -1, dtype=I32)[:, None]
    off = blk0[:, None] - pick(pad_start) + jnp.arange(MOE_BLOCK, dtype=I32)[None, :]
    asg = order[jnp.clip(pick(start) + off, 0, A - 1)]
    slot_token = jnp.where(off < pick(counts), asg // TOP_K, 0).reshape(n_blocks, 1, MOE_BLOCK)
    y = _expert_ffn(h, slot_token, block_expert, n_used, lw["w_gate"], lw["w_up"], lw["w_down"])
    return _combine_ln(h, y, dest, gates[:TOP_K].T, lw["ln2_g"], lw["ln2_b"])


def _trunk(x, ln_in_g, ln_in_b, layers):
    b, s, D = x.shape
    tabs = _rope_tables(s)
    h = _layer_norm(x.reshape(b * s, D), ln_in_g, ln_in_b)
    for lw in layers:
        h, ids, gates = _mixer(h.reshape(b, s, D), lw, tabs)
        h = _moe(h.reshape(b * s, D), ids, gates, lw)
    return h.reshape(b, s, D)


def _prep_layers(w_in, conv_w, conv_b, a_log, dt_bias, d_skip, ssd_norm_w, w_out, ln1_g, ln1_b,
                 router_group, router_expert, w_gate, w_up, w_down, ln2_g, ln2_b):
    layers = []
    z0 = ATT_PROJ
    x0 = ATT_PROJ + SSD_WIDTH
    d0 = x0 + CONV_DIM
    for i in range(w_in.shape[0]):
        w = w_in[i]
        w_dt = jnp.pad(w[:, d0:], ((0, 0), (0, LANES - 2 * SSD_HEADS))).astype(BF16)
        w_router = jnp.concatenate([
            router_group[i].T, jnp.zeros((8 - N_EXPERT_GROUPS, D_MODEL), F32), router_expert[i].T], 0)
        layers.append(dict(
            w_att=w[:, :z0].astype(BF16),
            w_zx=jnp.concatenate([w[:, x0:d0], w[:, z0:x0]], 1).astype(BF16),
            w_dt=w_dt,
            conv_w=conv_w[i], conv_b=conv_b[i], a_log=a_log[i], dt_bias=dt_bias[i], d_skip=d_skip[i],
            ssd_norm_w=ssd_norm_w[i], w_out=w_out[i].astype(BF16), ln1_g=ln1_g[i], ln1_b=ln1_b[i],
            w_router=w_router, w_gate=w_gate[i].astype(BF16), w_up=w_up[i].astype(BF16),
            w_down=w_down[i].astype(BF16), ln2_g=ln2_g[i], ln2_b=ln2_b[i]))
    return layers


def kernel(x_prompt, x_sample, ln_in_g, ln_in_b, w_in, conv_w, conv_b, a_log, dt_bias, d_skip, ssd_norm_w,
           w_out, ln1_g, ln1_b, router_group, router_expert, w_gate, w_up, w_down, ln2_g, ln2_b):
    layers = _prep_layers(w_in, conv_w, conv_b, a_log, dt_bias, d_skip, ssd_norm_w, w_out, ln1_g, ln1_b,
                          router_group, router_expert, w_gate, w_up, w_down, ln2_g, ln2_b)
    y_prompt = _trunk(x_prompt, ln_in_g, ln_in_b, layers)
    y_sample = _trunk(x_sample, ln_in_g, ln_in_b, layers)
    return (y_prompt, y_sample)
```

```python
import functools

import jax
import jax.numpy as jnp
from jax import lax
from jax.experimental import pallas as pl
from jax.experimental.pallas import tpu as pltpu

F32 = jnp.float32
BF16 = jnp.bfloat16
I32 = jnp.int32

D_MODEL = 1024
DEPTH = 4
HEAD_DIM = 64
DILATED_PATTERNS = ((128, 1), (512, 4), (2048, 16))
N_DIL = 3
ATT_HEADS = 8
ATT_WIDTH = ATT_HEADS * HEAD_DIM
ATT_PROJ = N_DIL * 3 * ATT_WIDTH
ROPE_THETA = 10000.0
SSD_HEADS = 8
SSD_WIDTH = SSD_HEADS * HEAD_DIM
SSD_GROUPS = 2
HEADS_PER_GROUP = SSD_HEADS // SSD_GROUPS
D_STATE = 128
CONV_K = 5
CONV_DIM = SSD_WIDTH + 2 * SSD_GROUPS * D_STATE
N_EXPERT_GROUPS = 4
EXPERTS_PER_GROUP = 8
N_EXPERTS = N_EXPERT_GROUPS * EXPERTS_PER_GROUP
TOP_K = 2
EXPERT_FF = 512
MOE_BLOCK = 256
DN_ALPHA = (2 * DEPTH) ** 0.25
LN_EPS = 1e-5
RMS_EPS = 1e-5
NEG = -1e30

LANES = 128
SSD_CHUNK = 128
ATT_SUB = 128
ATT_HALO = 64
CONV_HALO = 16
NBUF = 3
VMEM_LIMIT = 48 * 1024 * 1024


def _cparams(*sem):
    return pltpu.CompilerParams(dimension_semantics=sem, vmem_limit_bytes=VMEM_LIMIT)


def _const_spec(shape):
    nd = len(shape)
    return pl.BlockSpec(shape, lambda *_: (0,) * nd)


def _ln_rows(x, g, b):
    mu = jnp.mean(x, -1, keepdims=True)
    xc = x - mu
    var = jnp.mean(xc * xc, -1, keepdims=True)
    return xc * lax.rsqrt(var + LN_EPS) * g + b


def _silu(x):
    return x / (1.0 + jnp.exp(-x))


def _ln_kernel(x_ref, g_ref, b_ref, o_ref):
    o_ref[...] = _ln_rows(x_ref[...], g_ref[...], b_ref[...])


def _layer_norm(x, g, b, tm=512):
    T, D = x.shape
    return pl.pallas_call(
        _ln_kernel,
        grid=(T // tm,),
        in_specs=[pl.BlockSpec((tm, D), lambda i: (i, 0)), _const_spec((1, D)), _const_spec((1, D))],
        out_specs=pl.BlockSpec((tm, D), lambda i: (i, 0)),
        out_shape=jax.ShapeDtypeStruct((T, D), F32),
        compiler_params=_cparams("parallel"),
        name="ln_in",
    )(x, g.reshape(1, D), b.reshape(1, D))


def _rope_chunks(y, tab):
    cos = tab[:, :LANES]
    sin = tab[:, LANES:]
    lane = lax.broadcasted_iota(I32, (1, LANES), 1)
    first = (lane % HEAD_DIM) < (HEAD_DIM // 2)
    out = []
    for c in range(ATT_WIDTH // LANES):
        tc = y[:, c * LANES:(c + 1) * LANES]
        rot = jnp.where(first, pltpu.roll(tc, LANES - HEAD_DIM // 2, 1), pltpu.roll(tc, HEAD_DIM // 2, 1))
        out.append(tc * cos + rot * sin)
    return out


def _proj_kernel(x_ref, wa_ref, wz_ref, wd_ref, t0_ref, t1_ref, t2_ref,
                 a0_ref, a1_ref, a2_ref, oz_ref, od_ref, xc_ref, xs_ref, *, tm, tn):
    xb = x_ref[0].astype(BF16)
    nlc = x_ref.shape[2] // LANES
    for c in range(nlc):
        xc_ref[c] = x_ref[0, :, c * LANES:(c + 1) * LANES]
    for j in range(oz_ref.shape[2] // tn):
        sl = slice(j * tn, (j + 1) * tn)
        oz_ref[0, :, sl] = jnp.dot(xb, wz_ref[:, sl], preferred_element_type=F32).astype(oz_ref.dtype)
    od_ref[0] = jnp.dot(xb, wd_ref[...], preferred_element_type=F32)
    scale = HEAD_DIM ** -0.5
    nchunk = ATT_WIDTH // LANES
    for g, (t_ref, a_ref) in enumerate(((t0_ref, a0_ref), (t1_ref, a1_ref), (t2_ref, a2_ref))):
        dil = DILATED_PATTERNS[g][1]
        n = tm // dil
        if dil == 1:
            xp = xb
        else:
            for r in range(dil):
                for c in range(nlc):
                    xs_ref[r * n:(r + 1) * n, c * LANES:(c + 1) * LANES] = (
                        xc_ref[c, pl.ds(r, n, stride=dil), :].astype(BF16))
            xp = xs_ref[...]
        tab = t_ref[...].reshape(tm, 2 * LANES)
        for j in range(3):
            c0 = (g * 3 + j) * ATT_WIDTH
            y = jnp.dot(xp, wa_ref[:, c0:c0 + ATT_WIDTH], preferred_element_type=F32)
            if j < 2:
                chunks = _rope_chunks(y, tab)
                if j == 0:
                    chunks = [ch * scale for ch in chunks]
            else:
                chunks = [y[:, c * LANES:(c + 1) * LANES] for c in range(nchunk)]
            for c, ch in enumerate(chunks):
                chb = ch.astype(BF16)
                col = j * ATT_WIDTH + c * LANES
                for r in range(dil):
                    a_ref[0, r, :, col:col + LANES] = chb[r * n:(r + 1) * n]


def _in_proj(h, w_att, w_zx, w_dt, tabs, tm=512, tn=512):
    b, s, D = h.shape
    dils = [d for _, d in DILATED_PATTERNS]
    qkv = 3 * ATT_WIDTH
    nz, nd = w_zx.shape[1], w_dt.shape[1]
    row = lambda n: pl.BlockSpec((1, tm, n), lambda bi, i: (bi, i, 0))
    sub = lambda d, n: pl.BlockSpec((1, d, tm // d, n), lambda bi, i: (bi, 0, i, 0))
    return pl.pallas_call(
        functools.partial(_proj_kernel, tm=tm, tn=tn),
        grid=(b, s // tm),
        in_specs=[row(D)] + [_const_spec(w.shape) for w in (w_att, w_zx, w_dt)]
        + [pl.BlockSpec((d, tm // d, 2 * LANES), lambda bi, i: (0, i, 0)) for d in dils],
        out_specs=[sub(d, qkv) for d in dils] + [row(nz), row(nd)],
        out_shape=[jax.ShapeDtypeStruct((b, d, s // d, qkv), BF16) for d in dils]
        + [jax.ShapeDtypeStruct((b, s, nz), BF16), jax.ShapeDtypeStruct((b, s, nd), F32)],
        scratch_shapes=[pltpu.VMEM((D // LANES, tm, LANES), F32), pltpu.VMEM((tm, D), BF16)],
        compiler_params=_cparams("parallel", "parallel"),
        name="in_proj",
    )(h, w_att, w_zx, w_dt, *tabs)


def _attn_kernel(q_ref, kc_ref, kp_ref, kn_ref, vc_ref, vp_ref, vn_ref, o_ref, lse_ref, kbuf, vbuf,
                 *, bq, seq_len, half):
    qi = pl.program_id(2)
    kbuf[0:ATT_HALO] = kp_ref[0, 0]
    kbuf[ATT_HALO:ATT_HALO + bq] = kc_ref[0, 0]
    kbuf[ATT_HALO + bq:] = kn_ref[0, 0]
    vbuf[0:ATT_HALO] = vp_ref[0, 0]
    vbuf[ATT_HALO:ATT_HALO + bq] = vc_ref[0, 0]
    vbuf[ATT_HALO + bq:] = vn_ref[0, 0]

    nk = ATT_SUB + 2 * ATT_HALO
    npair = ATT_HEADS // 2
    ri = lax.broadcasted_iota(I32, (ATT_SUB, nk), 0)
    ci = lax.broadcasted_iota(I32, (ATT_SUB, nk), 1)
    band = jnp.abs(ci - ATT_HALO - ri) <= half
    krow = lax.broadcasted_iota(I32, (1, nk), 1)
    lane = lax.broadcasted_iota(I32, (1, LANES), 1)
    even = lane < HEAD_DIM
    for sb in range(bq // ATT_SUB):
        r0 = sb * ATT_SUB
        kpos = qi * bq + r0 - ATT_HALO + krow
        mask = band & ((kpos >= 0) & (kpos < seq_len))
        ss = []
        for j in range(npair):
            cs = slice(j * LANES, (j + 1) * LANES)
            qp = q_ref[0, 0, r0:r0 + ATT_SUB, cs]
            zero = jnp.zeros_like(qp)
            lhs = jnp.concatenate([jnp.where(even, qp, zero), jnp.where(even, zero, qp)], axis=0)
            ss.append(lax.dot_general(lhs, kbuf[r0:r0 + nk, cs], (((1,), (1,)), ((), ())),
                                      preferred_element_type=F32))
        s = jnp.stack(ss).reshape(ATT_HEADS, ATT_SUB, nk)
        s = jnp.where(mask[None], s, NEG)
        m = jnp.max(s, -1, keepdims=True)
        p = jnp.exp(s - m)
        l = jnp.sum(p, -1, keepdims=True)
        pb = p.astype(BF16).reshape(npair, 2 * ATT_SUB, nk)
        l2 = l.reshape(npair, 2 * ATT_SUB, 1)
        for j in range(npair):
            cs = slice(j * LANES, (j + 1) * LANES)
            o2 = jnp.dot(pb[j], vbuf[r0:r0 + nk, cs], preferred_element_type=F32) / l2[j]
            o_ref[0, 0, r0:r0 + ATT_SUB, cs] = jnp.where(even, o2[:ATT_SUB], o2[ATT_SUB:]).astype(o_ref.dtype)
        m_tile = jnp.zeros((ATT_SUB, LANES), F32)
        l_tile = jnp.ones((ATT_SUB, LANES), F32)
        for h in range(ATT_HEADS):
            m_tile = jnp.where(lane == h, m[h], m_tile)
            l_tile = jnp.where(lane == h, l[h], l_tile)
        lse_ref[0, 0, r0:r0 + ATT_SUB, :] = m_tile + jnp.log(l_tile)


def _dilated_attention(att, gi, bq=1024):
    win, dil = DILATED_PATTERNS[gi]
    half = win // (2 * dil)
    assert half <= ATT_HALO
    b, _, L, _ = att.shape
    bq = min(bq, L)
    nq = L // bq
    hb = bq // ATT_HALO
    nhb = L // ATT_HALO
    cur = lambda j: (lambda bi, r, qi: (bi, r, qi, j))
    prev = lambda j: (lambda bi, r, qi: (bi, r, jnp.maximum(qi * hb - 1, 0), j))
    nxt = lambda j: (lambda bi, r, qi: (bi, r, jnp.minimum((qi + 1) * hb, nhb - 1), j))
    blk = lambda n, f: pl.BlockSpec((1, 1, n, ATT_WIDTH), f)
    return pl.pallas_call(
        functools.partial(_attn_kernel, bq=bq, seq_len=L, half=half),
        grid=(b, dil, nq),
        in_specs=[blk(bq, cur(0)),
                  blk(bq, cur(1)), blk(ATT_HALO, prev(1)), blk(ATT_HALO, nxt(1)),
                  blk(bq, cur(2)), blk(ATT_HALO, prev(2)), blk(ATT_HALO, nxt(2))],
        out_specs=[pl.BlockSpec((1, 1, bq, ATT_WIDTH), lambda bi, r, qi: (bi, r, qi, 0)),
                   pl.BlockSpec((1, 1, bq, LANES), lambda bi, r, qi: (bi, r, qi, 0))],
        out_shape=[jax.ShapeDtypeStruct((b, dil, L, ATT_WIDTH), BF16),
                   jax.ShapeDtypeStruct((b, dil, L, LANES), F32)],
        scratch_shapes=[pltpu.VMEM((bq + 2 * ATT_HALO, ATT_WIDTH), BF16),
                        pltpu.VMEM((bq + 2 * ATT_HALO, ATT_WIDTH), BF16)],
        compiler_params=_cparams("parallel", "parallel", "parallel"),
        name=f"dil_attn_{gi}",
    )(att, att, att, att, att, att, att)


def _rope_tables(s):
    half = HEAD_DIM // 2
    inv = ROPE_THETA ** (-jnp.arange(half, dtype=F32) / half)
    ang = jnp.arange(s).astype(F32)[:, None] * inv[None, :]
    cos, sin = jnp.cos(ang), jnp.sin(ang)
    cos_h = jnp.concatenate([cos, cos], -1)
    sin_h = jnp.concatenate([-sin, sin], -1)
    rep = LANES // HEAD_DIM
    tab = jnp.concatenate([jnp.tile(cos_h, (1, rep)), jnp.tile(sin_h, (1, rep))], -1)
    return [tab.reshape(s // d, d, 2 * LANES).transpose(1, 0, 2) for _, d in DILATED_PATTERNS]


def _conv_kernel(c_ref, p_ref, n_ref, w_ref, b_ref, o_ref, buf, *, ts):
    i = pl.program_id(1)
    last = pl.num_programs(1) - 1
    pad = (CONV_K - 1) // 2
    buf[0:CONV_HALO] = jnp.where(i > 0, p_ref[0].astype(F32), 0.0)
    buf[CONV_HALO:CONV_HALO + ts] = c_ref[0].astype(F32)
    buf[CONV_HALO + ts:] = jnp.where(i < last, n_ref[0].astype(F32), 0.0)
    acc = jnp.zeros((ts, CONV_DIM), F32) + b_ref[...]
    for j in range(CONV_K):
        acc = acc + w_ref[j:j + 1, :] * buf[CONV_HALO - pad + j:CONV_HALO - pad + j + ts, :]
    o_ref[0] = _silu(acc).astype(o_ref.dtype)


def _conv_silu(zx_v, conv_w, conv_b, ts=1024):
    b, s, _ = zx_v.shape
    ts = min(ts, s)
    r = ts // CONV_HALO
    nh = s // CONV_HALO
    w8 = jnp.zeros((8, CONV_DIM), F32).at[:CONV_K].set(conv_w)
    return pl.pallas_call(
        functools.partial(_conv_kernel, ts=ts),
        grid=(b, s // ts),
        in_specs=[pl.BlockSpec((1, ts, CONV_DIM), lambda bi, i: (bi, i, 0)),
                  pl.BlockSpec((1, CONV_HALO, CONV_DIM), lambda bi, i: (bi, jnp.maximum(i * r - 1, 0), 0)),
                  pl.BlockSpec((1, CONV_HALO, CONV_DIM), lambda bi, i: (bi, jnp.minimum((i + 1) * r, nh - 1), 0)),
                  _const_spec((8, CONV_DIM)), _const_spec((1, CONV_DIM))],
        out_specs=pl.BlockSpec((1, ts, CONV_DIM), lambda bi, i: (bi, i, 0)),
        out_shape=jax.ShapeDtypeStruct((b, s, CONV_DIM), BF16),
        scratch_shapes=[pltpu.VMEM((ts + 2 * CONV_HALO, CONV_DIM), F32)],
        compiler_params=_cparams("parallel", "parallel"),
        name="conv_silu",
    )(zx_v, zx_v, zx_v, w8, conv_b.reshape(1, CONV_DIM))


def _expand_heads(v, off):
    head = lax.broadcasted_iota(I32, (1, SSD_WIDTH), 1) // HEAD_DIM
    out = jnp.zeros((v.shape[0], SSD_WIDTH), F32)
    for h in range(SSD_HEADS):
        out = jnp.where(head == h, v[:, off + h:off + h + 1], out)
    return out


def _head_selector(off):
    r = lax.broadcasted_iota(I32, (LANES, SSD_WIDTH), 0)
    c = lax.broadcasted_iota(I32, (LANES, SSD_WIDTH), 1)
    return (r == c // HEAD_DIM + off).astype(BF16)


def _expand_heads_mxu(v, sel):
    hi = v.astype(BF16)
    lo = (v - hi.astype(F32)).astype(BF16)
    return (jnp.dot(hi, sel, preferred_element_type=F32) + jnp.dot(lo, sel, preferred_element_type=F32))


def _softplus(x):
    return jnp.maximum(x, 0.0) + jnp.log(1.0 + jnp.exp(-jnp.abs(x)))


def _ssd_chunk(xc, dtr, bias, a_row, state_ref, *, reverse, off):
    Q = SSD_CHUNK
    dt = _softplus(dtr + bias)
    a = dt * a_row
    ri = lax.broadcasted_iota(I32, (Q, Q), 0)
    ci = lax.broadcasted_iota(I32, (Q, Q), 1)
    keep = (ci >= ri) if reverse else (ci <= ri)
    tri = keep.astype(BF16)
    a0 = a.astype(BF16)
    a1 = (a - a0.astype(F32)).astype(BF16)
    a2 = (a - a0.astype(F32) - a1.astype(F32)).astype(BF16)
    cum = (jnp.dot(tri, a0, preferred_element_type=F32) + jnp.dot(tri, a1, preferred_element_type=F32)
           + jnp.dot(tri, a2, preferred_element_type=F32))
    cum_t = cum.T
    edge = 0 if reverse else Q - 1
    tot = cum[edge:edge + 1, :]
    sel = _head_selector(off)
    dt512 = _expand_heads_mxu(dt, sel)
    dec512 = _expand_heads_mxu(jnp.exp(tot - cum), sel)
    ecum512 = _expand_heads_mxu(jnp.exp(cum), sel)
    etot512 = _expand_heads(jnp.exp(tot), off)
    xs = xc[:, :SSD_WIDTH].astype(F32)
    xdt = xs * dt512
    xdt_b = xdt.astype(BF16)
    xdd_b = (xdt * dec512).astype(BF16)
    gw = HEADS_PER_GROUP * HEAD_DIM
    lane = lax.broadcasted_iota(I32, (1, LANES), 1)
    even = lane < HEAD_DIM
    ys = []
    for g in range(SSD_GROUPS):
        bg = xc[:, SSD_WIDTH + g * D_STATE:SSD_WIDTH + (g + 1) * D_STATE]
        cg = xc[:, SSD_WIDTH + (SSD_GROUPS + g) * D_STATE:SSD_WIDTH + (SSD_GROUPS + g + 1) * D_STATE]
        cb = lax.dot_general(cg, bg, (((1,), (1,)), ((), ())), preferred_element_type=F32)
        sg = state_ref[g]
        yoff = jnp.dot(cg, sg.astype(BF16), preferred_element_type=F32)
        for pr in range(HEADS_PER_GROUP // 2):
            ms = []
            for hh in (2 * pr, 2 * pr + 1):
                ln = off + g * HEADS_PER_GROUP + hh
                seg = cum[:, ln:ln + 1] - cum_t[ln:ln + 1, :]
                lmat = jnp.where(keep, jnp.exp(jnp.where(keep, seg, 0.0)), 0.0)
                ms.append((cb * lmat).astype(BF16))
            c0 = g * gw + pr * LANES
            yd2 = jnp.dot(jnp.concatenate(ms, axis=0), xdt_b[:, c0:c0 + LANES], preferred_element_type=F32)
            yd = jnp.where(even, yd2[:Q], yd2[Q:])
            ys.append(yd + yoff[:, pr * LANES:(pr + 1) * LANES] * ecum512[:, c0:c0 + LANES])
        bg_t = bg.astype(F32).T.astype(BF16)
        state_ref[g] = sg * etot512[:, g * gw:(g + 1) * gw] + jnp.dot(
            bg_t, xdd_b[:, g * gw:(g + 1) * gw], preferred_element_type=F32)
    return ys


def _ssd_fwd_kernel(x_ref, dt_ref, bias_ref, a_ref, y_ref, state_ref, *, nch):
    @pl.when(pl.program_id(1) == 0)
    def _():
        state_ref[...] = jnp.zeros_like(state_ref)

    for c in range(nch):
        rows = slice(c * SSD_CHUNK, (c + 1) * SSD_CHUNK)
        ys = _ssd_chunk(x_ref[0, rows, :], dt_ref[0, rows, :], bias_ref[...], a_ref[...], state_ref,
                        reverse=False, off=0)
        for j, y in enumerate(ys):
            y_ref[0, rows, j * LANES:(j + 1) * LANES] = y


def _ssd_bwd_kernel(x_ref, dt_ref, bias_ref, a_ref, yf_ref, z_ref, dskip_ref, nw_ref, o_ref, state_ref,
                    ybuf, *, nch):
    @pl.when(pl.program_id(1) == 0)
    def _():
        state_ref[...] = jnp.zeros_like(state_ref)

    for c in reversed(range(nch)):
        rows = slice(c * SSD_CHUNK, (c + 1) * SSD_CHUNK)
        ys = _ssd_chunk(x_ref[0, rows, :], dt_ref[0, rows, :], bias_ref[...], a_ref[...], state_ref,
                        reverse=True, off=SSD_HEADS)
        for j, y in enumerate(ys):
            ybuf[rows, j * LANES:(j + 1) * LANES] = y
    xs = x_ref[0, :, :SSD_WIDTH].astype(F32)
    y = yf_ref[0] + ybuf[...] + xs * dskip_ref[...]
    gy = y * _silu(z_ref[0].astype(F32))
    gw = SSD_WIDTH // SSD_GROUPS
    for g in range(SSD_GROUPS):
        part = gy[:, g * gw:(g + 1) * gw]
        ms = jnp.mean(part * part, -1, keepdims=True)
        o_ref[0, :, g * gw:(g + 1) * gw] = (part * lax.rsqrt(ms + RMS_EPS) * nw_ref[:, g * gw:(g + 1) * gw]
                                             ).astype(o_ref.dtype)


def _ssd(xc, dt_v, zx_v, dt_bias, a_log, d_skip, norm_w, nch=8):
    b, s, _ = xc.shape
    nch = min(nch, s // SSD_CHUNK)
    R = nch * SSD_CHUNK
    n = s // R
    a_neg = -jnp.exp(a_log.astype(F32))
    pad = LANES - 2 * SSD_HEADS
    bias = jnp.pad(dt_bias.astype(F32).reshape(1, 2 * SSD_HEADS), ((0, 0), (0, pad)))
    a_f = jnp.pad(a_neg[0].reshape(1, SSD_HEADS), ((0, 0), (0, LANES - SSD_HEADS)))
    a_b = jnp.pad(a_neg[1].reshape(1, SSD_HEADS), ((0, 0), (SSD_HEADS, pad)))
    dskip = jnp.repeat(d_skip.astype(F32), HEAD_DIM).reshape(1, SSD_WIDTH)
    state = pltpu.VMEM((SSD_GROUPS, D_STATE, HEADS_PER_GROUP * HEAD_DIM), F32)
    fwd = lambda bi, i: (bi, i, 0)
    rev = lambda bi, i: (bi, n - 1 - i, 0)
    y_f = pl.pallas_call(
        functools.partial(_ssd_fwd_kernel, nch=nch),
        grid=(b, n),
        in_specs=[pl.BlockSpec((1, R, CONV_DIM), fwd), pl.BlockSpec((1, R, LANES), fwd),
                  _const_spec((1, LANES)), _const_spec((1, LANES))],
        out_specs=pl.BlockSpec((1, R, SSD_WIDTH), fwd),
        out_shape=jax.ShapeDtypeStruct((b, s, SSD_WIDTH), F32),
        scratch_shapes=[state],
        compiler_params=_cparams("parallel", "arbitrary"),
        name="ssd_fwd",
    )(xc, dt_v, bias, a_f)
    out = pl.pallas_call(
        functools.partial(_ssd_bwd_kernel, nch=nch),
        grid=(b, n),
        in_specs=[pl.BlockSpec((1, R, CONV_DIM), rev), pl.BlockSpec((1, R, LANES), rev),
                  _const_spec((1, LANES)), _const_spec((1, LANES)),
                  pl.BlockSpec((1, R, SSD_WIDTH), rev),
                  pl.BlockSpec((1, R, SSD_WIDTH), lambda bi, i: (bi, n - 1 - i, CONV_DIM // SSD_WIDTH)),
                  _const_spec((1, SSD_WIDTH)), _const_spec((1, SSD_WIDTH))],
        out_specs=pl.BlockSpec((1, R, SSD_WIDTH), rev),
        out_shape=jax.ShapeDtypeStruct((b, s, SSD_WIDTH), BF16),
        scratch_shapes=[state, pltpu.VMEM((R, SSD_WIDTH), F32)],
        compiler_params=_cparams("parallel", "arbitrary"),
        name="ssd_bwd",
    )(xc, dt_v, bias, a_b, y_f, zx_v, dskip, norm_w.astype(F32).reshape(1, SSD_WIDTH))
    return out


def _natural_order(src_ref, scr_ref):
    dil, n, w = src_ref.shape[1:]
    if dil == 1:
        return src_ref[0, 0].astype(F32)
    for r in range(dil):
        for c in range(w // LANES):
            scr_ref[c, pl.ds(r, n, stride=dil), :] = src_ref[0, r, :, c * LANES:(c + 1) * LANES].astype(F32)
    return jnp.concatenate([scr_ref[c] for c in range(w // LANES)], axis=1)


def _out_proj_kernel(ssd_ref, o0_ref, o1_ref, o2_ref, l0_ref, l1_ref, l2_ref, h_ref, w1_ref, w2_ref,
                     g_ref, b_ref, wr_ref, out_ref, id_ref, gate_ref, so1, so2, sl1, sl2):
    lses = [_natural_order(r, s) for r, s in ((l0_ref, None), (l1_ref, sl1), (l2_ref, sl2))]
    mx = jnp.maximum(jnp.maximum(lses[0], lses[1]), lses[2])
    es = [jnp.exp(l - mx) for l in lses]
    den = es[0] + es[1] + es[2]
    sel = _head_selector(0)
    att = jnp.zeros((h_ref.shape[1], ATT_WIDTH), F32)
    for e, o_ref, scr in zip(es, (o0_ref, o1_ref, o2_ref), (None, so1, so2)):
        att = att + _expand_heads_mxu(e / den, sel) * _natural_order(o_ref, scr)
    y = jnp.dot(ssd_ref[0], w1_ref[...], preferred_element_type=F32)
    y = y + jnp.dot(att.astype(BF16), w2_ref[...], preferred_element_type=F32)
    hn = _ln_rows(DN_ALPHA * h_ref[0] + y, g_ref[...], b_ref[...])
    out_ref[0] = hn
    id_ref[...], gate_ref[...] = _route(hn, wr_ref)


def _out_proj_ln(ssd, os_, lses, h, w_out, g, b, w_router, tm=1024):
    bsz, s, D = h.shape
    w0 = w_router.astype(BF16)
    w_router = jnp.concatenate([w0, (w_router - w0.astype(F32)).astype(BF16)], 0)
    n = s // tm
    route = pl.BlockSpec((8, tm), lambda bi, i: (0, bi * n + i))
    dils = [d for _, d in DILATED_PATTERNS]
    row = lambda n: pl.BlockSpec((1, tm, n), lambda bi, i: (bi, i, 0))
    sub = lambda d, n: pl.BlockSpec((1, d, tm // d, n), lambda bi, i: (bi, 0, i, 0))
    w1 = w_out[:SSD_WIDTH]
    w2 = w_out[SSD_WIDTH:]
    return pl.pallas_call(
        _out_proj_kernel,
        grid=(bsz, s // tm),
        in_specs=[row(SSD_WIDTH)] + [sub(d, ATT_WIDTH) for d in dils] + [sub(d, LANES) for d in dils]
        + [row(D), _const_spec(w1.shape), _const_spec(w2.shape), _const_spec((1, D)), _const_spec((1, D)),
           _const_spec(w_router.shape)],
        out_specs=[row(D), route, route],
        out_shape=[jax.ShapeDtypeStruct((bsz, s, D), F32), jax.ShapeDtypeStruct((8, bsz * s), I32),
                   jax.ShapeDtypeStruct((8, bsz * s), F32)],
        scratch_shapes=[pltpu.VMEM((ATT_WIDTH // LANES, tm, LANES), F32)] * 2
        + [pltpu.VMEM((1, tm, LANES), F32)] * 2,
        compiler_params=_cparams("parallel", "parallel"),
        name="out_proj_ln",
    )(ssd, *os_, *lses, h, w1, w2, g.reshape(1, D), b.reshape(1, D), w_router)


def _route(h, w_ref):
    nt = (((1,), (1,)), ((), ()))
    h0 = h.astype(BF16)
    h1 = (h - h0.astype(F32)).astype(BF16)
    nr = w_ref.shape[0] // 2
    r0 = lax.dot_general(w_ref[...], h0, nt, preferred_element_type=F32)
    r1 = lax.dot_general(w_ref[0:nr, :], h1, nt, preferred_element_type=F32)
    logits = r0[0:nr] + r0[nr:] + r1
    tm = logits.shape[1]
    row = lax.broadcasted_iota(I32, (8, tm), 0)
    lg = jnp.where(row < N_EXPERT_GROUPS, logits[0:8], NEG)
    gm = jnp.max(lg, 0, keepdims=True)
    gs = jnp.sum(jnp.exp(lg - gm), 0, keepdims=True)
    g_idx = jnp.min(jnp.where(lg == gm, row, 8), 0, keepdims=True)
    g_prob = 1.0 / gs
    el = jnp.zeros((8, tm), F32)
    for g in range(N_EXPERT_GROUPS):
        el = jnp.where(g_idx == g, logits[8 + 8 * g:16 + 8 * g], el)
    em = jnp.max(el, 0, keepdims=True)
    ee = jnp.exp(el - em)
    p = ee / jnp.sum(ee, 0, keepdims=True)
    p1 = jnp.max(p, 0, keepdims=True)
    i1 = jnp.min(jnp.where(p == p1, row, 8), 0, keepdims=True)
    pr = jnp.where(row == i1, -1.0, p)
    p2 = jnp.max(pr, 0, keepdims=True)
    i2 = jnp.min(jnp.where(pr == p2, row, 8), 0, keepdims=True)
    den = p1 + p2
    base = g_idx * EXPERTS_PER_GROUP
    ids = jnp.where(row == 0, base + i1, jnp.where(row == 1, base + i2, 0))
    gates = jnp.where(row == 0, g_prob * p1 / den, jnp.where(row == 1, g_prob * p2 / den, 0.0))
    return ids, gates


def _expert_kernel(be_ref, nu_ref, tok0_ref, tok1_ref, tok2_ref, h_hbm, wg_ref, wu_ref, wd_ref, o_ref,
                   xbuf, gsem):
    i = pl.program_id(0)
    nu = nu_ref[0]

    def gather_row(tok_ref, s, j, q=0):
        pltpu.async_copy(h_hbm.at[pl.ds(tok_ref[0, 0, j], 1)], xbuf.at[s, pl.ds(j, 1)], gsem.at[s], priority=q)

    def gather_wait(s):
        pltpu.make_async_copy(h_hbm.at[pl.ds(0, MOE_BLOCK)], xbuf.at[s], gsem.at[s]).wait()

    def rolled(ref, s):
        def body(j, c):
            gather_row(ref, s, j)
            return c
        lax.fori_loop(0, MOE_BLOCK, body, 0, unroll=8)

    @pl.when(i == 0)
    def _():
        rolled(tok0_ref, 0)
        rolled(tok1_ref, 1)

    @pl.when(i >= nu)
    def _():
        o_ref[...] = jnp.zeros(o_ref.shape, F32)

    def step(cur):
        nxt, nx2 = (cur + 1) % NBUF, (cur + 2) % NBUF
        gather_wait(cur)

        def issue(part, nparts=4):
            n = MOE_BLOCK // nparts
            for j in range(part * n, (part + 1) * n):
                gather_row(tok2_ref, nx2, j, j % 2)

        x = xbuf[cur].astype(BF16)
        issue(0)
        hg = jnp.dot(x, wg_ref[0], preferred_element_type=F32)
        issue(1)
        hu = jnp.dot(x, wu_ref[0], preferred_element_type=F32)
        issue(2)
        hdn = (_silu(hg) * hu).astype(BF16)
        issue(3)
        o_ref[...] = jnp.dot(hdn, wd_ref[0], preferred_element_type=F32)

        @pl.when(i == nu - 1)
        def _():
            gather_wait(nxt)
            gather_wait(nx2)

    for k in range(NBUF):
        pl.when((i < nu) & (i % NBUF == k))(functools.partial(step, k))


def _expert_ffn(h, slot_token, block_expert, n_used, w_gate, w_up, w_down):
    T, D = h.shape
    nb = slot_token.shape[0]
    smem = lambda f: pl.BlockSpec((1, 1, MOE_BLOCK), f, memory_space=pltpu.SMEM)
    ahead = lambda k: (lambda i, be, nu: (jnp.minimum(i + k, nb - 1), 0, 0))
    grid_spec = pltpu.PrefetchScalarGridSpec(
        num_scalar_prefetch=2,
        grid=(nb,),
        in_specs=[smem(ahead(0)), smem(ahead(1)), smem(ahead(2)),
                  pl.BlockSpec(memory_space=pl.ANY),
                  pl.BlockSpec((1, D, EXPERT_FF), lambda i, be, nu: (be[i], 0, 0)),
                  pl.BlockSpec((1, D, EXPERT_FF), lambda i, be, nu: (be[i], 0, 0)),
                  pl.BlockSpec((1, EXPERT_FF, D), lambda i, be, nu: (be[i], 0, 0))],
        out_specs=pl.BlockSpec((MOE_BLOCK, D), lambda i, be, nu: (i, 0)),
        scratch_shapes=[pltpu.VMEM((NBUF, MOE_BLOCK, D), F32), pltpu.SemaphoreType.DMA((NBUF,))],
    )
    return pl.pallas_call(
        _expert_kernel,
        grid_spec=grid_spec,
        out_shape=jax.ShapeDtypeStruct((nb * MOE_BLOCK, D), F32),
        compiler_params=_cparams("arbitrary"),
        name="expert_ffn",
    )(block_expert, n_used, slot_token, slot_token, slot_token, h, w_gate, w_up, w_down)


def _combine_kernel(dc_ref, dn_ref, h_ref, gate_ref, g_ref, b_ref, y_hbm, o_ref, yb0, yb1, sem, *, tm):
    i = pl.program_id(0)
    last = pl.num_programs(0) - 1
    ybuf = (yb0, yb1)

    def row(d_ref, s, t, k):
        pltpu.async_copy(y_hbm.at[pl.ds(d_ref[0, 0, TOP_K * t + k], 1)], ybuf[s].at[k, pl.ds(t, 1)],
                         sem.at[s], priority=k)

    def wait(s):
        for k in range(TOP_K):
            pltpu.make_async_copy(y_hbm.at[pl.ds(0, tm)], ybuf[s].at[k], sem.at[s]).wait()

    @pl.when(i == 0)
    def _():
        def body(t, c):
            for k in range(TOP_K):
                row(dc_ref, 0, t, k)
            return c
        lax.fori_loop(0, tm, body, 0, unroll=8)

    def step(cur):
        wait(cur)
        for t in range(tm):
            for k in range(TOP_K):
                row(dn_ref, 1 - cur, t, k)
        gt = gate_ref[...]
        ffn = ybuf[cur][0] * gt[:, 0:1] + ybuf[cur][1] * gt[:, 1:2]
        o_ref[...] = _ln_rows(DN_ALPHA * h_ref[...] + ffn, g_ref[...], b_ref[...])

        @pl.when(i == last)
        def _():
            wait(1 - cur)

    for s in range(2):
        pl.when(i % 2 == s)(functools.partial(step, s))


def _combine_ln(h, y, dest, gates, g, b, tm=512):
    T, D = h.shape
    n = T // tm
    dest = dest.reshape(n, 1, TOP_K * tm)
    smem = lambda f: pl.BlockSpec((1, 1, TOP_K * tm), f, memory_space=pltpu.SMEM)
    return pl.pallas_call(
        functools.partial(_combine_kernel, tm=tm),
        grid=(n,),
        in_specs=[smem(lambda i: (i, 0, 0)), smem(lambda i: (jnp.minimum(i + 1, n - 1), 0, 0)),
                  pl.BlockSpec((tm, D), lambda i: (i, 0)), pl.BlockSpec((tm, TOP_K), lambda i: (i, 0)),
                  _const_spec((1, D)), _const_spec((1, D)), pl.BlockSpec(memory_space=pl.ANY)],
        out_specs=pl.BlockSpec((tm, D), lambda i: (i, 0)),
        out_shape=jax.ShapeDtypeStruct((T, D), F32),
        scratch_shapes=[pltpu.VMEM((TOP_K, tm, D), F32)] * 2 + [pltpu.SemaphoreType.DMA((2,))],
        compiler_params=_cparams("arbitrary"),
        name="combine_ln",
    )(dest, dest, h, gates, g.reshape(1, D), b.reshape(1, D), y)


def _mixer(h, lw, tabs):
    *atts, zx, dt_raw = _in_proj(h, lw["w_att"], lw["w_zx"], lw["w_dt"], tabs)
    os_, lses = [], []
    for gi in range(N_DIL):
        o, lse = _dilated_attention(atts[gi], gi)
        os_.append(o)
        lses.append(lse)
    xc = _conv_silu(zx, lw["conv_w"], lw["conv_b"])
    ssd = _ssd(xc, dt_raw, zx, lw["dt_bias"], lw["a_log"], lw["d_skip"], lw["ssd_norm_w"])
    return _out_proj_ln(ssd, os_, lses, h, lw["w_out"], lw["ln1_g"], lw["ln1_b"], lw["w_router"])


def _moe(h, ids, gates, lw):
    T, D = h.shape
    A = T * TOP_K
    expert = ids[:TOP_K].T.reshape(A)
    order = jnp.argsort(expert, stable=True).astype(I32)
    inv = jnp.argsort(order).astype(I32)
    onehot = expert[:, None] == jnp.arange(N_EXPERTS, dtype=I32)[None, :]
    counts = jnp.sum(onehot, 0, dtype=I32)
    padded = (counts + MOE_BLOCK - 1) // MOE_BLOCK * MOE_BLOCK
    pad_end = jnp.cumsum(padded)
    pad_start = pad_end - padded
    start = jnp.cumsum(counts) - counts
    dest = inv + jnp.sum(jnp.where(onehot, (pad_start - start)[None, :], 0), -1, dtype=I32)
    n_blocks = -(-A // MOE_BLOCK) + N_EXPERTS
    blk0 = jnp.arange(n_blocks, dtype=I32) * MOE_BLOCK
    block_expert = jnp.minimum(jnp.sum(pad_end[None, :] <= blk0[:, None], -1, dtype=I32), N_EXPERTS - 1)
    n_used = (pad_end[-1] // MOE_BLOCK).astype(I32).reshape(1)
    sel = block_expert[:, None] == jnp.arange(N_EXPERTS, dtype=I32)[None, :]
    pick = lambda v: jnp.sum(jnp.where(sel, v[None, :], 0), -1, dtype=I32)[:, None]
    off = blk0[:, None] - pick(pad_start) + jnp.arange(MOE_BLOCK, dtype=I32)[None, :]
    asg = order[jnp.clip(pick(start) + off, 0, A - 1)]
    slot_token = jnp.where(off < pick(counts), asg // TOP_K, 0).reshape(n_blocks, 1, MOE_BLOCK)
    y = _expert_ffn(h, slot_token, block_expert, n_used, lw["w_gate"], lw["w_up"], lw["w_down"])
    return _combine_ln(h, y, dest, gates[:TOP_K].T, lw["ln2_g"], lw["ln2_b"])


def _trunk(x, ln_in_g, ln_in_b, layers):
    b, s, D = x.shape
    tabs = _rope_tables(s)
    h = _layer_norm(x.reshape(b * s, D), ln_in_g, ln_in_b)
    for lw in layers:
        h, ids, gates = _mixer(h.reshape(b, s, D), lw, tabs)
        h = _moe(h.reshape(b * s, D), ids, gates, lw)
    return h.reshape(b, s, D)


def _prep_layers(w_in, conv_w, conv_b, a_log, dt_bias, d_skip, ssd_norm_w, w_out, ln1_g, ln1_b,
                 router_group, router_expert, w_gate, w_up, w_down, ln2_g, ln2_b):
    layers = []
    z0 = ATT_PROJ
    x0 = ATT_PROJ + SSD_WIDTH
    d0 = x0 + CONV_DIM
    for i in range(w_in.shape[0]):
        w = w_in[i]
        w_dt = jnp.pad(w[:, d0:], ((0, 0), (0, LANES - 2 * SSD_HEADS))).astype(BF16)
        w_router = jnp.concatenate([
            router_group[i].T, jnp.zeros((8 - N_EXPERT_GROUPS, D_MODEL), F32), router_expert[i].T], 0)
        layers.append(dict(
            w_att=w[:, :z0].astype(BF16),
            w_zx=jnp.concatenate([w[:, x0:d0], w[:, z0:x0]], 1).astype(BF16),
            w_dt=w_dt,
            conv_w=conv_w[i], conv_b=conv_b[i], a_log=a_log[i], dt_bias=dt_bias[i], d_skip=d_skip[i],
            ssd_norm_w=ssd_norm_w[i], w_out=w_out[i].astype(BF16), ln1_g=ln1_g[i], ln1_b=ln1_b[i],
            w_router=w_router, w_gate=w_gate[i].astype(BF16), w_up=w_up[i].astype(BF16),
            w_down=w_down[i].astype(BF16), ln2_g=ln2_g[i], ln2_b=ln2_b[i]))
    return layers


def kernel(x_prompt, x_sample, ln_in_g, ln_in_b, w_in, conv_w, conv_b, a_log, dt_bias, d_skip, ssd_norm_w,
           w_out, ln1_g, ln1_b, router_group, router_expert, w_gate, w_up, w_down, ln2_g, ln2_b):
    layers = _prep_layers(w_in, conv_w, conv_b, a_log, dt_bias, d_skip, ssd_norm_w, w_out, ln1_g, ln1_b,
                          router_group, router_expert, w_gate, w_up, w_down, ln2_g, ln2_b)
    y_prompt = _trunk(x_prompt, ln_in_g, ln_in_b, layers)
    y_sample = _trunk(x_sample, ln_in_g, ln_in_b, layers)
    return (y_prompt, y_sample)
```

```python
import functools

import jax
import jax.numpy as jnp
from jax import lax
from jax.experimental import pallas as pl
from jax.experimental.pallas import tpu as pltpu

F32 = jnp.float32
BF16 = jnp.bfloat16
I32 = jnp.int32

D_MODEL = 1024
DEPTH = 4
HEAD_DIM = 64
DILATED_PATTERNS = ((128, 1), (512, 4), (2048, 16))
N_DIL = 3
ATT_HEADS = 8
ATT_WIDTH = ATT_HEADS * HEAD_DIM
ATT_PROJ = N_DIL * 3 * ATT_WIDTH
ROPE_THETA = 10000.0
SSD_HEADS = 8
SSD_WIDTH = SSD_HEADS * HEAD_DIM
SSD_GROUPS = 2
HEADS_PER_GROUP = SSD_HEADS // SSD_GROUPS
D_STATE = 128
CONV_K = 5
CONV_DIM = SSD_WIDTH + 2 * SSD_GROUPS * D_STATE
N_EXPERT_GROUPS = 4
EXPERTS_PER_GROUP = 8
N_EXPERTS = N_EXPERT_GROUPS * EXPERTS_PER_GROUP
TOP_K = 2
EXPERT_FF = 512
MOE_BLOCK = 256
DN_ALPHA = (2 * DEPTH) ** 0.25
LN_EPS = 1e-5
RMS_EPS = 1e-5
NEG = -1e30

LANES = 128
SSD_CHUNK = 128
ATT_SUB = 128
ATT_HALO = 64
CONV_HALO = 16
NBUF = 3
VMEM_LIMIT = 48 * 1024 * 1024


def _cparams(*sem):
    return pltpu.CompilerParams(dimension_semantics=sem, vmem_limit_bytes=VMEM_LIMIT)


def _const_spec(shape):
    nd = len(shape)
    return pl.BlockSpec(shape, lambda *_: (0,) * nd)


def _ln_rows(x, g, b):
    mu = jnp.mean(x, -1, keepdims=True)
    xc = x - mu
    var = jnp.mean(xc * xc, -1, keepdims=True)
    return xc * lax.rsqrt(var + LN_EPS) * g + b


def _silu(x):
    return x / (1.0 + jnp.exp(-x))


def _ln_kernel(x_ref, g_ref, b_ref, o_ref):
    o_ref[...] = _ln_rows(x_ref[...], g_ref[...], b_ref[...])


def _layer_norm(x, g, b, tm=512):
    T, D = x.shape
    return pl.pallas_call(
        _ln_kernel,
        grid=(T // tm,),
        in_specs=[pl.BlockSpec((tm, D), lambda i: (i, 0)), _const_spec((1, D)), _const_spec((1, D))],
        out_specs=pl.BlockSpec((tm, D), lambda i: (i, 0)),
        out_shape=jax.ShapeDtypeStruct((T, D), F32),
        compiler_params=_cparams("parallel"),
        name="ln_in",
    )(x, g.reshape(1, D), b.reshape(1, D))


def _rope_chunks(y, tab):
    cos = tab[:, :LANES]
    sin = tab[:, LANES:]
    lane = lax.broadcasted_iota(I32, (1, LANES), 1)
    first = (lane % HEAD_DIM) < (HEAD_DIM // 2)
    out = []
    for c in range(ATT_WIDTH // LANES):
        tc = y[:, c * LANES:(c + 1) * LANES]
        rot = jnp.where(first, pltpu.roll(tc, LANES - HEAD_DIM // 2, 1), pltpu.roll(tc, HEAD_DIM // 2, 1))
        out.append(tc * cos + rot * sin)
    return out


def _proj_kernel(x_ref, wa_ref, wz_ref, wd_ref, t0_ref, t1_ref, t2_ref,
                 a0_ref, a1_ref, a2_ref, oz_ref, od_ref, xc_ref, xs_ref, *, tm, tn):
    xb = x_ref[0].astype(BF16)
    nlc = x_ref.shape[2] // LANES
    for c in range(nlc):
        xc_ref[c] = x_ref[0, :, c * LANES:(c + 1) * LANES]
    for j in range(oz_ref.shape[2] // tn):
        sl = slice(j * tn, (j + 1) * tn)
        oz_ref[0, :, sl] = jnp.dot(xb, wz_ref[:, sl], preferred_element_type=F32).astype(oz_ref.dtype)
    od_ref[0] = jnp.dot(xb, wd_ref[...], preferred_element_type=F32)
    scale = HEAD_DIM ** -0.5
    nchunk = ATT_WIDTH // LANES
    for g, (t_ref, a_ref) in enumerate(((t0_ref, a0_ref), (t1_ref, a1_ref), (t2_ref, a2_ref))):
        dil = DILATED_PATTERNS[g][1]
        n = tm // dil
        if dil == 1:
            xp = xb
        else:
            for r in range(dil):
                for c in range(nlc):
                    xs_ref[r * n:(r + 1) * n, c * LANES:(c + 1) * LANES] = (
                        xc_ref[c, pl.ds(r, n, stride=dil), :].astype(BF16))
            xp = xs_ref[...]
        tab = t_ref[...].reshape(tm, 2 * LANES)
        for j in range(3):
            c0 = (g * 3 + j) * ATT_WIDTH
            y = jnp.dot(xp, wa_ref[:, c0:c0 + ATT_WIDTH], preferred_element_type=F32)
            if j < 2:
                chunks = _rope_chunks(y, tab)
                if j == 0:
                    chunks = [ch * scale for ch in chunks]
            else:
                chunks = [y[:, c * LANES:(c + 1) * LANES] for c in range(nchunk)]
            for c, ch in enumerate(chunks):
                chb = ch.astype(BF16)
                col = j * ATT_WIDTH + c * LANES
                for r in range(dil):
                    a_ref[0, r, :, col:col + LANES] = chb[r * n:(r + 1) * n]


def _in_proj(h, w_att, w_zx, w_dt, tabs, tm=512, tn=512):
    b, s, D = h.shape
    dils = [d for _, d in DILATED_PATTERNS]
    qkv = 3 * ATT_WIDTH
    nz, nd = w_zx.shape[1], w_dt.shape[1]
    row = lambda n: pl.BlockSpec((1, tm, n), lambda bi, i: (bi, i, 0))
    sub = lambda d, n: pl.BlockSpec((1, d, tm // d, n), lambda bi, i: (bi, 0, i, 0))
    return pl.pallas_call(
        functools.partial(_proj_kernel, tm=tm, tn=tn),
        grid=(b, s // tm),
        in_specs=[row(D)] + [_const_spec(w.shape) for w in (w_att, w_zx, w_dt)]
        + [pl.BlockSpec((d, tm // d, 2 * LANES), lambda bi, i: (0, i, 0)) for d in dils],
        out_specs=[sub(d, qkv) for d in dils] + [row(nz), row(nd)],
        out_shape=[jax.ShapeDtypeStruct((b, d, s // d, qkv), BF16) for d in dils]
        + [jax.ShapeDtypeStruct((b, s, nz), BF16), jax.ShapeDtypeStruct((b, s, nd), F32)],
        scratch_shapes=[pltpu.VMEM((D // LANES, tm, LANES), F32), pltpu.VMEM((tm, D), BF16)],
        compiler_params=_cparams("parallel", "parallel"),
        name="in_proj",
    )(h, w_att, w_zx, w_dt, *tabs)


def _attn_kernel(q_ref, kc_ref, kp_ref, kn_ref, vc_ref, vp_ref, vn_ref, o_ref, lse_ref, kbuf, vbuf,
                 *, bq, seq_len, half):
    qi = pl.program_id(2)
    kbuf[0:ATT_HALO] = kp_ref[0, 0]
    kbuf[ATT_HALO:ATT_HALO + bq] = kc_ref[0, 0]
    kbuf[ATT_HALO + bq:] = kn_ref[0, 0]
    vbuf[0:ATT_HALO] = vp_ref[0, 0]
    vbuf[ATT_HALO:ATT_HALO + bq] = vc_ref[0, 0]
    vbuf[ATT_HALO + bq:] = vn_ref[0, 0]

    nk = ATT_SUB + 2 * ATT_HALO
    npair = ATT_HEADS // 2
    ri = lax.broadcasted_iota(I32, (ATT_SUB, nk), 0)
    ci = lax.broadcasted_iota(I32, (ATT_SUB, nk), 1)
    band = jnp.abs(ci - ATT_HALO - ri) <= half
    krow = lax.broadcasted_iota(I32, (1, nk), 1)
    lane = lax.broadcasted_iota(I32, (1, LANES), 1)
    even = lane < HEAD_DIM
    for sb in range(bq // ATT_SUB):
        r0 = sb * ATT_SUB
        kpos = qi * bq + r0 - ATT_HALO + krow
        mask = band & ((kpos >= 0) & (kpos < seq_len))
        ss = []
        for j in range(npair):
            cs = slice(j * LANES, (j + 1) * LANES)
            qp = q_ref[0, 0, r0:r0 + ATT_SUB, cs]
            zero = jnp.zeros_like(qp)
            lhs = jnp.concatenate([jnp.where(even, qp, zero), jnp.where(even, zero, qp)], axis=0)
            ss.append(lax.dot_general(lhs, kbuf[r0:r0 + nk, cs], (((1,), (1,)), ((), ())),
                                      preferred_element_type=F32))
        s = jnp.stack(ss).reshape(ATT_HEADS, ATT_SUB, nk)
        s = jnp.where(mask[None], s, NEG)
        m = jnp.max(s, -1, keepdims=True)
        p = jnp.exp(s - m)
        l = jnp.sum(p, -1, keepdims=True)
        pb = p.astype(BF16).reshape(npair, 2 * ATT_SUB, nk)
        l2 = l.reshape(npair, 2 * ATT_SUB, 1)
        for j in range(npair):
            cs = slice(j * LANES, (j + 1) * LANES)
            o2 = jnp.dot(pb[j], vbuf[r0:r0 + nk, cs], preferred_element_type=F32) / l2[j]
            o_ref[0, 0, r0:r0 + ATT_SUB, cs] = jnp.where(even, o2[:ATT_SUB], o2[ATT_SUB:]).astype(o_ref.dtype)
        m_tile = jnp.zeros((ATT_SUB, LANES), F32)
        l_tile = jnp.ones((ATT_SUB, LANES), F32)
        for h in range(ATT_HEADS):
            m_tile = jnp.where(lane == h, m[h], m_tile)
            l_tile = jnp.where(lane == h, l[h], l_tile)
        lse_ref[0, 0, r0:r0 + ATT_SUB, :] = m_tile + jnp.log(l_tile)


def _dilated_attention(att, gi, bq=1024):
    win, dil = DILATED_PATTERNS[gi]
    half = win // (2 * dil)
    assert half <= ATT_HALO
    b, _, L, _ = att.shape
    bq = min(bq, L)
    nq = L // bq
    hb = bq // ATT_HALO
    nhb = L // ATT_HALO
    cur = lambda j: (lambda bi, r, qi: (bi, r, qi, j))
    prev = lambda j: (lambda bi, r, qi: (bi, r, jnp.maximum(qi * hb - 1, 0), j))
    nxt = lambda j: (lambda bi, r, qi: (bi, r, jnp.minimum((qi + 1) * hb, nhb - 1), j))
    blk = lambda n, f: pl.BlockSpec((1, 1, n, ATT_WIDTH), f)
    return pl.pallas_call(
        functools.partial(_attn_kernel, bq=bq, seq_len=L, half=half),
        grid=(b, dil, nq),
        in_specs=[blk(bq, cur(0)),
                  blk(bq, cur(1)), blk(ATT_HALO, prev(1)), blk(ATT_HALO, nxt(1)),
                  blk(bq, cur(2)), blk(ATT_HALO, prev(2)), blk(ATT_HALO, nxt(2))],
        out_specs=[pl.BlockSpec((1, 1, bq, ATT_WIDTH), lambda bi, r, qi: (bi, r, qi, 0)),
                   pl.BlockSpec((1, 1, bq, LANES), lambda bi, r, qi: (bi, r, qi, 0))],
        out_shape=[jax.ShapeDtypeStruct((b, dil, L, ATT_WIDTH), BF16),
                   jax.ShapeDtypeStruct((b, dil, L, LANES), F32)],
        scratch_shapes=[pltpu.VMEM((bq + 2 * ATT_HALO, ATT_WIDTH), BF16),
                        pltpu.VMEM((bq + 2 * ATT_HALO, ATT_WIDTH), BF16)],
        compiler_params=_cparams("parallel", "parallel", "parallel"),
        name=f"dil_attn_{gi}",
    )(att, att, att, att, att, att, att)


def _rope_tables(s):
    half = HEAD_DIM // 2
    inv = ROPE_THETA ** (-jnp.arange(half, dtype=F32) / half)
    ang = jnp.arange(s).astype(F32)[:, None] * inv[None, :]
    cos, sin = jnp.cos(ang), jnp.sin(ang)
    cos_h = jnp.concatenate([cos, cos], -1)
    sin_h = jnp.concatenate([-sin, sin], -1)
    rep = LANES // HEAD_DIM
    tab = jnp.concatenate([jnp.tile(cos_h, (1, rep)), jnp.tile(sin_h, (1, rep))], -1)
    return [tab.reshape(s // d, d, 2 * LANES).transpose(1, 0, 2) for _, d in DILATED_PATTERNS]


def _conv_kernel(c_ref, p_ref, n_ref, w_ref, b_ref, o_ref, buf, *, ts):
    i = pl.program_id(1)
    last = pl.num_programs(1) - 1
    pad = (CONV_K - 1) // 2
    buf[0:CONV_HALO] = jnp.where(i > 0, p_ref[0].astype(F32), 0.0)
    buf[CONV_HALO:CONV_HALO + ts] = c_ref[0].astype(F32)
    buf[CONV_HALO + ts:] = jnp.where(i < last, n_ref[0].astype(F32), 0.0)
    xe = buf[...]
    n = ts + 2 * CONV_HALO
    acc = jnp.zeros((ts, CONV_DIM), F32) + b_ref[...]
    for j in range(CONV_K):
        z = xe if j == pad else pltpu.roll(xe, (pad - j) % n, 0)
        acc = acc + w_ref[j:j + 1, :] * z[CONV_HALO:CONV_HALO + ts]
    o_ref[0] = _silu(acc).astype(o_ref.dtype)


def _conv_silu(zx_v, conv_w, conv_b, ts=1024):
    b, s, _ = zx_v.shape
    ts = min(ts, s)
    r = ts // CONV_HALO
    nh = s // CONV_HALO
    w8 = jnp.zeros((8, CONV_DIM), F32).at[:CONV_K].set(conv_w)
    return pl.pallas_call(
        functools.partial(_conv_kernel, ts=ts),
        grid=(b, s // ts),
        in_specs=[pl.BlockSpec((1, ts, CONV_DIM), lambda bi, i: (bi, i, 0)),
                  pl.BlockSpec((1, CONV_HALO, CONV_DIM), lambda bi, i: (bi, jnp.maximum(i * r - 1, 0), 0)),
                  pl.BlockSpec((1, CONV_HALO, CONV_DIM), lambda bi, i: (bi, jnp.minimum((i + 1) * r, nh - 1), 0)),
                  _const_spec((8, CONV_DIM)), _const_spec((1, CONV_DIM))],
        out_specs=pl.BlockSpec((1, ts, CONV_DIM), lambda bi, i: (bi, i, 0)),
        out_shape=jax.ShapeDtypeStruct((b, s, CONV_DIM), BF16),
        scratch_shapes=[pltpu.VMEM((ts + 2 * CONV_HALO, CONV_DIM), F32)],
        compiler_params=_cparams("parallel", "parallel"),
        name="conv_silu",
    )(zx_v, zx_v, zx_v, w8, conv_b.reshape(1, CONV_DIM))


def _expand_heads(v, off):
    head = lax.broadcasted_iota(I32, (1, SSD_WIDTH), 1) // HEAD_DIM
    out = jnp.zeros((v.shape[0], SSD_WIDTH), F32)
    for h in range(SSD_HEADS):
        out = jnp.where(head == h, v[:, off + h:off + h + 1], out)
    return out


def _head_selector(off):
    r = lax.broadcasted_iota(I32, (LANES, SSD_WIDTH), 0)
    c = lax.broadcasted_iota(I32, (LANES, SSD_WIDTH), 1)
    return (r == c // HEAD_DIM + off).astype(BF16)


def _expand_heads_mxu(v, sel):
    hi = v.astype(BF16)
    lo = (v - hi.astype(F32)).astype(BF16)
    return (jnp.dot(hi, sel, preferred_element_type=F32) + jnp.dot(lo, sel, preferred_element_type=F32))


def _softplus(x):
    return jnp.maximum(x, 0.0) + jnp.log(1.0 + jnp.exp(-jnp.abs(x)))


def _ssd_chunk(xc, dtr, bias, a_row, state_ref, *, reverse, off):
    Q = SSD_CHUNK
    dt = _softplus(dtr + bias)
    a = dt * a_row
    ri = lax.broadcasted_iota(I32, (Q, Q), 0)
    ci = lax.broadcasted_iota(I32, (Q, Q), 1)
    keep = (ci >= ri) if reverse else (ci <= ri)
    tri = keep.astype(BF16)
    a0 = a.astype(BF16)
    a1 = (a - a0.astype(F32)).astype(BF16)
    a2 = (a - a0.astype(F32) - a1.astype(F32)).astype(BF16)
    cum = (jnp.dot(tri, a0, preferred_element_type=F32) + jnp.dot(tri, a1, preferred_element_type=F32)
           + jnp.dot(tri, a2, preferred_element_type=F32))
    cum_t = cum.T
    edge = 0 if reverse else Q - 1
    tot = cum[edge:edge + 1, :]
    sel = _head_selector(off)
    dt512 = _expand_heads_mxu(dt, sel)
    dec512 = _expand_heads_mxu(jnp.exp(tot - cum), sel)
    ecum512 = _expand_heads_mxu(jnp.exp(cum), sel)
    etot512 = _expand_heads(jnp.exp(tot), off)
    xs = xc[:, :SSD_WIDTH].astype(F32)
    xdt = xs * dt512
    xdt_b = xdt.astype(BF16)
    xdd_b = (xdt * dec512).astype(BF16)
    gw = HEADS_PER_GROUP * HEAD_DIM
    lane = lax.broadcasted_iota(I32, (1, LANES), 1)
    even = lane < HEAD_DIM
    ys = []
    for g in range(SSD_GROUPS):
        bg = xc[:, SSD_WIDTH + g * D_STATE:SSD_WIDTH + (g + 1) * D_STATE]
        cg = xc[:, SSD_WIDTH + (SSD_GROUPS + g) * D_STATE:SSD_WIDTH + (SSD_GROUPS + g + 1) * D_STATE]
        cb = lax.dot_general(cg, bg, (((1,), (1,)), ((), ())), preferred_element_type=F32)
        sg = state_ref[g]
        yoff = jnp.dot(cg, sg.astype(BF16), preferred_element_type=F32)
        for pr in range(HEADS_PER_GROUP // 2):
            ms = []
            for hh in (2 * pr, 2 * pr + 1):
                ln = off + g * HEADS_PER_GROUP + hh
                seg = cum[:, ln:ln + 1] - cum_t[ln:ln + 1, :]
                lmat = jnp.where(keep, jnp.exp(jnp.where(keep, seg, 0.0)), 0.0)
                ms.append((cb * lmat).astype(BF16))
            c0 = g * gw + pr * LANES
            yd2 = jnp.dot(jnp.concatenate(ms, axis=0), xdt_b[:, c0:c0 + LANES], preferred_element_type=F32)
            yd = jnp.where(even, yd2[:Q], yd2[Q:])
            ys.append(yd + yoff[:, pr * LANES:(pr + 1) * LANES] * ecum512[:, c0:c0 + LANES])
        bg_t = bg.astype(F32).T.astype(BF16)
        state_ref[g] = sg * etot512[:, g * gw:(g + 1) * gw] + jnp.dot(
            bg_t, xdd_b[:, g * gw:(g + 1) * gw], preferred_element_type=F32)
    return ys


def _ssd_fwd_kernel(x_ref, dt_ref, bias_ref, a_ref, y_ref, state_ref, *, nch):
    @pl.when(pl.program_id(1) == 0)
    def _():
        state_ref[...] = jnp.zeros_like(state_ref)

    for c in range(nch):
        rows = slice(c * SSD_CHUNK, (c + 1) * SSD_CHUNK)
        ys = _ssd_chunk(x_ref[0, rows, :], dt_ref[0, rows, :], bias_ref[...], a_ref[...], state_ref,
                        reverse=False, off=0)
        for j, y in enumerate(ys):
            y_ref[0, rows, j * LANES:(j + 1) * LANES] = y


def _ssd_bwd_kernel(x_ref, dt_ref, bias_ref, a_ref, yf_ref, z_ref, dskip_ref, nw_ref, o_ref, state_ref,
                    ybuf, *, nch):
    @pl.when(pl.program_id(1) == 0)
    def _():
        state_ref[...] = jnp.zeros_like(state_ref)

    for c in reversed(range(nch)):
        rows = slice(c * SSD_CHUNK, (c + 1) * SSD_CHUNK)
        ys = _ssd_chunk(x_ref[0, rows, :], dt_ref[0, rows, :], bias_ref[...], a_ref[...], state_ref,
                        reverse=True, off=SSD_HEADS)
        for j, y in enumerate(ys):
            ybuf[rows, j * LANES:(j + 1) * LANES] = y
    xs = x_ref[0, :, :SSD_WIDTH].astype(F32)
    y = yf_ref[0] + ybuf[...] + xs * dskip_ref[...]
    gy = y * _silu(z_ref[0].astype(F32))
    gw = SSD_WIDTH // SSD_GROUPS
    for g in range(SSD_GROUPS):
        part = gy[:, g * gw:(g + 1) * gw]
        ms = jnp.mean(part * part, -1, keepdims=True)
        o_ref[0, :, g * gw:(g + 1) * gw] = (part * lax.rsqrt(ms + RMS_EPS) * nw_ref[:, g * gw:(g + 1) * gw]
                                             ).astype(o_ref.dtype)


def _ssd(xc, dt_v, zx_v, dt_bias, a_log, d_skip, norm_w, nch=8):
    b, s, _ = xc.shape
    nch = min(nch, s // SSD_CHUNK)
    R = nch * SSD_CHUNK
    n = s // R
    a_neg = -jnp.exp(a_log.astype(F32))
    pad = LANES - 2 * SSD_HEADS
    bias = jnp.pad(dt_bias.astype(F32).reshape(1, 2 * SSD_HEADS), ((0, 0), (0, pad)))
    a_f = jnp.pad(a_neg[0].reshape(1, SSD_HEADS), ((0, 0), (0, LANES - SSD_HEADS)))
    a_b = jnp.pad(a_neg[1].reshape(1, SSD_HEADS), ((0, 0), (SSD_HEADS, pad)))
    dskip = jnp.repeat(d_skip.astype(F32), HEAD_DIM).reshape(1, SSD_WIDTH)
    state = pltpu.VMEM((SSD_GROUPS, D_STATE, HEADS_PER_GROUP * HEAD_DIM), F32)
    fwd = lambda bi, i: (bi, i, 0)
    rev = lambda bi, i: (bi, n - 1 - i, 0)
    y_f = pl.pallas_call(
        functools.partial(_ssd_fwd_kernel, nch=nch),
        grid=(b, n),
        in_specs=[pl.BlockSpec((1, R, CONV_DIM), fwd), pl.BlockSpec((1, R, LANES), fwd),
                  _const_spec((1, LANES)), _const_spec((1, LANES))],
        out_specs=pl.BlockSpec((1, R, SSD_WIDTH), fwd),
        out_shape=jax.ShapeDtypeStruct((b, s, SSD_WIDTH), F32),
        scratch_shapes=[state],
        compiler_params=_cparams("parallel", "arbitrary"),
        name="ssd_fwd",
    )(xc, dt_v, bias, a_f)
    out = pl.pallas_call(
        functools.partial(_ssd_bwd_kernel, nch=nch),
        grid=(b, n),
        in_specs=[pl.BlockSpec((1, R, CONV_DIM), rev), pl.BlockSpec((1, R, LANES), rev),
                  _const_spec((1, LANES)), _const_spec((1, LANES)),
                  pl.BlockSpec((1, R, SSD_WIDTH), rev),
                  pl.BlockSpec((1, R, SSD_WIDTH), lambda bi, i: (bi, n - 1 - i, CONV_DIM // SSD_WIDTH)),
                  _const_spec((1, SSD_WIDTH)), _const_spec((1, SSD_WIDTH))],
        out_specs=pl.BlockSpec((1, R, SSD_WIDTH), rev),
        out_shape=jax.ShapeDtypeStruct((b, s, SSD_WIDTH), BF16),
        scratch_shapes=[state, pltpu.VMEM((R, SSD_WIDTH), F32)],
        compiler_params=_cparams("parallel", "arbitrary"),
        name="ssd_bwd",
    )(xc, dt_v, bias, a_b, y_f, zx_v, dskip, norm_w.astype(F32).reshape(1, SSD_WIDTH))
    return out


def _natural_order(src_ref, scr_ref):
    dil, n, w = src_ref.shape[1:]
    if dil == 1:
        return src_ref[0, 0].astype(F32)
    for r in range(dil):
        for c in range(w // LANES):
            scr_ref[c, pl.ds(r, n, stride=dil), :] = src_ref[0, r, :, c * LANES:(c + 1) * LANES].astype(F32)
    return jnp.concatenate([scr_ref[c] for c in range(w // LANES)], axis=1)


def _out_proj_kernel(ssd_ref, o0_ref, o1_ref, o2_ref, l0_ref, l1_ref, l2_ref, h_ref, w1_ref, w2_ref,
                     g_ref, b_ref, wr_ref, out_ref, id_ref, gate_ref, so1, so2, sl1, sl2):
    lses = [_natural_order(r, s) for r, s in ((l0_ref, None), (l1_ref, sl1), (l2_ref, sl2))]
    mx = jnp.maximum(jnp.maximum(lses[0], lses[1]), lses[2])
    es = [jnp.exp(l - mx) for l in lses]
    den = es[0] + es[1] + es[2]
    sel = _head_selector(0)
    att = jnp.zeros((h_ref.shape[1], ATT_WIDTH), F32)
    for e, o_ref, scr in zip(es, (o0_ref, o1_ref, o2_ref), (None, so1, so2)):
        att = att + _expand_heads_mxu(e / den, sel) * _natural_order(o_ref, scr)
    y = jnp.dot(ssd_ref[0], w1_ref[...], preferred_element_type=F32)
    y = y + jnp.dot(att.astype(BF16), w2_ref[...], preferred_element_type=F32)
    hn = _ln_rows(DN_ALPHA * h_ref[0] + y, g_ref[...], b_ref[...])
    out_ref[0] = hn
    id_ref[...], gate_ref[...] = _route(hn, wr_ref)


def _out_proj_ln(ssd, os_, lses, h, w_out, g, b, w_router, tm=1024):
    bsz, s, D = h.shape
    w0 = w_router.astype(BF16)
    w_router = jnp.concatenate([w0, (w_router - w0.astype(F32)).astype(BF16)], 0)
    n = s // tm
    route = pl.BlockSpec((8, tm), lambda bi, i: (0, bi * n + i))
    dils = [d for _, d in DILATED_PATTERNS]
    row = lambda n: pl.BlockSpec((1, tm, n), lambda bi, i: (bi, i, 0))
    sub = lambda d, n: pl.BlockSpec((1, d, tm // d, n), lambda bi, i: (bi, 0, i, 0))
    w1 = w_out[:SSD_WIDTH]
    w2 = w_out[SSD_WIDTH:]
    return pl.pallas_call(
        _out_proj_kernel,
        grid=(bsz, s // tm),
        in_specs=[row(SSD_WIDTH)] + [sub(d, ATT_WIDTH) for d in dils] + [sub(d, LANES) for d in dils]
        + [row(D), _const_spec(w1.shape), _const_spec(w2.shape), _const_spec((1, D)), _const_spec((1, D)),
           _const_spec(w_router.shape)],
        out_specs=[row(D), route, route],
        out_shape=[jax.ShapeDtypeStruct((bsz, s, D), F32), jax.ShapeDtypeStruct((8, bsz * s), I32),
                   jax.ShapeDtypeStruct((8, bsz * s), F32)],
        scratch_shapes=[pltpu.VMEM((ATT_WIDTH // LANES, tm, LANES), F32)] * 2
        + [pltpu.VMEM((1, tm, LANES), F32)] * 2,
        compiler_params=_cparams("parallel", "parallel"),
        name="out_proj_ln",
    )(ssd, *os_, *lses, h, w1, w2, g.reshape(1, D), b.reshape(1, D), w_router)


def _route(h, w_ref):
    nt = (((1,), (1,)), ((), ()))
    h0 = h.astype(BF16)
    h1 = (h - h0.astype(F32)).astype(BF16)
    nr = w_ref.shape[0] // 2
    r0 = lax.dot_general(w_ref[...], h0, nt, preferred_element_type=F32)
    r1 = lax.dot_general(w_ref[0:nr, :], h1, nt, preferred_element_type=F32)
    logits = r0[0:nr] + r0[nr:] + r1
    tm = logits.shape[1]
    row = lax.broadcasted_iota(I32, (8, tm), 0)
    lg = jnp.where(row < N_EXPERT_GROUPS, logits[0:8], NEG)
    gm = jnp.max(lg, 0, keepdims=True)
    gs = jnp.sum(jnp.exp(lg - gm), 0, keepdims=True)
    g_idx = jnp.min(jnp.where(lg == gm, row, 8), 0, keepdims=True)
    g_prob = 1.0 / gs
    el = jnp.zeros((8, tm), F32)
    for g in range(N_EXPERT_GROUPS):
        el = jnp.where(g_idx == g, logits[8 + 8 * g:16 + 8 * g], el)
    em = jnp.max(el, 0, keepdims=True)
    ee = jnp.exp(el - em)
    p = ee / jnp.sum(ee, 0, keepdims=True)
    p1 = jnp.max(p, 0, keepdims=True)
    i1 = jnp.min(jnp.where(p == p1, row, 8), 0, keepdims=True)
    pr = jnp.where(row == i1, -1.0, p)
    p2 = jnp.max(pr, 0, keepdims=True)
    i2 = jnp.min(jnp.where(pr == p2, row, 8), 0, keepdims=True)
    den = p1 + p2
    base = g_idx * EXPERTS_PER_GROUP
    ids = jnp.where(row == 0, base + i1, jnp.where(row == 1, base + i2, 0))
    gates = jnp.where(row == 0, g_prob * p1 / den, jnp.where(row == 1, g_prob * p2 / den, 0.0))
    return ids, gates


def _expert_kernel(be_ref, nu_ref, tok0_ref, tok1_ref, tok2_ref, h_hbm, wg_ref, wu_ref, wd_ref, o_ref,
                   xbuf, gsem):
    i = pl.program_id(0)
    nu = nu_ref[0]

    def gather_row(tok_ref, s, j, q=0):
        pltpu.async_copy(h_hbm.at[pl.ds(tok_ref[0, 0, j], 1)], xbuf.at[s, pl.ds(j, 1)], gsem.at[s], priority=q)

    def gather_wait(s):
        pltpu.make_async_copy(h_hbm.at[pl.ds(0, MOE_BLOCK)], xbuf.at[s], gsem.at[s]).wait()

    def rolled(ref, s):
        def body(j, c):
            gather_row(ref, s, j)
            return c
        lax.fori_loop(0, MOE_BLOCK, body, 0, unroll=8)

    @pl.when(i == 0)
    def _():
        rolled(tok0_ref, 0)
        rolled(tok1_ref, 1)

    @pl.when(i >= nu)
    def _():
        o_ref[...] = jnp.zeros(o_ref.shape, F32)

    def step(cur):
        nxt, nx2 = (cur + 1) % NBUF, (cur + 2) % NBUF
        gather_wait(cur)

        def issue(part, nparts=4):
            n = MOE_BLOCK // nparts
            for j in range(part * n, (part + 1) * n):
                gather_row(tok2_ref, nx2, j, j % 2)

        x = xbuf[cur].astype(BF16)
        issue(0)
        hg = jnp.dot(x, wg_ref[0], preferred_element_type=F32)
        issue(1)
        hu = jnp.dot(x, wu_ref[0], preferred_element_type=F32)
        issue(2)
        hdn = (_silu(hg) * hu).astype(BF16)
        issue(3)
        o_ref[...] = jnp.dot(hdn, wd_ref[0], preferred_element_type=F32)

        @pl.when(i == nu - 1)
        def _():
            gather_wait(nxt)
            gather_wait(nx2)

    for k in range(NBUF):
        pl.when((i < nu) & (i % NBUF == k))(functools.partial(step, k))


def _expert_ffn(h, slot_token, block_expert, n_used, w_gate, w_up, w_down):
    T, D = h.shape
    nb = slot_token.shape[0]
    smem = lambda f: pl.BlockSpec((1, 1, MOE_BLOCK), f, memory_space=pltpu.SMEM)
    ahead = lambda k: (lambda i, be, nu: (jnp.minimum(i + k, nb - 1), 0, 0))
    grid_spec = pltpu.PrefetchScalarGridSpec(
        num_scalar_prefetch=2,
        grid=(nb,),
        in_specs=[smem(ahead(0)), smem(ahead(1)), smem(ahead(2)),
                  pl.BlockSpec(memory_space=pl.ANY),
                  pl.BlockSpec((1, D, EXPERT_FF), lambda i, be, nu: (be[i], 0, 0)),
                  pl.BlockSpec((1, D, EXPERT_FF), lambda i, be, nu: (be[i], 0, 0)),
                  pl.BlockSpec((1, EXPERT_FF, D), lambda i, be, nu: (be[i], 0, 0))],
        out_specs=pl.BlockSpec((MOE_BLOCK, D), lambda i, be, nu: (i, 0)),
        scratch_shapes=[pltpu.VMEM((NBUF, MOE_BLOCK, D), F32), pltpu.SemaphoreType.DMA((NBUF,))],
    )
    return pl.pallas_call(
        _expert_kernel,
        grid_spec=grid_spec,
        out_shape=jax.ShapeDtypeStruct((nb * MOE_BLOCK, D), F32),
        compiler_params=_cparams("arbitrary"),
        name="expert_ffn",
    )(block_expert, n_used, slot_token, slot_token, slot_token, h, w_gate, w_up, w_down)


def _combine_kernel(dc_ref, dn_ref, h_ref, gate_ref, g_ref, b_ref, y_hbm, o_ref, yb0, yb1, sem, *, tm):
    i = pl.program_id(0)
    last = pl.num_programs(0) - 1
    ybuf = (yb0, yb1)

    def row(d_ref, s, t, k):
        pltpu.async_copy(y_hbm.at[pl.ds(d_ref[0, 0, TOP_K * t + k], 1)], ybuf[s].at[k, pl.ds(t, 1)],
                         sem.at[s], priority=k)

    def wait(s):
        for k in range(TOP_K):
            pltpu.make_async_copy(y_hbm.at[pl.ds(0, tm)], ybuf[s].at[k], sem.at[s]).wait()

    @pl.when(i == 0)
    def _():
        def body(t, c):
            for k in range(TOP_K):
                row(dc_ref, 0, t, k)
            return c
        lax.fori_loop(0, tm, body, 0, unroll=8)

    def step(cur):
        wait(cur)
        for t in range(tm):
            for k in range(TOP_K):
                row(dn_ref, 1 - cur, t, k)
        gt = gate_ref[...]
        ffn = ybuf[cur][0] * gt[:, 0:1] + ybuf[cur][1] * gt[:, 1:2]
        o_ref[...] = _ln_rows(DN_ALPHA * h_ref[...] + ffn, g_ref[...], b_ref[...])

        @pl.when(i == last)
        def _():
            wait(1 - cur)

    for s in range(2):
        pl.when(i % 2 == s)(functools.partial(step, s))


def _combine_ln(h, y, dest, gates, g, b, tm=512):
    T, D = h.shape
    n = T // tm
    dest = dest.reshape(n, 1, TOP_K * tm)
    smem = lambda f: pl.BlockSpec((1, 1, TOP_K * tm), f, memory_space=pltpu.SMEM)
    return pl.pallas_call(
        functools.partial(_combine_kernel, tm=tm),
        grid=(n,),
        in_specs=[smem(lambda i: (i, 0, 0)), smem(lambda i: (jnp.minimum(i + 1, n - 1), 0, 0)),
                  pl.BlockSpec((tm, D), lambda i: (i, 0)), pl.BlockSpec((tm, TOP_K), lambda i: (i, 0)),
                  _const_spec((1, D)), _const_spec((1, D)), pl.BlockSpec(memory_space=pl.ANY)],
        out_specs=pl.BlockSpec((tm, D), lambda i: (i, 0)),
        out_shape=jax.ShapeDtypeStruct((T, D), F32),
        scratch_shapes=[pltpu.VMEM((TOP_K, tm, D), F32)] * 2 + [pltpu.SemaphoreType.DMA((2,))],
        compiler_params=_cparams("arbitrary"),
        name="combine_ln",
    )(dest, dest, h, gates, g.reshape(1, D), b.reshape(1, D), y)


def _mixer(h, lw, tabs):
    *atts, zx, dt_raw = _in_proj(h, lw["w_att"], lw["w_zx"], lw["w_dt"], tabs)
    os_, lses = [], []
    for gi in range(N_DIL):
        o, lse = _dilated_attention(atts[gi], gi)
        os_.append(o)
        lses.append(lse)
    xc = _conv_silu(zx, lw["conv_w"], lw["conv_b"])
    ssd = _ssd(xc, dt_raw, zx, lw["dt_bias"], lw["a_log"], lw["d_skip"], lw["ssd_norm_w"])
    return _out_proj_ln(ssd, os_, lses, h, lw["w_out"], lw["ln1_g"], lw["ln1_b"], lw["w_router"])


def _moe(h, ids, gates, lw):
    T, D = h.shape
    A = T * TOP_K
    expert = ids[:TOP_K].T.reshape(A)
    order = jnp.argsort(expert, stable=True).astype(I32)
    inv = jnp.argsort(order).astype(I32)
    onehot = expert[:, None] == jnp.arange(N_EXPERTS, dtype=I32)[None, :]
    counts = jnp.sum(onehot, 0, dtype=I32)
    padded = (counts + MOE_BLOCK - 1) // MOE_BLOCK * MOE_BLOCK
    pad_end = jnp.cumsum(padded)
    pad_start = pad_end - padded
    start = jnp.cumsum(counts) - counts
    dest = inv + jnp.sum(jnp.where(onehot, (pad_start - start)[None, :], 0), -1, dtype=I32)
    n_blocks = -(-A // MOE_BLOCK) + N_EXPERTS
    blk0 = jnp.arange(n_blocks, dtype=I32) * MOE_BLOCK
    block_expert = jnp.minimum(jnp.sum(pad_end[None, :] <= blk0[:, None], -1, dtype=I32), N_EXPERTS - 1)
    n_used = (pad_end[-1] // MOE_BLOCK).astype(I32).reshape(1)
    sel = block_expert[:, None] == jnp.arange(N_EXPERTS, dtype=I32)[None, :]
    pick = lambda v: jnp.sum(jnp.where(sel, v[None, :], 0), -1, dtype=I32)[:, None]
    off = blk0[:, None] - pick(pad_start) + jnp.arange(MOE_BLOCK, dtype=I32)[None, :]
    asg = order[jnp.clip(pick(start) + off, 0, A - 1)]
    slot_token = jnp.where(off < pick(counts), asg // TOP_K, 0).reshape(n_blocks, 1, MOE_BLOCK)
    y = _expert_ffn(h, slot_token, block_expert, n_used, lw["w_gate"], lw["w_up"], lw["w_down"])
    return _combine_ln(h, y, dest, gates[:TOP_K].T, lw["ln2_g"], lw["ln2_b"])


def _trunk(x, ln_in_g, ln_in_b, layers):
    b, s, D = x.shape
    tabs = _rope_tables(s)
    h = _layer_norm(x.reshape(b * s, D), ln_in_g, ln_in_b)
    for lw in layers:
        h, ids, gates = _mixer(h.reshape(b, s, D), lw, tabs)
        h = _moe(h.reshape(b * s, D), ids, gates, lw)
    return h.reshape(b, s, D)


def _prep_layers(w_in, conv_w, conv_b, a_log, dt_bias, d_skip, ssd_norm_w, w_out, ln1_g, ln1_b,
                 router_group, router_expert, w_gate, w_up, w_down, ln2_g, ln2_b):
    layers = []
    z0 = ATT_PROJ
    x0 = ATT_PROJ + SSD_WIDTH
    d0 = x0 + CONV_DIM
    for i in range(w_in.shape[0]):
        w = w_in[i]
        w_dt = jnp.pad(w[:, d0:], ((0, 0), (0, LANES - 2 * SSD_HEADS))).astype(BF16)
        w_router = jnp.concatenate([
            router_group[i].T, jnp.zeros((8 - N_EXPERT_GROUPS, D_MODEL), F32), router_expert[i].T], 0)
        layers.append(dict(
            w_att=w[:, :z0].astype(BF16),
            w_zx=jnp.concatenate([w[:, x0:d0], w[:, z0:x0]], 1).astype(BF16),
            w_dt=w_dt,
            conv_w=conv_w[i], conv_b=conv_b[i], a_log=a_log[i], dt_bias=dt_bias[i], d_skip=d_skip[i],
            ssd_norm_w=ssd_norm_w[i], w_out=w_out[i].astype(BF16), ln1_g=ln1_g[i], ln1_b=ln1_b[i],
            w_router=w_router, w_gate=w_gate[i].astype(BF16), w_up=w_up[i].astype(BF16),
            w_down=w_down[i].astype(BF16), ln2_g=ln2_g[i], ln2_b=ln2_b[i]))
    return layers


def kernel(x_prompt, x_sample, ln_in_g, ln_in_b, w_in, conv_w, conv_b, a_log, dt_bias, d_skip, ssd_norm_w,
           w_out, ln1_g, ln1_b, router_group, router_expert, w_gate, w_up, w_down, ln2_g, ln2_b):
    layers = _prep_layers(w_in, conv_w, conv_b, a_log, dt_bias, d_skip, ssd_norm_w, w_out, ln1_g, ln1_b,
                          router_group, router_expert, w_gate, w_up, w_down, ln2_g, ln2_b)
    y_prompt = _trunk(x_prompt, ln_in_g, ln_in_b, layers)
    y_sample = _trunk(x_sample, ln_in_g, ln_in_b, layers)
    return (y_prompt, y_sample)
```
